```python
import jax, jax.numpy as jnp
from jax import lax
import numpy as np

D_MODEL = 1024
BATCH = 8
SEQ = 2048
DEPTH = 2
DEC_BATCH = 128
DEC_SEQ = 8
PAST_LEN = 16384
PAGE_SIZE = 128

N_META = 16
N_MIXERS = 2
N_GDN_LAYERS = (DEPTH + 1) // 2
N_RWKV_LAYERS = DEPTH // 2
GDN_QK_HEADS = 8
GDN_V_HEADS = 16
GDN_HEAD_K = 128
GDN_HEAD_V = 128
GDN_KEY_DIM = GDN_QK_HEADS * GDN_HEAD_K
GDN_VALUE_DIM = GDN_V_HEADS * GDN_HEAD_V
GDN_CONV_DIM = 2 * GDN_KEY_DIM + GDN_VALUE_DIM
GDN_IN_DIM = GDN_CONV_DIM + GDN_VALUE_DIM + 2 * GDN_V_HEADS
CONV_W = 4
CHUNK = 64
RWKV_HEAD = 64
RWKV_HEADS = D_MODEL // RWKV_HEAD
DECAY_LORA = 64
A_LORA = 64
RMS_EPS = 1e-6
GN_EPS = 64e-5

kernel_name = "gdn_rwkv7_hybrid_step"


def rmsnorm(x, w):
    xf = x.astype(jnp.float32)
    return xf * lax.rsqrt(jnp.mean(xf * xf, -1, keepdims=True) + RMS_EPS) * w.astype(jnp.float32)


def l2norm(x, eps=1e-6):
    return x * lax.rsqrt(jnp.sum(x * x, -1, keepdims=True) + eps)


def causal_conv(u, buf, w):
    T = u.shape[1]
    up = jnp.concatenate([buf.astype(u.dtype), u], 1)
    y = up[:, 0:T] * w[:, 0]
    for j in range(1, CONV_W):
        y = y + up[:, j:j + T] * w[:, j]
    return jax.nn.silu(y), up[:, up.shape[1] - (CONV_W - 1):]


def gdn_chunked(q, k, v, g, beta, S0, chunk):
    B, T, H, DK = q.shape
    DV = v.shape[-1]
    n = T // chunk

    def blocks(t):
        t = t.reshape((B, n, chunk) + t.shape[2:])
        return t.transpose((1, 0, 3, 2) + tuple(range(4, t.ndim)))

    q, k, v, g, beta = blocks(q), blocks(k), blocks(v), blocks(g), blocks(beta)
    gc = jnp.cumsum(g, axis=-1)
    causal = jnp.tril(jnp.ones((chunk, chunk), bool))
    strict = jnp.tril(jnp.ones((chunk, chunk), bool), -1)
    decay = jnp.exp(jnp.where(causal, gc[..., :, None] - gc[..., None, :], -jnp.inf))
    kb = k * beta[..., None]
    vb = v * beta[..., None]
    L = jnp.where(strict, jnp.einsum('nbhik,nbhjk->nbhij', kb, k) * decay, 0.0)
    A = L + jnp.eye(chunk, dtype=L.dtype)
    rhs = jnp.concatenate([vb, kb * jnp.exp(gc)[..., None]], -1)
    sol = lax.linalg.triangular_solve(A, rhs, left_side=True, lower=True, unit_diagonal=True)
    u, wk = sol[..., :DV], sol[..., DV:]
    attn = jnp.einsum('nbhik,nbhjk->nbhij', q, k) * decay
    g_last = gc[..., -1]
    k_dec = k * jnp.exp(g_last[..., None] - gc)[..., None]
    q_dec = q * jnp.exp(gc)[..., None]

    def step(S, inp):
        u_c, w_c, a_c, qd_c, kd_c, gl_c = inp
        v_new = u_c - jnp.einsum('bhck,bhkv->bhcv', w_c, S)
        o = jnp.einsum('bhck,bhkv->bhcv', qd_c, S) + jnp.einsum('bhij,bhjv->bhiv', a_c, v_new)
        S = S * jnp.exp(gl_c)[..., None, None] + jnp.einsum('bhck,bhcv->bhkv', kd_c, v_new)
        return S, o

    S, o = lax.scan(step, S0.astype(jnp.float32), (u, wk, attn, q_dec, k_dec, g_last))
    return o.transpose(1, 0, 3, 2, 4).reshape(B, T, H, DV), S


def gdn_mixer(xn, S0, conv_buf, w_in, conv_w, a_log, dt_bias, gn_w, w_out, segments):
    B, T, _ = xn.shape
    proj = xn @ w_in
    c1 = GDN_CONV_DIM
    c2 = c1 + GDN_VALUE_DIM
    c3 = c2 + GDN_V_HEADS
    qkv, z, b, a = proj[..., :c1], proj[..., c1:c2], proj[..., c2:c3], proj[..., c3:]
    qkv, new_buf = causal_conv(qkv, conv_buf, conv_w)
    rep = GDN_V_HEADS // GDN_QK_HEADS
    q = jnp.repeat(l2norm(qkv[..., :GDN_KEY_DIM].reshape(B, T, GDN_QK_HEADS, GDN_HEAD_K)), rep, axis=2) * (GDN_HEAD_K ** -0.5)
    k = jnp.repeat(l2norm(qkv[..., GDN_KEY_DIM:2 * GDN_KEY_DIM].reshape(B, T, GDN_QK_HEADS, GDN_HEAD_K)), rep, axis=2)
    v = qkv[..., 2 * GDN_KEY_DIM:].reshape(B, T, GDN_V_HEADS, GDN_HEAD_V)
    beta = jax.nn.sigmoid(b)
    g = -jnp.exp(a_log.astype(jnp.float32)) * jax.nn.softplus(a + dt_bias)
    S = S0
    outs = []
    start = 0
    for length, chunk in segments:
        o_seg, S = gdn_chunked(q[:, start:start + length], k[:, start:start + length], v[:, start:start + length],
                               g[:, start:start + length], beta[:, start:start + length], S, chunk)
        outs.append(o_seg)
        start += length
    o = jnp.concatenate(outs, 1)
    o = rmsnorm(o, gn_w) * jax.nn.silu(z.reshape(B, T, GDN_V_HEADS, GDN_HEAD_V))
    return o.reshape(B, T, GDN_VALUE_DIM) @ w_out, S, new_buf


def rwkv_mixer(xn, S0, shift_prev, mu, w_rkvz, w0, w1, w2, a0, a1, a2, k_k, k_a, r_k, lnx_w, lnx_b, w_o):
    B, T, D = xn.shape
    H, N = RWKV_HEADS, RWKV_HEAD
    xprev = jnp.concatenate([shift_prev[:, None].astype(jnp.float32), xn[:, :-1]], 1)
    xx = xprev - xn
    xs = xn[None] + xx[None] * mu[:, None, None, :]
    rkvz = jnp.einsum('sbtd,sde->sbte', xs[:4], w_rkvz)
    r, k, v, z = rkvz[0], rkvz[1], rkvz[2], rkvz[3]
    w = -jax.nn.softplus(-(w0 + jnp.tanh(xs[4] @ w1) @ w2)) - 0.5
    a = jax.nn.sigmoid(a0 + (xs[5] @ a1) @ a2)
    kk = l2norm((k * k_k).reshape(B, T, H, N))
    k = k * (1.0 + (a - 1.0) * k_a)
    decay = jnp.exp(-jnp.exp(w))
    hs = lambda t: t.reshape(B, T, H, N)
    r_h, k_h, v_h, a_h, d_h = hs(r), hs(k), hs(v), hs(a), hs(decay)
    b_h = kk * a_h
    tm = lambda t: t.transpose(1, 0, 2, 3)

    def step(S, inp):
        r_t, k_t, v_t, d_t, kk_t, b_t = inp
        S = (S * d_t[:, :, None, :]
             + jnp.einsum('bhvk,bhk->bhv', S, -kk_t)[..., None] * b_t[:, :, None, :]
             + v_t[..., None] * k_t[:, :, None, :])
        return S, jnp.einsum('bhvk,bhk->bhv', S, r_t)

    S, y = lax.scan(step, S0.astype(jnp.float32), (tm(r_h), tm(k_h), tm(v_h), tm(d_h), tm(kk), tm(b_h)))
    y = y.transpose(1, 0, 2, 3)
    mean = jnp.mean(y, -1, keepdims=True)
    var = jnp.mean(jnp.square(y - mean), -1, keepdims=True)
    y = (y - mean) * lax.rsqrt(var + GN_EPS) * lnx_w.reshape(H, N) + lnx_b.reshape(H, N)
    y = y + jnp.sum(r_h * k_h * r_k, -1, keepdims=True) * v_h
    y = y.reshape(B, T, D) * jax.nn.silu(z)
    return y @ w_o, S, xn[:, -1]


def trunk(x, gdn_S, gdn_conv, rwkv_S, rwkv_shift, norm_w, final_norm_w, gdn_p, rwkv_p, segments):
    new_gdn_S, new_gdn_conv, new_rwkv_S, new_rwkv_shift = [], [], [], []
    for i in range(DEPTH):
        xn = rmsnorm(x, norm_w[i])
        j = i // N_MIXERS
        if i % N_MIXERS == 0:
            out, S, buf = gdn_mixer(xn, gdn_S[j], gdn_conv[j], *[p[j] for p in gdn_p], segments)
            new_gdn_S.append(S)
            new_gdn_conv.append(buf)
        else:
            out, S, sh = rwkv_mixer(xn, rwkv_S[j], rwkv_shift[j], *[p[j] for p in rwkv_p])
            new_rwkv_S.append(S)
            new_rwkv_shift.append(sh)
        x = x + out.astype(x.dtype)
    y = rmsnorm(x, final_norm_w).astype(x.dtype)
    return y, jnp.stack(new_gdn_S), jnp.stack(new_gdn_conv), jnp.stack(new_rwkv_S), jnp.stack(new_rwkv_shift)


def setup_inputs(seed: int = 0) -> dict:
    key = jax.random.key(seed)
    ks = jax.random.split(key, 32)
    nrm = jax.random.normal
    uni = jax.random.uniform
    D = D_MODEL
    NA, NB = N_GDN_LAYERS, N_RWKV_LAYERS
    dt = jnp.exp(uni(ks[12], (NA, GDN_V_HEADS), minval=float(np.log(1e-3)), maxval=float(np.log(1e-1))))
    return {
        'x_prompt': nrm(ks[0], (BATCH, SEQ, D)),
        'x_sample': nrm(ks[1], (DEC_BATCH, DEC_SEQ, D)),
        'state_gdn': 0.05 * nrm(ks[2], (NA, DEC_BATCH, GDN_V_HEADS, GDN_HEAD_K, GDN_HEAD_V)),
        'state_gdn_conv': nrm(ks[3], (NA, DEC_BATCH, CONV_W - 1, GDN_CONV_DIM)),
        'state_rwkv': 0.1 * nrm(ks[4], (NB, DEC_BATCH, RWKV_HEADS, RWKV_HEAD, RWKV_HEAD)),
        'state_rwkv_shift': nrm(ks[5], (NB, DEC_BATCH, D)),
        'meta_tokens': nrm(ks[6], (N_META, D)),
        'norm_w': 1.0 + 0.01 * nrm(ks[7], (DEPTH, D)),
        'final_norm_w': 1.0 + 0.01 * nrm(ks[8], (D,)),
        'gdn_w_in': nrm(ks[9], (NA, D, GDN_IN_DIM)) * D ** -0.5,
        'gdn_conv_w': nrm(ks[10], (NA, GDN_CONV_DIM, CONV_W)) * CONV_W ** -0.5,
        'gdn_a_log': jnp.log(uni(ks[11], (NA, GDN_V_HEADS), minval=1.0, maxval=16.0)),
        'gdn_dt_bias': dt + jnp.log(-jnp.expm1(-dt)),
        'gdn_norm_w': 1.0 + 0.01 * nrm(ks[13], (NA, GDN_HEAD_V)),
        'gdn_w_out': nrm(ks[14], (NA, GDN_VALUE_DIM, D)) * GDN_VALUE_DIM ** -0.5,
        'rwkv_mu': uni(ks[15], (NB, 6, D)),
        'rwkv_w_rkvz': nrm(ks[16], (NB, 4, D, D)) * D ** -0.5,
        'rwkv_w0': uni(ks[17], (NB, D), minval=-6.0, maxval=-1.0),
        'rwkv_w1': 0.1 * nrm(ks[18], (NB, D, DECAY_LORA)) * D ** -0.5,
        'rwkv_w2': 0.1 * nrm(ks[19], (NB, DECAY_LORA, D)) * DECAY_LORA ** -0.5,
        'rwkv_a0': 0.1 * nrm(ks[20], (NB, D)),
        'rwkv_a1': nrm(ks[21], (NB, D, A_LORA)) * D ** -0.5,
        'rwkv_a2': 0.1 * nrm(ks[22], (NB, A_LORA, D)) * A_LORA ** -0.5,
        'rwkv_k_k': 0.85 + 0.05 * nrm(ks[23], (NB, D)),
        'rwkv_k_a': 1.0 + 0.05 * nrm(ks[24], (NB, D)),
        'rwkv_r_k': 0.1 * nrm(ks[25], (NB, RWKV_HEADS, RWKV_HEAD)),
        'rwkv_lnx_w': 1.0 + 0.01 * nrm(ks[26], (NB, D)),
        'rwkv_lnx_b': 0.01 * nrm(ks[27], (NB, D)),
        'rwkv_w_o': nrm(ks[28], (NB, D, D)) * D ** -0.5,
    }


def reference(x_prompt, x_sample, state_gdn, state_gdn_conv, state_rwkv, state_rwkv_shift,
              meta_tokens, norm_w, final_norm_w,
              gdn_w_in, gdn_conv_w, gdn_a_log, gdn_dt_bias, gdn_norm_w, gdn_w_out,
              rwkv_mu, rwkv_w_rkvz, rwkv_w0, rwkv_w1, rwkv_w2, rwkv_a0, rwkv_a1, rwkv_a2,
              rwkv_k_k, rwkv_k_a, rwkv_r_k, rwkv_lnx_w, rwkv_lnx_b, rwkv_w_o):
    gdn_p = (gdn_w_in, gdn_conv_w, gdn_a_log, gdn_dt_bias, gdn_norm_w, gdn_w_out)
    rwkv_p = (rwkv_mu, rwkv_w_rkvz, rwkv_w0, rwkv_w1, rwkv_w2, rwkv_a0, rwkv_a1, rwkv_a2,
              rwkv_k_k, rwkv_k_a, rwkv_r_k, rwkv_lnx_w, rwkv_lnx_b, rwkv_w_o)
    f32 = jnp.float32
    Bp, Tp, _ = x_prompt.shape
    meta = jnp.broadcast_to(meta_tokens.astype(x_prompt.dtype)[None], (Bp, N_META, D_MODEL))
    xp = jnp.concatenate([meta, x_prompt], 1)
    z_gdn = jnp.zeros((N_GDN_LAYERS, Bp, GDN_V_HEADS, GDN_HEAD_K, GDN_HEAD_V), f32)
    z_conv = jnp.zeros((N_GDN_LAYERS, Bp, CONV_W - 1, GDN_CONV_DIM), f32)
    z_rwkv = jnp.zeros((N_RWKV_LAYERS, Bp, RWKV_HEADS, RWKV_HEAD, RWKV_HEAD), f32)
    z_shift = jnp.zeros((N_RWKV_LAYERS, Bp, D_MODEL), f32)
    y_p, p_gdn, p_gdn_conv, p_rwkv, p_rwkv_shift = trunk(
        xp, z_gdn, z_conv, z_rwkv, z_shift, norm_w, final_norm_w, gdn_p, rwkv_p,
        ((N_META, N_META), (Tp, CHUNK)))
    y_prompt = y_p[:, N_META:]
    Ts = x_sample.shape[1]
    y_sample, s_gdn, s_gdn_conv, s_rwkv, s_rwkv_shift = trunk(
        x_sample, state_gdn, state_gdn_conv, state_rwkv, state_rwkv_shift, norm_w, final_norm_w,
        gdn_p, rwkv_p, ((Ts, Ts),))
    return (y_prompt, y_sample, p_gdn, p_gdn_conv, p_rwkv, p_rwkv_shift, s_gdn, s_gdn_conv, s_rwkv, s_rwkv_shift)
```

```python
import functools

import jax
import jax.numpy as jnp
from jax import lax
from jax.experimental import pallas as pl
from jax.experimental.pallas import tpu as pltpu

F32 = jnp.float32
BF16 = jnp.bfloat16

D_MODEL = 1024
N_META = 16
GDN_QK_HEADS = 8
GDN_V_HEADS = 16
GDN_HEAD = 128
GDN_KEY_DIM = GDN_QK_HEADS * GDN_HEAD
GDN_VALUE_DIM = GDN_V_HEADS * GDN_HEAD
GDN_CONV_DIM = 2 * GDN_KEY_DIM + GDN_VALUE_DIM
GDN_QKVZ_DIM = GDN_CONV_DIM + GDN_VALUE_DIM
CONV_W = 4
GDN_CHUNK = 64
RWKV_HEAD = 64
RWKV_HEADS = D_MODEL // RWKV_HEAD
RWKV_PAIRS = RWKV_HEADS // 2
RWKV_CHUNK = 64
LORA = 64
RMS_EPS = 1e-6
L2_EPS = 1e-6
GN_EPS = 64e-5
LANES = 128
SUBLANES = 8
VMEM_LIMIT_BYTES = 56 * 1024 * 1024


def _bdot(a, b):
    return jnp.dot(a.astype(BF16), b.astype(BF16), preferred_element_type=F32)


def _bdot_nt(a, b):
    return lax.dot_general(a.astype(BF16), b.astype(BF16), (((1,), (1,)), ((), ())), preferred_element_type=F32)


def _bdot_tn(a, b):
    return lax.dot_general(a.astype(BF16), b.astype(BF16), (((0,), (0,)), ((), ())), preferred_element_type=F32)


def _split2(a):
    hi = a.astype(BF16)
    lo = (a - hi.astype(F32)).astype(BF16)
    return hi, lo


def _split3(a):
    hi = a.astype(BF16)
    r1 = a - hi.astype(F32)
    mid = r1.astype(BF16)
    lo = (r1 - mid.astype(F32)).astype(BF16)
    return hi, mid, lo


def _dot3(a, b):
    ah, al = _split2(a)
    bh, bl = _split2(b)
    d = functools.partial(jnp.dot, preferred_element_type=F32)
    return d(ah, bh) + (d(ah, bl) + d(al, bh))


def _dot_exact_lhs(m_bf16, x, dims):
    xs = _split3(x)
    out = None
    for part in xs:
        if dims == "nn":
            t = jnp.dot(m_bf16, part, preferred_element_type=F32)
        else:
            t = lax.dot_general(part, m_bf16, (((0,), (1,)), ((), ())), preferred_element_type=F32)
        out = t if out is None else out + t
    return out


def _rms(x, w):
    return x * lax.rsqrt(jnp.mean(x * x, axis=-1, keepdims=True) + RMS_EPS) * w


def _silu(x):
    return x * jax.nn.sigmoid(x)


def _softplus(x):
    return jnp.maximum(x, 0.0) + jnp.log1p(jnp.exp(-jnp.abs(x)))


def _iota2(n, m):
    return lax.broadcasted_iota(jnp.int32, (n, m), 0), lax.broadcasted_iota(jnp.int32, (n, m), 1)


def _log2(n):
    l = n.bit_length() - 1
    assert (1 << l) == n, n
    return l


def _tri_inv(low, n, blk, dot):
    row, col = _iota2(n, n)
    eye = jnp.where(row == col, 1.0, 0.0).astype(F32)
    s = 1
    t = None
    while s < blk:
        sh = _log2(s)
        sub_r = row >> sh
        sub_c = col >> sh
        m = ((sub_r >> 1) == (sub_c >> 1)) & ((sub_r & 1) == 1) & ((sub_c & 1) == 0)
        off = jnp.where(m, low, 0.0)
        if t is None:
            t = eye - off
        else:
            t = t - dot(dot(t, off), t)
        s *= 2
    return eye if t is None else t


def _gdn_in_kernel(x_ref, nw_ref, w_ref, wba_ref, qkvz_ref, ba_ref, xn_scr):
    n = pl.program_id(1)

    @pl.when(n == 0)
    def _():
        xn = _rms(x_ref[...], nw_ref[...])
        xb = xn.astype(BF16)
        xn_scr[...] = xb
        ba_ref[...] = jnp.dot(xb, wba_ref[...], preferred_element_type=F32)

    qkvz_ref[...] = jnp.dot(xn_scr[...], w_ref[...], preferred_element_type=F32)


def _gdn_in(x2d, nw, w_qkvz, w_ba, tm, tn):
    m = x2d.shape[0]
    return pl.pallas_call(
        _gdn_in_kernel,
        grid=(m // tm, GDN_QKVZ_DIM // tn),
        in_specs=[
            pl.BlockSpec((tm, D_MODEL), lambda i, j: (i, 0)),
            pl.BlockSpec((1, D_MODEL), lambda i, j: (0, 0)),
            pl.BlockSpec((D_MODEL, tn), lambda i, j: (0, j)),
            pl.BlockSpec((D_MODEL, LANES), lambda i, j: (0, 0)),
        ],
        out_specs=[
            pl.BlockSpec((tm, tn), lambda i, j: (i, j)),
            pl.BlockSpec((tm, LANES), lambda i, j: (i, 0)),
        ],
        out_shape=[
            jax.ShapeDtypeStruct((m, GDN_QKVZ_DIM), F32),
            jax.ShapeDtypeStruct((m, LANES), F32),
        ],
        scratch_shapes=[pltpu.VMEM((tm, D_MODEL), BF16)],
        compiler_params=pltpu.CompilerParams(
            dimension_semantics=("parallel", "arbitrary"), vmem_limit_bytes=VMEM_LIMIT_BYTES),
        name="gdn_in",
    )(x2d, nw, w_qkvz, w_ba)


def _gdn_chunk_kernel(qkv_ref, ba_ref, buf_ref, s0_ref, cw_ref, alog_ref, dtb_ref,
                      o_ref, sout_ref, nbuf_ref, s_scr, up_scr, *, chunk, n_chunks):
    c = pl.program_id(1)
    C = chunk
    C2 = 2 * C
    pad = SUBLANES - (CONV_W - 1)

    @pl.when(c == 0)
    def _():
        s_scr[...] = s0_ref[0]
        up_scr[0:SUBLANES, :] = jnp.zeros((SUBLANES, GDN_CONV_DIM), F32)
        up_scr[pad:SUBLANES, :] = buf_ref[0]

    up_scr[SUBLANES:SUBLANES + C, :] = qkv_ref[0]
    y = up_scr[pad:pad + C, :] * cw_ref[0:1, :]
    for j in range(1, CONV_W):
        y = y + up_scr[pad + j:pad + j + C, :] * cw_ref[j:j + 1, :]
    act = _silu(y)
    tail = up_scr[C:C + SUBLANES, :]
    up_scr[0:SUBLANES, :] = tail

    @pl.when(c == n_chunks - 1)
    def _():
        nbuf_ref[0] = tail[pad:SUBLANES, :]

    ba = ba_ref[0]
    beta_all = jax.nn.sigmoid(ba)
    g_all = -jnp.exp(alog_ref[...]) * _softplus(ba + dtb_ref[...])
    r1, c1 = _iota2(C, C)
    tril = jnp.where(r1 >= c1, 1.0, 0.0).astype(BF16)
    gc_all = _dot_exact_lhs(tril, g_all, "nn")
    r2, c2 = _iota2(C2, C)
    tril2 = jnp.where((r2 & (C - 1)) >= c2, 1.0, 0.0).astype(BF16)
    gct_all = _dot_exact_lhs(tril2, g_all, "tn")

    row, col = _iota2(C2, C2)
    same = (row >= C) == (col >= C)
    causal = same & (row >= col)
    strict = same & (row > col)
    top = lax.broadcasted_iota(jnp.int32, (C2, 1), 0) < C
    left = lax.broadcasted_iota(jnp.int32, (1, C2), 1) < C

    for qh in range(GDN_QK_HEADS):
        q = act[:, qh * GDN_HEAD:(qh + 1) * GDN_HEAD]
        k = act[:, GDN_KEY_DIM + qh * GDN_HEAD:GDN_KEY_DIM + (qh + 1) * GDN_HEAD]
        q = q * lax.rsqrt(jnp.sum(q * q, axis=-1, keepdims=True) + L2_EPS) * (GDN_HEAD ** -0.5)
        k = k * lax.rsqrt(jnp.sum(k * k, axis=-1, keepdims=True) + L2_EPS)
        h0 = 2 * qh
        h1 = h0 + 1
        a0, a1 = GDN_V_HEADS + h0, GDN_V_HEADS + h1
        beta_s = jnp.concatenate([beta_all[:, h0:h0 + 1], beta_all[:, h1:h1 + 1]], axis=0)
        gc_s = jnp.concatenate([gc_all[:, a0:a0 + 1], gc_all[:, a1:a1 + 1]], axis=0)
        gc_row = jnp.where(left, gct_all[a0:a0 + 1, :], gct_all[a1:a1 + 1, :])
        glast0 = gc_all[C - 1:C, a0:a0 + 1]
        glast1 = gc_all[C - 1:C, a1:a1 + 1]
        glast_s = jnp.where(top, glast0, glast1)
        decay = jnp.where(causal, jnp.exp(jnp.minimum(gc_s - gc_row, 0.0)), 0.0)
        k_s = jnp.concatenate([k, k], axis=0)
        q_s = jnp.concatenate([q, q], axis=0)
        v_s = jnp.concatenate([act[:, 2 * GDN_KEY_DIM + h0 * GDN_HEAD:2 * GDN_KEY_DIM + (h0 + 1) * GDN_HEAD],
                               act[:, 2 * GDN_KEY_DIM + h1 * GDN_HEAD:2 * GDN_KEY_DIM + (h1 + 1) * GDN_HEAD]], axis=0)
        kk = _bdot_nt(k_s, k_s)
        low = jnp.where(strict, kk * beta_s * decay, 0.0)
        tinv = _tri_inv(low, C2, C, _dot3)
        egc = jnp.exp(gc_s)
        rhs = jnp.concatenate([v_s * beta_s, k_s * (beta_s * egc)], axis=1)
        sol = _dot3(tinv, rhs)
        u_s = sol[:, :GDN_HEAD]
        w_s = sol[:, GDN_HEAD:]
        attn = _bdot_nt(q_s, k_s) * decay
        qd_s = q_s * egc
        kd_s = k_s * jnp.exp(glast_s - gc_s)
        st0 = s_scr[h0]
        st1 = s_scr[h1]
        vn0 = u_s[:C] - _bdot(w_s[:C], st0)
        vn1 = u_s[C:] - _bdot(w_s[C:], st1)
        vn_s = jnp.concatenate([vn0, vn1], axis=0)
        o_s = _bdot(attn, vn_s) + jnp.concatenate([_bdot(qd_s[:C], st0), _bdot(qd_s[C:], st1)], axis=0)
        o_ref[0, :, h0 * GDN_HEAD:(h0 + 1) * GDN_HEAD] = o_s[:C]
        o_ref[0, :, h1 * GDN_HEAD:(h1 + 1) * GDN_HEAD] = o_s[C:]
        s_scr[h0] = st0 * jnp.exp(glast0) + _bdot_tn(kd_s[:C], vn0)
        s_scr[h1] = st1 * jnp.exp(glast1) + _bdot_tn(kd_s[C:], vn1)

    @pl.when(c == n_chunks - 1)
    def _():
        sout_ref[0] = s_scr[...]


def _gdn_chunk(qkvz, ba, conv_buf, s0, conv_wt, alog_row, dtb_row, chunk):
    b, t, _ = qkvz.shape
    n_chunks = t // chunk
    kern = functools.partial(_gdn_chunk_kernel, chunk=chunk, n_chunks=n_chunks)
    return pl.pallas_call(
        kern,
        grid=(b, n_chunks),
        in_specs=[
            pl.BlockSpec((1, chunk, GDN_CONV_DIM), lambda i, c: (i, c, 0)),
            pl.BlockSpec((1, chunk, LANES), lambda i, c: (i, c, 0)),
            pl.BlockSpec((1, CONV_W - 1, GDN_CONV_DIM), lambda i, c: (i, 0, 0)),
            pl.BlockSpec((1, GDN_V_HEADS, GDN_HEAD, GDN_HEAD), lambda i, c: (i, 0, 0, 0)),
            pl.BlockSpec((CONV_W, GDN_CONV_DIM), lambda i, c: (0, 0)),
            pl.BlockSpec((1, LANES), lambda i, c: (0, 0)),
            pl.BlockSpec((1, LANES), lambda i, c: (0, 0)),
        ],
        out_specs=[
            pl.BlockSpec((1, chunk, GDN_VALUE_DIM), lambda i, c: (i, c, 0)),
            pl.BlockSpec((1, GDN_V_HEADS, GDN_HEAD, GDN_HEAD), lambda i, c: (i, 0, 0, 0)),
            pl.BlockSpec((1, CONV_W - 1, GDN_CONV_DIM), lambda i, c: (i, 0, 0)),
        ],
        out_shape=[
            jax.ShapeDtypeStruct((b, t, GDN_VALUE_DIM), F32),
            jax.ShapeDtypeStruct((b, GDN_V_HEADS, GDN_HEAD, GDN_HEAD), F32),
            jax.ShapeDtypeStruct((b, CONV_W - 1, GDN_CONV_DIM), F32),
        ],
        scratch_shapes=[
            pltpu.VMEM((GDN_V_HEADS, GDN_HEAD, GDN_HEAD), F32),
            pltpu.VMEM((chunk + SUBLANES, GDN_CONV_DIM), F32),
        ],
        compiler_params=pltpu.CompilerParams(
            dimension_semantics=("parallel", "arbitrary"), vmem_limit_bytes=VMEM_LIMIT_BYTES),
        name="gdn_chunk",
    )(qkvz, ba, conv_buf, s0, conv_wt, alog_row, dtb_row)


def _gdn_out_kernel(o_ref, z_ref, x_ref, gw_ref, w_ref, y_ref):
    gw = gw_ref[...]
    parts = []
    for h in range(GDN_V_HEADS):
        sl = slice(h * GDN_HEAD, (h + 1) * GDN_HEAD)
        parts.append((_rms(o_ref[:, sl], gw) * _silu(z_ref[:, sl])).astype(BF16))
    gated = jnp.concatenate(parts, axis=1)
    y_ref[...] = x_ref[...] + jnp.dot(gated, w_ref[...], preferred_element_type=F32)


def _gdn_out(o2d, qkvz2d, x2d, gw, w_out, tm):
    m = x2d.shape[0]
    z_block = GDN_CONV_DIM // GDN_VALUE_DIM
    return pl.pallas_call(
        _gdn_out_kernel,
        grid=(m // tm,),
        in_specs=[
            pl.BlockSpec((tm, GDN_VALUE_DIM), lambda i: (i, 0)),
            pl.BlockSpec((tm, GDN_VALUE_DIM), lambda i: (i, z_block)),
            pl.BlockSpec((tm, D_MODEL), lambda i: (i, 0)),
            pl.BlockSpec((1, GDN_HEAD), lambda i: (0, 0)),
            pl.BlockSpec((GDN_VALUE_DIM, D_MODEL), lambda i: (0, 0)),
        ],
        out_specs=pl.BlockSpec((tm, D_MODEL), lambda i: (i, 0)),
        out_shape=jax.ShapeDtypeStruct((m, D_MODEL), F32),
        compiler_params=pltpu.CompilerParams(
            dimension_semantics=("parallel",), vmem_limit_bytes=VMEM_LIMIT_BYTES),
        name="gdn_out",
    )(o2d, qkvz2d, x2d, gw, w_out)


def _head_sum(x, ones_bd):
    parts = []
    for p in range(RWKV_PAIRS):
        hi, lo = _split2(x[:, p * LANES:(p + 1) * LANES])
        parts.append(jnp.dot(hi, ones_bd, preferred_element_type=F32) + jnp.dot(lo, ones_bd, preferred_element_type=F32))
    return jnp.concatenate(parts, axis=1)


def _rwkv_in_kernel(x_ref, xp_ref, sh_ref, nw_ref, mu_ref, wr_ref, wk_ref, wv_ref, wz_ref,
                    w0_ref, w1_ref, w2_ref, a0_ref, a1_ref, a2_ref, kk_ref, ka_ref, rk_ref, ones_ref,
                    r_out, k_out, v_out, ld_out, kn_out, b_out, z_out, bonus_out, last_out, *, tb, tt, n_t):
    t = pl.program_id(1)
    m = tb * tt
    nw = nw_ref[...]
    xn = _rms(x_ref[...].reshape(m, D_MODEL), nw)
    prev_last = _rms(xp_ref[:, SUBLANES - 1:SUBLANES, :], nw)
    first = jnp.where(t == 0, sh_ref[...], prev_last)
    first = jnp.broadcast_to(first, (tb, tt, D_MODEL)).reshape(m, D_MODEL)
    rolled = pltpu.roll(xn, 1, axis=0)
    rows = lax.broadcasted_iota(jnp.int32, (m, 1), 0)
    xprev = jnp.where((rows & (tt - 1)) == 0, first, rolled)
    xx = xprev - xn
    mu = mu_ref[...]

    def mix(i):
        return xn + xx * mu[i:i + 1, :]

    r = _bdot(mix(0), wr_ref[...])
    k = _bdot(mix(1), wk_ref[...])
    v = _bdot(mix(2), wv_ref[...])
    z = _bdot(mix(3), wz_ref[...])
    w = w0_ref[...] + _bdot(jnp.tanh(_bdot(mix(4), w1_ref[...])), w2_ref[...])
    w = -_softplus(-w) - 0.5
    a = jax.nn.sigmoid(a0_ref[...] + _bdot(_bdot(mix(5), a1_ref[...]), a2_ref[...]))
    ones_bd = ones_ref[...]
    kk = k * kk_ref[...]
    kn = kk * lax.rsqrt(_head_sum(kk * kk, ones_bd) + L2_EPS)
    k = k * (1.0 + (a - 1.0) * ka_ref[...])
    r_out[...] = r
    k_out[...] = k
    v_out[...] = v
    z_out[...] = z
    ld_out[...] = -jnp.exp(w)
    kn_out[...] = kn
    b_out[...] = kn * a
    bonus_out[...] = _head_sum(r * k * rk_ref[...], ones_bd) * v

    @pl.when(t == n_t - 1)
    def _():
        last_out[...] = _rms(x_ref[:, tt - 1:tt, :], nw)


def _rwkv_in(x3d, shift, nw, mu, wr, wk, wv, wz, w0, w1, w2, a0, a1, a2, k_k, k_a, r_k, ones_bd, tb, tt):
    b, t, _ = x3d.shape
    n_t = t // tt
    tpb = tt // SUBLANES
    kern = functools.partial(_rwkv_in_kernel, tb=tb, tt=tt, n_t=n_t)
    row = lambda n: pl.BlockSpec((n, D_MODEL), lambda i, j: (0, 0))
    big = pl.BlockSpec((D_MODEL, D_MODEL), lambda i, j: (0, 0))
    tok = pl.BlockSpec((tb * tt, D_MODEL), lambda i, j: (i * n_t + j, 0))
    m = b * t
    outs = pl.pallas_call(
        kern,
        grid=(b // tb, n_t),
        in_specs=[
            pl.BlockSpec((tb, tt, D_MODEL), lambda i, j: (i, j, 0)),
            pl.BlockSpec((tb, SUBLANES, D_MODEL), lambda i, j: (i, jnp.maximum(j * tpb - 1, 0), 0)),
            pl.BlockSpec((tb, 1, D_MODEL), lambda i, j: (i, 0, 0)),
            row(1), row(6), big, big, big, big,
            row(1), pl.BlockSpec((D_MODEL, LORA), lambda i, j: (0, 0)), pl.BlockSpec((LORA, D_MODEL), lambda i, j: (0, 0)),
            row(1), pl.BlockSpec((D_MODEL, LORA), lambda i, j: (0, 0)), pl.BlockSpec((LORA, D_MODEL), lambda i, j: (0, 0)),
            row(1), row(1), row(1),
            pl.BlockSpec((LANES, LANES), lambda i, j: (0, 0)),
        ],
        out_specs=[tok] * 8 + [pl.BlockSpec((tb, 1, D_MODEL), lambda i, j: (i, 0, 0))],
        out_shape=[jax.ShapeDtypeStruct((m, D_MODEL), F32)] * 8 + [jax.ShapeDtypeStruct((b, 1, D_MODEL), F32)],
        compiler_params=pltpu.CompilerParams(
            dimension_semantics=("parallel", "arbitrary"), vmem_limit_bytes=VMEM_LIMIT_BYTES),
        name="rwkv_in",
    )(x3d, x3d, shift, nw, mu, wr, wk, wv, wz, w0, w1, w2, a0, a1, a2, k_k, k_a, r_k, ones_bd)
    return outs


def _rwkv_rec_kernel(r_ref, k_ref, v_ref, ld_ref, kn_ref, b_ref, s0_ref, y_ref, sout_ref, s_scr, *, chunk, n_chunks):
    c = pl.program_id(1)
    C = chunk
    C2 = 2 * C

    @pl.when(c == 0)
    def _():
        s_scr[...] = s0_ref[0]

    r1, c1 = _iota2(C, C)
    tril = jnp.where(r1 >= c1, 1.0, 0.0).astype(BF16)
    lane = lax.broadcasted_iota(jnp.int32, (1, LANES), 1)
    m_lo = lane < RWKV_HEAD
    row, col = _iota2(C2, C2)
    strict2 = row > col
    rc, cc = _iota2(C, C2)
    tl = cc & (C - 1)
    strict_pair = rc > tl
    incl_pair = rc >= tl
    rb, cb = _iota2(LANES, LANES)
    bd_mask = (rb >= RWKV_HEAD) == (cb >= RWKV_HEAD)

    def stack(x):
        return jnp.concatenate([jnp.where(m_lo, x, 0.0), jnp.where(m_lo, 0.0, x)], axis=0)

    for p in range(RWKV_PAIRS):
        sl = slice(p * LANES, (p + 1) * LANES)
        r = r_ref[0, :, sl]
        k = k_ref[0, :, sl]
        v = v_ref[0, :, sl]
        ld = ld_ref[0, :, sl]
        a = -kn_ref[0, :, sl]
        b = b_ref[0, :, sl]
        gc = _dot_exact_lhs(tril, ld, "nn")
        glast = gc[C - 1:C, :]
        e_in = jnp.exp(gc)
        e_ex = jnp.exp(gc - ld)
        e_neg = jnp.exp(-gc)
        e_end = jnp.exp(glast - gc)
        rt = r * e_in
        at = a * e_ex
        kt_s = stack(k * e_neg)
        bt_s = stack(b * e_neg)
        v_s = stack(v)
        st = s_scr[p]
        ar = jnp.concatenate([at, rt], axis=0)
        g_k = _bdot_nt(ar, kt_s)
        g_rb = _bdot_nt(rt, bt_s)
        l_bd = jnp.where(strict2, -_bdot_nt(stack(at), bt_s), 0.0)
        ars = _bdot_nt(ar, st)
        a_ak = jnp.where(strict_pair, g_k[:C], 0.0)
        u0 = ars[:C] + _bdot(a_ak, v_s)
        tinv = _tri_inv(l_bd, C2, C, _dot3)
        u_s = _dot3(tinv, stack(u0))
        rk = jnp.where(incl_pair, g_k[C:], 0.0)
        rb_m = jnp.where(incl_pair, g_rb, 0.0)
        y = ars[C:] + (_bdot(rk, v_s) + _bdot(rb_m, u_s))
        y_ref[0, :, sl] = y
        u = u_s[:C] + u_s[C:]
        upd = _bdot_tn(jnp.concatenate([v, u], axis=0), jnp.concatenate([k * e_end, b * e_end], axis=0))
        s_scr[p] = st * jnp.exp(glast) + jnp.where(bd_mask, upd, 0.0)

    @pl.when(c == n_chunks - 1)
    def _():
        sout_ref[0] = s_scr[...]


def _rwkv_rec(r, k, v, ld, kn, bvec, s0_bd, chunk):
    b, t, _ = r.shape
    n_chunks = t // chunk
    kern = functools.partial(_rwkv_rec_kernel, chunk=chunk, n_chunks=n_chunks)
    tok = pl.BlockSpec((1, chunk, D_MODEL), lambda i, c: (i, c, 0))
    st = pl.BlockSpec((1, RWKV_PAIRS, LANES, LANES), lambda i, c: (i, 0, 0, 0))
    return pl.pallas_call(
        kern,
        grid=(b, n_chunks),
        in_specs=[tok] * 6 + [st],
        out_specs=[tok, st],
        out_shape=[
            jax.ShapeDtypeStruct((b, t, D_MODEL), F32),
            jax.ShapeDtypeStruct((b, RWKV_PAIRS, LANES, LANES), F32),
        ],
        scratch_shapes=[pltpu.VMEM((RWKV_PAIRS, LANES, LANES), F32)],
        compiler_params=pltpu.CompilerParams(
            dimension_semantics=("parallel", "arbitrary"), vmem_limit_bytes=VMEM_LIMIT_BYTES),
        name="rwkv_rec",
    )(r, k, v, ld, kn, bvec, s0_bd)


def _rwkv_out_kernel(y_ref, z_ref, bonus_ref, x_ref, lw_ref, lb_ref, avg_ref, w_ref, fw_ref, out_ref):
    y = y_ref[...]
    avg = avg_ref[...]
    mean = _head_sum(y, avg)
    yc = y - mean
    var = _head_sum(yc * yc, avg)
    gn = yc * lax.rsqrt(var + GN_EPS) * lw_ref[...] + lb_ref[...]
    o = (gn + bonus_ref[...]) * _silu(z_ref[...])
    x2 = x_ref[...] + _bdot(o, w_ref[...])
    out_ref[...] = _rms(x2, fw_ref[...])


def _rwkv_out(y2d, z2d, bonus2d, x2d, lnx_w, lnx_b, avg_bd, w_o, fw, tm):
    m = x2d.shape[0]
    tok = pl.BlockSpec((tm, D_MODEL), lambda i: (i, 0))
    row = pl.BlockSpec((1, D_MODEL), lambda i: (0, 0))
    return pl.pallas_call(
        _rwkv_out_kernel,
        grid=(m // tm,),
        in_specs=[tok, tok, tok, tok, row, row,
                  pl.BlockSpec((LANES, LANES), lambda i: (0, 0)),
                  pl.BlockSpec((D_MODEL, D_MODEL), lambda i: (0, 0)), row],
        out_specs=tok,
        out_shape=jax.ShapeDtypeStruct((m, D_MODEL), F32),
        compiler_params=pltpu.CompilerParams(
            dimension_semantics=("parallel",), vmem_limit_bytes=VMEM_LIMIT_BYTES),
        name="rwkv_out",
    )(y2d, z2d, bonus2d, x2d, lnx_w, lnx_b, avg_bd, w_o, fw)


def _pick_tile(m, cap):
    t = min(m, cap)
    while m % t:
        t //= 2
    return t


def _rwkv_state_to_pairs(s):
    b = s.shape[0]
    s = s.reshape(b, RWKV_PAIRS, 2, RWKV_HEAD, RWKV_HEAD)
    z = jnp.zeros((b, RWKV_PAIRS, RWKV_HEAD, RWKV_HEAD), s.dtype)
    top = jnp.concatenate([s[:, :, 0], z], axis=-1)
    bot = jnp.concatenate([z, s[:, :, 1]], axis=-1)
    return jnp.concatenate([top, bot], axis=-2)


def _rwkv_state_from_pairs(s):
    b = s.shape[0]
    s0 = s[:, :, :RWKV_HEAD, :RWKV_HEAD]
    s1 = s[:, :, RWKV_HEAD:, RWKV_HEAD:]
    return jnp.stack([s0, s1], axis=2).reshape(b, RWKV_HEADS, RWKV_HEAD, RWKV_HEAD)


def _trunk(x, gdn_s, gdn_conv, rwkv_s, rwkv_shift, p, gdn_chunk, rwkv_chunk):
    b, t, _ = x.shape
    m = b * t
    x2d = x.reshape(m, D_MODEL)
    tm = _pick_tile(m, 512)
    qkvz, ba = _gdn_in(x2d, p["nw0"], p["w_qkvz"], p["w_ba"], tm, 1024)
    o, s_gdn, conv_new = _gdn_chunk(qkvz.reshape(b, t, GDN_QKVZ_DIM), ba.reshape(b, t, LANES), gdn_conv, gdn_s,
                                    p["conv_wt"], p["alog_row"], p["dtb_row"], gdn_chunk)
    tm2 = _pick_tile(m, 256)
    x1 = _gdn_out(o.reshape(m, GDN_VALUE_DIM), qkvz, x2d, p["gn_w"], p["w_out"], tm2)
    if t >= 256:
        tb, tt = 1, 256
    elif t * b <= 256:
        tb, tt = b, t
    else:
        tb, tt = 256 // t, t
    r, k, v, ld, kn, bvec, z, bonus, last = _rwkv_in(
        x1.reshape(b, t, D_MODEL), rwkv_shift.reshape(b, 1, D_MODEL), p["nw1"], p["mu"],
        p["wr"], p["wk"], p["wv"], p["wz"], p["w0"], p["w1"], p["w2"], p["a0"], p["a1"], p["a2"],
        p["k_k"], p["k_a"], p["r_k"], p["ones_bd"], tb, tt)
    sh3 = lambda a: a.reshape(b, t, D_MODEL)
    y, s_rwkv_bd = _rwkv_rec(sh3(r), sh3(k), sh3(v), sh3(ld), sh3(kn), sh3(bvec), _rwkv_state_to_pairs(rwkv_s), rwkv_chunk)
    out = _rwkv_out(y.reshape(m, D_MODEL), z, bonus, x1, p["lnx_w"], p["lnx_b"], p["avg_bd"], p["w_o"], p["fw"], tm2)
    return (out.reshape(b, t, D_MODEL), s_gdn, conv_new, _rwkv_state_from_pairs(s_rwkv_bd), last.reshape(b, D_MODEL))


def kernel(x_prompt, x_sample, state_gdn, state_gdn_conv, state_rwkv, state_rwkv_shift, meta_tokens, norm_w, final_norm_w, gdn_w_in, gdn_conv_w, gdn_a_log, gdn_dt_bias, gdn_norm_w, gdn_w_out, rwkv_mu, rwkv_w_rkvz, rwkv_w0, rwkv_w1, rwkv_w2, rwkv_a0, rwkv_a1, rwkv_a2, rwkv_k_k, rwkv_k_a, rwkv_r_k, rwkv_lnx_w, rwkv_lnx_b, rwkv_w_o):
    assert norm_w.shape[0] == 2 and gdn_w_in.shape[0] == 1 and rwkv_mu.shape[0] == 1
    row = lambda a: a.reshape(1, -1).astype(F32)
    w_in = gdn_w_in[0]
    w_ba = jnp.pad(w_in[:, GDN_QKVZ_DIM:], ((0, 0), (0, LANES - 2 * GDN_V_HEADS)))
    gate_row = lambda a: jnp.pad(a.astype(F32), (GDN_V_HEADS, LANES - 2 * GDN_V_HEADS)).reshape(1, LANES)
    rb, cb = jnp.arange(LANES)[:, None] // RWKV_HEAD, jnp.arange(LANES)[None, :] // RWKV_HEAD
    bd = (rb == cb)
    p = {
        "nw0": row(norm_w[0]), "nw1": row(norm_w[1]), "fw": row(final_norm_w),
        "w_qkvz": w_in[:, :GDN_QKVZ_DIM].astype(BF16), "w_ba": w_ba.astype(BF16),
        "conv_wt": gdn_conv_w[0].T.astype(F32),
        "alog_row": gate_row(gdn_a_log[0]), "dtb_row": gate_row(gdn_dt_bias[0]),
        "gn_w": row(gdn_norm_w[0]), "w_out": gdn_w_out[0].astype(BF16),
        "mu": rwkv_mu[0].astype(F32),
        "wr": rwkv_w_rkvz[0, 0].astype(BF16), "wk": rwkv_w_rkvz[0, 1].astype(BF16),
        "wv": rwkv_w_rkvz[0, 2].astype(BF16), "wz": rwkv_w_rkvz[0, 3].astype(BF16),
        "w0": row(rwkv_w0[0]), "w1": rwkv_w1[0].astype(BF16), "w2": rwkv_w2[0].astype(BF16),
        "a0": row(rwkv_a0[0]), "a1": rwkv_a1[0].astype(BF16), "a2": rwkv_a2[0].astype(BF16),
        "k_k": row(rwkv_k_k[0]), "k_a": row(rwkv_k_a[0]), "r_k": row(rwkv_r_k[0]),
        "lnx_w": row(rwkv_lnx_w[0]), "lnx_b": row(rwkv_lnx_b[0]), "w_o": rwkv_w_o[0].astype(BF16),
        "ones_bd": bd.astype(BF16), "avg_bd": (bd.astype(F32) / RWKV_HEAD).astype(BF16),
    }
    bp = x_prompt.shape[0]
    zeros = lambda *s: jnp.zeros(s, F32)
    _, m_gdn, m_conv, m_rwkv, m_shift = _trunk(
        meta_tokens.astype(F32)[None], zeros(1, GDN_V_HEADS, GDN_HEAD, GDN_HEAD), zeros(1, CONV_W - 1, GDN_CONV_DIM),
        zeros(1, RWKV_HEADS, RWKV_HEAD, RWKV_HEAD), zeros(1, D_MODEL), p, N_META, N_META)
    rep = lambda a: jnp.broadcast_to(a, (bp,) + a.shape[1:])
    y_p, p_gdn, p_conv, p_rwkv, p_shift = _trunk(
        x_prompt, rep(m_gdn), rep(m_conv), rep(m_rwkv), rep(m_shift), p, GDN_CHUNK, RWKV_CHUNK)
    ts = x_sample.shape[1]
    y_s, s_gdn, s_conv, s_rwkv, s_shift = _trunk(
        x_sample, state_gdn[0], state_gdn_conv[0], state_rwkv[0], state_rwkv_shift[0], p, ts, ts)
    return (y_p, y_s, p_gdn[None], p_conv[None], p_rwkv[None], p_shift[None],
            s_gdn[None], s_conv[None], s_rwkv[None], s_shift[None])
```

```python
import functools

import jax
import jax.numpy as jnp
from jax import lax
from jax.experimental import pallas as pl
from jax.experimental.pallas import tpu as pltpu

F32 = jnp.float32
BF16 = jnp.bfloat16

D_MODEL = 1024
N_META = 16
GDN_QK_HEADS = 8
GDN_V_HEADS = 16
GDN_HEAD = 128
GDN_KEY_DIM = GDN_QK_HEADS * GDN_HEAD
GDN_VALUE_DIM = GDN_V_HEADS * GDN_HEAD
GDN_CONV_DIM = 2 * GDN_KEY_DIM + GDN_VALUE_DIM
GDN_QKVZ_DIM = GDN_CONV_DIM + GDN_VALUE_DIM
CONV_W = 4
GDN_CHUNK = 64
RWKV_HEAD = 64
RWKV_HEADS = D_MODEL // RWKV_HEAD
RWKV_PAIRS = RWKV_HEADS // 2
RWKV_CHUNK = 64
LORA = 64
RMS_EPS = 1e-6
L2_EPS = 1e-6
GN_EPS = 64e-5
LANES = 128
SUBLANES = 8
VMEM_LIMIT_BYTES = 56 * 1024 * 1024


def _bdot(a, b):
    return jnp.dot(a.astype(BF16), b.astype(BF16), preferred_element_type=F32)


def _bdot_nt(a, b):
    return lax.dot_general(a.astype(BF16), b.astype(BF16), (((1,), (1,)), ((), ())), preferred_element_type=F32)


def _bdot_tn(a, b):
    return lax.dot_general(a.astype(BF16), b.astype(BF16), (((0,), (0,)), ((), ())), preferred_element_type=F32)


def _split2(a):
    hi = a.astype(BF16)
    lo = (a - hi.astype(F32)).astype(BF16)
    return hi, lo


def _split3(a):
    hi = a.astype(BF16)
    r1 = a - hi.astype(F32)
    mid = r1.astype(BF16)
    lo = (r1 - mid.astype(F32)).astype(BF16)
    return hi, mid, lo


def _dot3(a, b):
    ah, al = _split2(a)
    bh, bl = _split2(b)
    d = functools.partial(jnp.dot, preferred_element_type=F32)
    return d(ah, bh) + (d(ah, bl) + d(al, bh))


def _dot_exact_lhs(m_bf16, x, dims):
    xs = _split3(x)
    out = None
    for part in xs:
        if dims == "nn":
            t = jnp.dot(m_bf16, part, preferred_element_type=F32)
        else:
            t = lax.dot_general(part, m_bf16, (((0,), (1,)), ((), ())), preferred_element_type=F32)
        out = t if out is None else out + t
    return out


_INV_DOT = _bdot


def _rms(x, w):
    return x * lax.rsqrt(jnp.mean(x * x, axis=-1, keepdims=True) + RMS_EPS) * w


def _silu(x):
    return x * jax.nn.sigmoid(x)


def _softplus(x):
    return jnp.maximum(x, 0.0) + jnp.log1p(jnp.exp(-jnp.abs(x)))


def _iota2(n, m):
    return lax.broadcasted_iota(jnp.int32, (n, m), 0), lax.broadcasted_iota(jnp.int32, (n, m), 1)


def _log2(n):
    l = n.bit_length() - 1
    assert (1 << l) == n, n
    return l


def _tri_inv_many(lows, n, blk, dot):
    row, col = _iota2(n, n)
    eye = jnp.where(row == col, 1.0, 0.0).astype(F32)
    s = 1
    ts = None
    while s < blk:
        sh = _log2(s)
        sub_r = row >> sh
        sub_c = col >> sh
        m = ((sub_r >> 1) == (sub_c >> 1)) & ((sub_r & 1) == 1) & ((sub_c & 1) == 0)
        offs = [jnp.where(m, low, 0.0) for low in lows]
        if ts is None:
            ts = [eye - off for off in offs]
        else:
            tmp = [dot(t, off) for t, off in zip(ts, offs)]
            ts = [t - dot(x, t) for t, x in zip(ts, tmp)]
        s *= 2
    return [eye for _ in lows] if ts is None else ts


def _gdn_in_kernel(x_ref, nw_ref, w_ref, wba_ref, qkvz_ref, ba_ref, xn_scr):
    n = pl.program_id(1)

    @pl.when(n == 0)
    def _():
        xn = _rms(x_ref[...], nw_ref[...])
        xb = xn.astype(BF16)
        xn_scr[...] = xb
        ba_ref[...] = jnp.dot(xb, wba_ref[...], preferred_element_type=F32)

    qkvz_ref[...] = jnp.dot(xn_scr[...], w_ref[...], preferred_element_type=F32)


def _gdn_in(x2d, nw, w_qkvz, w_ba, tm, tn):
    m = x2d.shape[0]
    return pl.pallas_call(
        _gdn_in_kernel,
        grid=(m // tm, GDN_QKVZ_DIM // tn),
        in_specs=[
            pl.BlockSpec((tm, D_MODEL), lambda i, j: (i, 0)),
            pl.BlockSpec((1, D_MODEL), lambda i, j: (0, 0)),
            pl.BlockSpec((D_MODEL, tn), lambda i, j: (0, j)),
            pl.BlockSpec((D_MODEL, LANES), lambda i, j: (0, 0)),
        ],
        out_specs=[
            pl.BlockSpec((tm, tn), lambda i, j: (i, j)),
            pl.BlockSpec((tm, LANES), lambda i, j: (i, 0)),
        ],
        out_shape=[
            jax.ShapeDtypeStruct((m, GDN_QKVZ_DIM), F32),
            jax.ShapeDtypeStruct((m, LANES), F32),
        ],
        scratch_shapes=[pltpu.VMEM((tm, D_MODEL), BF16)],
        compiler_params=pltpu.CompilerParams(
            dimension_semantics=("parallel", "arbitrary"), vmem_limit_bytes=VMEM_LIMIT_BYTES),
        name="gdn_in",
    )(x2d, nw, w_qkvz, w_ba)


def _gdn_chunk_kernel(qkv_ref, ba_ref, buf_ref, s0_ref, cw_ref, alog_ref, dtb_ref,
                      o_ref, sout_ref, nbuf_ref, s_scr, up_scr, *, chunk, n_chunks):
    c = pl.program_id(1)
    C = chunk
    C2 = 2 * C
    pad = SUBLANES - (CONV_W - 1)

    @pl.when(c == 0)
    def _():
        s_scr[...] = s0_ref[0]
        up_scr[0:SUBLANES, :] = jnp.zeros((SUBLANES, GDN_CONV_DIM), F32)
        up_scr[pad:SUBLANES, :] = buf_ref[0]

    up_scr[SUBLANES:SUBLANES + C, :] = qkv_ref[0]
    y = up_scr[pad:pad + C, :] * cw_ref[0:1, :]
    for j in range(1, CONV_W):
        y = y + up_scr[pad + j:pad + j + C, :] * cw_ref[j:j + 1, :]
    act = _silu(y)
    tail = up_scr[C:C + SUBLANES, :]
    up_scr[0:SUBLANES, :] = tail

    @pl.when(c == n_chunks - 1)
    def _():
        nbuf_ref[0] = tail[pad:SUBLANES, :]

    ba = ba_ref[0]
    beta_all = jax.nn.sigmoid(ba)
    g_all = -jnp.exp(alog_ref[...]) * _softplus(ba + dtb_ref[...])
    r1, c1 = _iota2(C, C)
    tril = jnp.where(r1 >= c1, 1.0, 0.0).astype(BF16)
    gc_all = _dot_exact_lhs(tril, g_all, "nn")
    r2, c2 = _iota2(C2, C)
    tril2 = jnp.where((r2 & (C - 1)) >= c2, 1.0, 0.0).astype(BF16)
    gct_all = _dot_exact_lhs(tril2, g_all, "tn")

    row, col = _iota2(C2, C2)
    same = (row >= C) == (col >= C)
    causal = same & (row >= col)
    strict = same & (row > col)
    top = lax.broadcasted_iota(jnp.int32, (C2, 1), 0) < C
    left = lax.broadcasted_iota(jnp.int32, (1, C2), 1) < C

    pairs = []
    for qh in range(GDN_QK_HEADS):
        q = act[:, qh * GDN_HEAD:(qh + 1) * GDN_HEAD]
        k = act[:, GDN_KEY_DIM + qh * GDN_HEAD:GDN_KEY_DIM + (qh + 1) * GDN_HEAD]
        q = q * lax.rsqrt(jnp.sum(q * q, axis=-1, keepdims=True) + L2_EPS) * (GDN_HEAD ** -0.5)
        k = k * lax.rsqrt(jnp.sum(k * k, axis=-1, keepdims=True) + L2_EPS)
        h0 = 2 * qh
        h1 = h0 + 1
        a0, a1 = GDN_V_HEADS + h0, GDN_V_HEADS + h1
        beta_s = jnp.concatenate([beta_all[:, h0:h0 + 1], beta_all[:, h1:h1 + 1]], axis=0)
        gc_s = jnp.concatenate([gc_all[:, a0:a0 + 1], gc_all[:, a1:a1 + 1]], axis=0)
        gc_row = jnp.where(left, gct_all[a0:a0 + 1, :], gct_all[a1:a1 + 1, :])
        glast0 = gc_all[C - 1:C, a0:a0 + 1]
        glast1 = gc_all[C - 1:C, a1:a1 + 1]
        glast_s = jnp.where(top, glast0, glast1)
        decay = jnp.where(causal, jnp.exp(jnp.minimum(gc_s - gc_row, 0.0)), 0.0)
        k_s = jnp.concatenate([k, k], axis=0)
        q_s = jnp.concatenate([q, q], axis=0)
        v_s = jnp.concatenate([act[:, 2 * GDN_KEY_DIM + h0 * GDN_HEAD:2 * GDN_KEY_DIM + (h0 + 1) * GDN_HEAD],
                               act[:, 2 * GDN_KEY_DIM + h1 * GDN_HEAD:2 * GDN_KEY_DIM + (h1 + 1) * GDN_HEAD]], axis=0)
        egc = jnp.exp(gc_s)
        pairs.append(dict(
            h0=h0, h1=h1,
            low=jnp.where(strict, _bdot_nt(k_s, k_s) * beta_s * decay, 0.0),
            rhs=jnp.concatenate([v_s * beta_s, k_s * (beta_s * egc)], axis=1),
            attn=_bdot_nt(q_s, k_s) * decay, qd_s=q_s * egc, kd_s=k_s * jnp.exp(glast_s - gc_s),
            eg0=jnp.exp(glast0), eg1=jnp.exp(glast1)))

    tinvs = _tri_inv_many([pr["low"] for pr in pairs], C2, C, _INV_DOT)
    sols = [_INV_DOT(t, pr["rhs"]) for t, pr in zip(tinvs, pairs)]
    for pr, sol in zip(pairs, sols):
        h0, h1 = pr["h0"], pr["h1"]
        u_s = sol[:, :GDN_HEAD]
        w_s = sol[:, GDN_HEAD:]
        st0 = s_scr[h0]
        st1 = s_scr[h1]
        vn0 = u_s[:C] - _bdot(w_s[:C], st0)
        vn1 = u_s[C:] - _bdot(w_s[C:], st1)
        vn_s = jnp.concatenate([vn0, vn1], axis=0)
        qd_s, kd_s = pr["qd_s"], pr["kd_s"]
        o_s = _bdot(pr["attn"], vn_s) + jnp.concatenate([_bdot(qd_s[:C], st0), _bdot(qd_s[C:], st1)], axis=0)
        o_ref[0, :, h0 * GDN_HEAD:(h0 + 1) * GDN_HEAD] = o_s[:C]
        o_ref[0, :, h1 * GDN_HEAD:(h1 + 1) * GDN_HEAD] = o_s[C:]
        s_scr[h0] = st0 * pr["eg0"] + _bdot_tn(kd_s[:C], vn0)
        s_scr[h1] = st1 * pr["eg1"] + _bdot_tn(kd_s[C:], vn1)

    @pl.when(c == n_chunks - 1)
    def _():
        sout_ref[0] = s_scr[...]


def _gdn_chunk(qkvz, ba, conv_buf, s0, conv_wt, alog_row, dtb_row, chunk):
    b, t, _ = qkvz.shape
    n_chunks = t // chunk
    kern = functools.partial(_gdn_chunk_kernel, chunk=chunk, n_chunks=n_chunks)
    return pl.pallas_call(
        kern,
        grid=(b, n_chunks),
        in_specs=[
            pl.BlockSpec((1, chunk, GDN_CONV_DIM), lambda i, c: (i, c, 0)),
            pl.BlockSpec((1, chunk, LANES), lambda i, c: (i, c, 0)),
            pl.BlockSpec((1, CONV_W - 1, GDN_CONV_DIM), lambda i, c: (i, 0, 0)),
            pl.BlockSpec((1, GDN_V_HEADS, GDN_HEAD, GDN_HEAD), lambda i, c: (i, 0, 0, 0)),
            pl.BlockSpec((CONV_W, GDN_CONV_DIM), lambda i, c: (0, 0)),
            pl.BlockSpec((1, LANES), lambda i, c: (0, 0)),
            pl.BlockSpec((1, LANES), lambda i, c: (0, 0)),
        ],
        out_specs=[
            pl.BlockSpec((1, chunk, GDN_VALUE_DIM), lambda i, c: (i, c, 0)),
            pl.BlockSpec((1, GDN_V_HEADS, GDN_HEAD, GDN_HEAD), lambda i, c: (i, 0, 0, 0)),
            pl.BlockSpec((1, CONV_W - 1, GDN_CONV_DIM), lambda i, c: (i, 0, 0)),
        ],
        out_shape=[
            jax.ShapeDtypeStruct((b, t, GDN_VALUE_DIM), F32),
            jax.ShapeDtypeStruct((b, GDN_V_HEADS, GDN_HEAD, GDN_HEAD), F32),
            jax.ShapeDtypeStruct((b, CONV_W - 1, GDN_CONV_DIM), F32),
        ],
        scratch_shapes=[
            pltpu.VMEM((GDN_V_HEADS, GDN_HEAD, GDN_HEAD), F32),
            pltpu.VMEM((chunk + SUBLANES, GDN_CONV_DIM), F32),
        ],
        compiler_params=pltpu.CompilerParams(
            dimension_semantics=("parallel", "arbitrary"), vmem_limit_bytes=VMEM_LIMIT_BYTES),
        name="gdn_chunk",
    )(qkvz, ba, conv_buf, s0, conv_wt, alog_row, dtb_row)


def _gdn_out_kernel(o_ref, z_ref, x_ref, gw_ref, w_ref, y_ref):
    gw = gw_ref[...]
    parts = []
    for h in range(GDN_V_HEADS):
        sl = slice(h * GDN_HEAD, (h + 1) * GDN_HEAD)
        parts.append((_rms(o_ref[:, sl], gw) * _silu(z_ref[:, sl])).astype(BF16))
    gated = jnp.concatenate(parts, axis=1)
    y_ref[...] = x_ref[...] + jnp.dot(gated, w_ref[...], preferred_element_type=F32)


def _gdn_out(o2d, qkvz2d, x2d, gw, w_out, tm):
    m = x2d.shape[0]
    z_block = GDN_CONV_DIM // GDN_VALUE_DIM
    return pl.pallas_call(
        _gdn_out_kernel,
        grid=(m // tm,),
        in_specs=[
            pl.BlockSpec((tm, GDN_VALUE_DIM), lambda i: (i, 0)),
            pl.BlockSpec((tm, GDN_VALUE_DIM), lambda i: (i, z_block)),
            pl.BlockSpec((tm, D_MODEL), lambda i: (i, 0)),
            pl.BlockSpec((1, GDN_HEAD), lambda i: (0, 0)),
            pl.BlockSpec((GDN_VALUE_DIM, D_MODEL), lambda i: (0, 0)),
        ],
        out_specs=pl.BlockSpec((tm, D_MODEL), lambda i: (i, 0)),
        out_shape=jax.ShapeDtypeStruct((m, D_MODEL), F32),
        compiler_params=pltpu.CompilerParams(
            dimension_semantics=("parallel",), vmem_limit_bytes=VMEM_LIMIT_BYTES),
        name="gdn_out",
    )(o2d, qkvz2d, x2d, gw, w_out)


def _head_sum(x, ones_bd):
    parts = []
    for p in range(RWKV_PAIRS):
        hi, lo = _split2(x[:, p * LANES:(p + 1) * LANES])
        parts.append(jnp.dot(hi, ones_bd, preferred_element_type=F32) + jnp.dot(lo, ones_bd, preferred_element_type=F32))
    return jnp.concatenate(parts, axis=1)


def _rwkv_in_kernel(x_ref, xp_ref, sh_ref, nw_ref, mu_ref, wr_ref, wk_ref, wv_ref, wz_ref,
                    w0_ref, w1_ref, w2_ref, a0_ref, a1_ref, a2_ref, kk_ref, ka_ref, rk_ref, ones_ref,
                    r_out, k_out, v_out, ld_out, kn_out, b_out, z_out, bonus_out, last_out, *, tb, tt, n_t):
    t = pl.program_id(1)
    m = tb * tt
    nw = nw_ref[...]
    xn = _rms(x_ref[...].reshape(m, D_MODEL), nw)
    prev_last = _rms(xp_ref[:, SUBLANES - 1:SUBLANES, :], nw)
    first = jnp.where(t == 0, sh_ref[...], prev_last)
    first = jnp.broadcast_to(first, (tb, tt, D_MODEL)).reshape(m, D_MODEL)
    rolled = pltpu.roll(xn, 1, axis=0)
    rows = lax.broadcasted_iota(jnp.int32, (m, 1), 0)
    xprev = jnp.where((rows & (tt - 1)) == 0, first, rolled)
    xx = xprev - xn
    mu = mu_ref[...]

    def mix(i):
        return xn + xx * mu[i:i + 1, :]

    r = _bdot(mix(0), wr_ref[...])
    k = _bdot(mix(1), wk_ref[...])
    v = _bdot(mix(2), wv_ref[...])
    z = _bdot(mix(3), wz_ref[...])
    w = w0_ref[...] + _bdot(jnp.tanh(_bdot(mix(4), w1_ref[...])), w2_ref[...])
    w = -_softplus(-w) - 0.5
    a = jax.nn.sigmoid(a0_ref[...] + _bdot(_bdot(mix(5), a1_ref[...]), a2_ref[...]))
    ones_bd = ones_ref[...]
    kk = k * kk_ref[...]
    kn = kk * lax.rsqrt(_head_sum(kk * kk, ones_bd) + L2_EPS)
    k = k * (1.0 + (a - 1.0) * ka_ref[...])
    r_out[...] = r
    k_out[...] = k
    v_out[...] = v
    z_out[...] = z
    ld_out[...] = -jnp.exp(w)
    kn_out[...] = kn
    b_out[...] = kn * a
    bonus_out[...] = _head_sum(r * k * rk_ref[...], ones_bd) * v

    @pl.when(t == n_t - 1)
    def _():
        last_out[...] = _rms(x_ref[:, tt - 1:tt, :], nw)


def _rwkv_in(x3d, shift, nw, mu, wr, wk, wv, wz, w0, w1, w2, a0, a1, a2, k_k, k_a, r_k, ones_bd, tb, tt):
    b, t, _ = x3d.shape
    n_t = t // tt
    tpb = tt // SUBLANES
    kern = functools.partial(_rwkv_in_kernel, tb=tb, tt=tt, n_t=n_t)
    row = lambda n: pl.BlockSpec((n, D_MODEL), lambda i, j: (0, 0))
    big = pl.BlockSpec((D_MODEL, D_MODEL), lambda i, j: (0, 0))
    tok = pl.BlockSpec((tb * tt, D_MODEL), lambda i, j: (i * n_t + j, 0))
    m = b * t
    outs = pl.pallas_call(
        kern,
        grid=(b // tb, n_t),
        in_specs=[
            pl.BlockSpec((tb, tt, D_MODEL), lambda i, j: (i, j, 0)),
            pl.BlockSpec((tb, SUBLANES, D_MODEL), lambda i, j: (i, jnp.maximum(j * tpb - 1, 0), 0)),
            pl.BlockSpec((tb, 1, D_MODEL), lambda i, j: (i, 0, 0)),
            row(1), row(6), big, big, big, big,
            row(1), pl.BlockSpec((D_MODEL, LORA), lambda i, j: (0, 0)), pl.BlockSpec((LORA, D_MODEL), lambda i, j: (0, 0)),
            row(1), pl.BlockSpec((D_MODEL, LORA), lambda i, j: (0, 0)), pl.BlockSpec((LORA, D_MODEL), lambda i, j: (0, 0)),
            row(1), row(1), row(1),
            pl.BlockSpec((LANES, LANES), lambda i, j: (0, 0)),
        ],
        out_specs=[tok] * 8 + [pl.BlockSpec((tb, 1, D_MODEL), lambda i, j: (i, 0, 0))],
        out_shape=[jax.ShapeDtypeStruct((m, D_MODEL), F32)] * 8 + [jax.ShapeDtypeStruct((b, 1, D_MODEL), F32)],
        compiler_params=pltpu.CompilerParams(
            dimension_semantics=("parallel", "arbitrary"), vmem_limit_bytes=VMEM_LIMIT_BYTES),
        name="rwkv_in",
    )(x3d, x3d, shift, nw, mu, wr, wk, wv, wz, w0, w1, w2, a0, a1, a2, k_k, k_a, r_k, ones_bd)
    return outs


def _rwkv_rec_kernel(r_ref, k_ref, v_ref, ld_ref, kn_ref, b_ref, s0_ref, y_ref, sout_ref, s_scr, *, chunk, n_chunks):
    c = pl.program_id(1)
    C = chunk
    C2 = 2 * C

    @pl.when(c == 0)
    def _():
        s_scr[...] = s0_ref[0]

    r1, c1 = _iota2(C, C)
    tril = jnp.where(r1 >= c1, 1.0, 0.0).astype(BF16)
    lane = lax.broadcasted_iota(jnp.int32, (1, LANES), 1)
    m_lo = lane < RWKV_HEAD
    row, col = _iota2(C2, C2)
    strict2 = row > col
    rc, cc = _iota2(C, C2)
    tl = cc & (C - 1)
    strict_pair = rc > tl
    incl_pair = rc >= tl
    rb, cb = _iota2(LANES, LANES)
    bd_mask = (rb >= RWKV_HEAD) == (cb >= RWKV_HEAD)

    def stack(x):
        return jnp.concatenate([jnp.where(m_lo, x, 0.0), jnp.where(m_lo, 0.0, x)], axis=0)

    pairs = []
    for p in range(RWKV_PAIRS):
        sl = slice(p * LANES, (p + 1) * LANES)
        r = r_ref[0, :, sl]
        k = k_ref[0, :, sl]
        v = v_ref[0, :, sl]
        ld = ld_ref[0, :, sl]
        a = -kn_ref[0, :, sl]
        b = b_ref[0, :, sl]
        gc = _dot_exact_lhs(tril, ld, "nn")
        glast = gc[C - 1:C, :]
        e_in = jnp.exp(gc)
        e_ex = jnp.exp(gc - ld)
        e_neg = jnp.exp(-gc)
        e_end = jnp.exp(glast - gc)
        rt = r * e_in
        at = a * e_ex
        kt_s = stack(k * e_neg)
        bt_s = stack(b * e_neg)
        v_s = stack(v)
        st = s_scr[p]
        ar = jnp.concatenate([at, rt], axis=0)
        g_k = _bdot_nt(ar, kt_s)
        g_rb = _bdot_nt(rt, bt_s)
        ars = _bdot_nt(ar, st)
        u0 = ars[:C] + _bdot(jnp.where(strict_pair, g_k[:C], 0.0), v_s)
        pairs.append(dict(
            p=p, sl=sl, st=st, v=v, v_s=v_s, u0_s=stack(u0), y0=ars[C:],
            low=jnp.where(strict2, -_bdot_nt(stack(at), bt_s), 0.0),
            rk=jnp.where(incl_pair, g_k[C:], 0.0), rb=jnp.where(incl_pair, g_rb, 0.0),
            kb_end=jnp.concatenate([k * e_end, b * e_end], axis=0), eg=jnp.exp(glast)))

    tinvs = _tri_inv_many([pr["low"] for pr in pairs], C2, C, _INV_DOT)
    us = [_INV_DOT(t, pr["u0_s"]) for t, pr in zip(tinvs, pairs)]
    for pr, u_s in zip(pairs, us):
        y_ref[0, :, pr["sl"]] = pr["y0"] + (_bdot(pr["rk"], pr["v_s"]) + _bdot(pr["rb"], u_s))
        u = u_s[:C] + u_s[C:]
        upd = _bdot_tn(jnp.concatenate([pr["v"], u], axis=0), pr["kb_end"])
        s_scr[pr["p"]] = pr["st"] * pr["eg"] + jnp.where(bd_mask, upd, 0.0)

    @pl.when(c == n_chunks - 1)
    def _():
        sout_ref[0] = s_scr[...]


def _rwkv_rec(r, k, v, ld, kn, bvec, s0_bd, chunk):
    b, t, _ = r.shape
    n_chunks = t // chunk
    kern = functools.partial(_rwkv_rec_kernel, chunk=chunk, n_chunks=n_chunks)
    tok = pl.BlockSpec((1, chunk, D_MODEL), lambda i, c: (i, c, 0))
    st = pl.BlockSpec((1, RWKV_PAIRS, LANES, LANES), lambda i, c: (i, 0, 0, 0))
    return pl.pallas_call(
        kern,
        grid=(b, n_chunks),
        in_specs=[tok] * 6 + [st],
        out_specs=[tok, st],
        out_shape=[
            jax.ShapeDtypeStruct((b, t, D_MODEL), F32),
            jax.ShapeDtypeStruct((b, RWKV_PAIRS, LANES, LANES), F32),
        ],
        scratch_shapes=[pltpu.VMEM((RWKV_PAIRS, LANES, LANES), F32)],
        compiler_params=pltpu.CompilerParams(
            dimension_semantics=("parallel", "arbitrary"), vmem_limit_bytes=VMEM_LIMIT_BYTES),
        name="rwkv_rec",
    )(r, k, v, ld, kn, bvec, s0_bd)


def _rwkv_out_kernel(y_ref, z_ref, bonus_ref, x_ref, lw_ref, lb_ref, avg_ref, w_ref, fw_ref, out_ref):
    y = y_ref[...]
    avg = avg_ref[...]
    mean = _head_sum(y, avg)
    yc = y - mean
    var = _head_sum(yc * yc, avg)
    gn = yc * lax.rsqrt(var + GN_EPS) * lw_ref[...] + lb_ref[...]
    o = (gn + bonus_ref[...]) * _silu(z_ref[...])
    x2 = x_ref[...] + _bdot(o, w_ref[...])
    out_ref[...] = _rms(x2, fw_ref[...])


def _rwkv_out(y2d, z2d, bonus2d, x2d, lnx_w, lnx_b, avg_bd, w_o, fw, tm):
    m = x2d.shape[0]
    tok = pl.BlockSpec((tm, D_MODEL), lambda i: (i, 0))
    row = pl.BlockSpec((1, D_MODEL), lambda i: (0, 0))
    return pl.pallas_call(
        _rwkv_out_kernel,
        grid=(m // tm,),
        in_specs=[tok, tok, tok, tok, row, row,
                  pl.BlockSpec((LANES, LANES), lambda i: (0, 0)),
                  pl.BlockSpec((D_MODEL, D_MODEL), lambda i: (0, 0)), row],
        out_specs=tok,
        out_shape=jax.ShapeDtypeStruct((m, D_MODEL), F32),
        compiler_params=pltpu.CompilerParams(
            dimension_semantics=("parallel",), vmem_limit_bytes=VMEM_LIMIT_BYTES),
        name="rwkv_out",
    )(y2d, z2d, bonus2d, x2d, lnx_w, lnx_b, avg_bd, w_o, fw)


def _pick_tile(m, cap):
    t = min(m, cap)
    while m % t:
        t //= 2
    return t


def _rwkv_state_to_pairs(s):
    b = s.shape[0]
    s = s.reshape(b, RWKV_PAIRS, 2, RWKV_HEAD, RWKV_HEAD)
    z = jnp.zeros((b, RWKV_PAIRS, RWKV_HEAD, RWKV_HEAD), s.dtype)
    top = jnp.concatenate([s[:, :, 0], z], axis=-1)
    bot = jnp.concatenate([z, s[:, :, 1]], axis=-1)
    return jnp.concatenate([top, bot], axis=-2)


def _rwkv_state_from_pairs(s):
    b = s.shape[0]
    s0 = s[:, :, :RWKV_HEAD, :RWKV_HEAD]
    s1 = s[:, :, RWKV_HEAD:, RWKV_HEAD:]
    return jnp.stack([s0, s1], axis=2).reshape(b, RWKV_HEADS, RWKV_HEAD, RWKV_HEAD)


def _trunk(x, gdn_s, gdn_conv, rwkv_s, rwkv_shift, p, gdn_chunk, rwkv_chunk):
    b, t, _ = x.shape
    m = b * t
    x2d = x.reshape(m, D_MODEL)
    tm = _pick_tile(m, 512)
    qkvz, ba = _gdn_in(x2d, p["nw0"], p["w_qkvz"], p["w_ba"], tm, 1024)
    o, s_gdn, conv_new = _gdn_chunk(qkvz.reshape(b, t, GDN_QKVZ_DIM), ba.reshape(b, t, LANES), gdn_conv, gdn_s,
                                    p["conv_wt"], p["alog_row"], p["dtb_row"], gdn_chunk)
    tm2 = _pick_tile(m, 256)
    x1 = _gdn_out(o.reshape(m, GDN_VALUE_DIM), qkvz, x2d, p["gn_w"], p["w_out"], tm2)
    if t >= 256:
        tb, tt = 1, 256
    elif t * b <= 256:
        tb, tt = b, t
    else:
        tb, tt = 256 // t, t
    r, k, v, ld, kn, bvec, z, bonus, last = _rwkv_in(
        x1.reshape(b, t, D_MODEL), rwkv_shift.reshape(b, 1, D_MODEL), p["nw1"], p["mu"],
        p["wr"], p["wk"], p["wv"], p["wz"], p["w0"], p["w1"], p["w2"], p["a0"], p["a1"], p["a2"],
        p["k_k"], p["k_a"], p["r_k"], p["ones_bd"], tb, tt)
    sh3 = lambda a: a.reshape(b, t, D_MODEL)
    y, s_rwkv_bd = _rwkv_rec(sh3(r), sh3(k), sh3(v), sh3(ld), sh3(kn), sh3(bvec), _rwkv_state_to_pairs(rwkv_s), rwkv_chunk)
    out = _rwkv_out(y.reshape(m, D_MODEL), z, bonus, x1, p["lnx_w"], p["lnx_b"], p["avg_bd"], p["w_o"], p["fw"], tm2)
    return (out.reshape(b, t, D_MODEL), s_gdn, conv_new, _rwkv_state_from_pairs(s_rwkv_bd), last.reshape(b, D_MODEL))


def kernel(x_prompt, x_sample, state_gdn, state_gdn_conv, state_rwkv, state_rwkv_shift, meta_tokens, norm_w, final_norm_w, gdn_w_in, gdn_conv_w, gdn_a_log, gdn_dt_bias, gdn_norm_w, gdn_w_out, rwkv_mu, rwkv_w_rkvz, rwkv_w0, rwkv_w1, rwkv_w2, rwkv_a0, rwkv_a1, rwkv_a2, rwkv_k_k, rwkv_k_a, rwkv_r_k, rwkv_lnx_w, rwkv_lnx_b, rwkv_w_o):
    assert norm_w.shape[0] == 2 and gdn_w_in.shape[0] == 1 and rwkv_mu.shape[0] == 1
    row = lambda a: a.reshape(1, -1).astype(F32)
    w_in = gdn_w_in[0]
    w_ba = jnp.pad(w_in[:, GDN_QKVZ_DIM:], ((0, 0), (0, LANES - 2 * GDN_V_HEADS)))
    gate_row = lambda a: jnp.pad(a.astype(F32), (GDN_V_HEADS, LANES - 2 * GDN_V_HEADS)).reshape(1, LANES)
    rb, cb = jnp.arange(LANES)[:, None] // RWKV_HEAD, jnp.arange(LANES)[None, :] // RWKV_HEAD
    bd = (rb == cb)
    p = {
        "nw0": row(norm_w[0]), "nw1": row(norm_w[1]), "fw": row(final_norm_w),
        "w_qkvz": w_in[:, :GDN_QKVZ_DIM].astype(BF16), "w_ba": w_ba.astype(BF16),
        "conv_wt": gdn_conv_w[0].T.astype(F32),
        "alog_row": gate_row(gdn_a_log[0]), "dtb_row": gate_row(gdn_dt_bias[0]),
        "gn_w": row(gdn_norm_w[0]), "w_out": gdn_w_out[0].astype(BF16),
        "mu": rwkv_mu[0].astype(F32),
        "wr": rwkv_w_rkvz[0, 0].astype(BF16), "wk": rwkv_w_rkvz[0, 1].astype(BF16),
        "wv": rwkv_w_rkvz[0, 2].astype(BF16), "wz": rwkv_w_rkvz[0, 3].astype(BF16),
        "w0": row(rwkv_w0[0]), "w1": rwkv_w1[0].astype(BF16), "w2": rwkv_w2[0].astype(BF16),
        "a0": row(rwkv_a0[0]), "a1": rwkv_a1[0].astype(BF16), "a2": rwkv_a2[0].astype(BF16),
        "k_k": row(rwkv_k_k[0]), "k_a": row(rwkv_k_a[0]), "r_k": row(rwkv_r_k[0]),
        "lnx_w": row(rwkv_lnx_w[0]), "lnx_b": row(rwkv_lnx_b[0]), "w_o": rwkv_w_o[0].astype(BF16),
        "ones_bd": bd.astype(BF16), "avg_bd": (bd.astype(F32) / RWKV_HEAD).astype(BF16),
    }
    bp = x_prompt.shape[0]
    zeros = lambda *s: jnp.zeros(s, F32)
    _, m_gdn, m_conv, m_rwkv, m_shift = _trunk(
        meta_tokens.astype(F32)[None], zeros(1, GDN_V_HEADS, GDN_HEAD, GDN_HEAD), zeros(1, CONV_W - 1, GDN_CONV_DIM),
        zeros(1, RWKV_HEADS, RWKV_HEAD, RWKV_HEAD), zeros(1, D_MODEL), p, N_META, N_META)
    rep = lambda a: jnp.broadcast_to(a, (bp,) + a.shape[1:])
    y_p, p_gdn, p_conv, p_rwkv, p_shift = _trunk(
        x_prompt, rep(m_gdn), rep(m_conv), rep(m_rwkv), rep(m_shift), p, GDN_CHUNK, RWKV_CHUNK)
    ts = x_sample.shape[1]
    y_s, s_gdn, s_conv, s_rwkv, s_shift = _trunk(
        x_sample, state_gdn[0], state_gdn_conv[0], state_rwkv[0], state_rwkv_shift[0], p, ts, ts)
    return (y_p, y_s, p_gdn[None], p_conv[None], p_rwkv[None], p_shift[None],
            s_gdn[None], s_conv[None], s_rwkv[None], s_shift[None])
```

```python
import functools

import jax
import jax.numpy as jnp
from jax import lax
from jax.experimental import pallas as pl
from jax.experimental.pallas import tpu as pltpu

F32 = jnp.float32
BF16 = jnp.bfloat16

D_MODEL = 1024
N_META = 16
GDN_QK_HEADS = 8
GDN_V_HEADS = 16
GDN_HEAD = 128
GDN_KEY_DIM = GDN_QK_HEADS * GDN_HEAD
GDN_VALUE_DIM = GDN_V_HEADS * GDN_HEAD
GDN_CONV_DIM = 2 * GDN_KEY_DIM + GDN_VALUE_DIM
GDN_QKVZ_DIM = GDN_CONV_DIM + GDN_VALUE_DIM
CONV_W = 4
GDN_CHUNK = 64
RWKV_HEAD = 64
RWKV_HEADS = D_MODEL // RWKV_HEAD
RWKV_PAIRS = RWKV_HEADS // 2
RWKV_CHUNK = 64
LORA = 64
RMS_EPS = 1e-6
L2_EPS = 1e-6
GN_EPS = 64e-5
LANES = 128
SUBLANES = 8
VMEM_LIMIT_BYTES = 56 * 1024 * 1024
CHUNKS_PER_STEP = 2
ROWS_PER_STEP = 4


def _bdot(a, b):
    return jnp.dot(a.astype(BF16), b.astype(BF16), preferred_element_type=F32)


def _bdot_nt(a, b):
    return lax.dot_general(a.astype(BF16), b.astype(BF16), (((1,), (1,)), ((), ())), preferred_element_type=F32)


def _bdot_tn(a, b):
    return lax.dot_general(a.astype(BF16), b.astype(BF16), (((0,), (0,)), ((), ())), preferred_element_type=F32)


def _split2(a):
    hi = a.astype(BF16)
    lo = (a - hi.astype(F32)).astype(BF16)
    return hi, lo


def _split3(a):
    hi = a.astype(BF16)
    r1 = a - hi.astype(F32)
    mid = r1.astype(BF16)
    lo = (r1 - mid.astype(F32)).astype(BF16)
    return hi, mid, lo


def _dot_exact_lhs(m_bf16, x, dims):
    xs = _split3(x)
    out = None
    for part in xs:
        if dims == "nn":
            t = jnp.dot(m_bf16, part, preferred_element_type=F32)
        else:
            t = lax.dot_general(part, m_bf16, (((0,), (1,)), ((), ())), preferred_element_type=F32)
        out = t if out is None else out + t
    return out


def _rms(x, w):
    return x * lax.rsqrt(jnp.mean(x * x, axis=-1, keepdims=True) + RMS_EPS) * w


def _silu(x):
    return x * jax.nn.sigmoid(x)


def _softplus(x):
    return jnp.maximum(x, 0.0) + jnp.log1p(jnp.exp(-jnp.abs(x)))


def _iota2(n, m):
    return lax.broadcasted_iota(jnp.int32, (n, m), 0), lax.broadcasted_iota(jnp.int32, (n, m), 1)


def _log2(n):
    l = n.bit_length() - 1
    assert (1 << l) == n, n
    return l


def _tri_inv_many(lows, n, blk):
    row, col = _iota2(n, n)
    eye = jnp.where(row == col, 1.0, 0.0).astype(BF16)
    negs = [(-low).astype(BF16) for low in lows]
    s = 1
    ts = None
    while s < blk:
        sh = _log2(s)
        sub_r = row >> sh
        sub_c = col >> sh
        m = ((sub_r >> 1) == (sub_c >> 1)) & ((sub_r & 1) == 1) & ((sub_c & 1) == 0)
        if ts is None:
            ts = [jnp.where(m, ng, eye) for ng in negs]
        else:
            xs = [jnp.dot(t, ng, preferred_element_type=F32).astype(BF16) for t, ng in zip(ts, negs)]
            zs = [jnp.dot(x, t, preferred_element_type=F32).astype(BF16) for x, t in zip(xs, ts)]
            ts = [jnp.where(m, z, t) for z, t in zip(zs, ts)]
        s *= 2
    return [eye for _ in lows] if ts is None else ts


def _gdn_in_kernel(x_ref, nw_ref, w_ref, wba_ref, qkvz_ref, ba_ref, xn_scr):
    n = pl.program_id(1)

    @pl.when(n == 0)
    def _():
        xn = _rms(x_ref[...], nw_ref[...])
        xb = xn.astype(BF16)
        xn_scr[...] = xb
        ba_ref[...] = jnp.dot(xb, wba_ref[...], preferred_element_type=F32)

    qkvz_ref[...] = jnp.dot(xn_scr[...], w_ref[...], preferred_element_type=F32)


def _gdn_in(x2d, nw, w_qkvz, w_ba, tm, tn):
    m = x2d.shape[0]
    return pl.pallas_call(
        _gdn_in_kernel,
        grid=(m // tm, GDN_QKVZ_DIM // tn),
        in_specs=[
            pl.BlockSpec((tm, D_MODEL), lambda i, j: (i, 0)),
            pl.BlockSpec((1, D_MODEL), lambda i, j: (0, 0)),
            pl.BlockSpec((D_MODEL, tn), lambda i, j: (0, j)),
            pl.BlockSpec((D_MODEL, LANES), lambda i, j: (0, 0)),
        ],
        out_specs=[
            pl.BlockSpec((tm, tn), lambda i, j: (i, j)),
            pl.BlockSpec((tm, LANES), lambda i, j: (i, 0)),
        ],
        out_shape=[
            jax.ShapeDtypeStruct((m, GDN_QKVZ_DIM), F32),
            jax.ShapeDtypeStruct((m, LANES), F32),
        ],
        scratch_shapes=[pltpu.VMEM((tm, D_MODEL), BF16)],
        compiler_params=pltpu.CompilerParams(
            dimension_semantics=("parallel", "arbitrary"), vmem_limit_bytes=VMEM_LIMIT_BYTES),
        name="gdn_in",
    )(x2d, nw, w_qkvz, w_ba)


def _gdn_chunk_kernel(qkv_ref, ba_ref, buf_ref, s0_ref, cw_ref, alog_ref, dtb_ref,
                      o_ref, sout_ref, nbuf_ref, s_scr, up_scr, *, chunk, gb, gt, n_steps):
    step = pl.program_id(1)
    C = chunk
    C2 = 2 * C
    L = gt * C
    pad = SUBLANES - (CONV_W - 1)
    carried = n_steps * gt > 1

    @pl.when(step == 0)
    def _():
        if carried:
            s_scr[...] = s0_ref[...]
        up_scr[:, 0:SUBLANES, :] = jnp.zeros((gb, SUBLANES, GDN_CONV_DIM), F32)
        up_scr[:, pad:SUBLANES, :] = buf_ref[...]

    up_scr[:, SUBLANES:SUBLANES + L, :] = qkv_ref[...]

    r1, c1 = _iota2(C, C)
    tril = jnp.where(r1 >= c1, 1.0, 0.0).astype(BF16)
    r2, c2 = _iota2(C2, C)
    tril2 = jnp.where((r2 & (C - 1)) >= c2, 1.0, 0.0).astype(BF16)
    row, col = _iota2(C2, C2)
    causal = ((row >= C) == (col >= C)) & (row >= col)
    top =lax.broadcasted_iota(jnp.int32, (C2, 1), 0) < C
    left = lax.broadcasted_iota(jnp.int32, (1, C2), 1) < C

    pairs = []
    for r in range(gb):
        y = up_scr[r, pad:pad + L, :] * cw_ref[0:1, :]
        for j in range(1, CONV_W):
            y = y + up_scr[r, pad + j:pad + j + L, :] * cw_ref[j:j + 1, :]
        act = _silu(y)
        ba = ba_ref[r]
        beta_l = jax.nn.sigmoid(ba)
        g_l = -jnp.exp(alog_ref[...]) * _softplus(ba + dtb_ref[...])
        for j in range(gt):
            rows = slice(j * C, (j + 1) * C)
            beta_all = beta_l[rows]
            gc_all = _dot_exact_lhs(tril, g_l[rows], "nn")
            gct_all = _dot_exact_lhs(tril2, g_l[rows], "tn")
            for qh in range(GDN_QK_HEADS):
                q = act[rows, qh * GDN_HEAD:(qh + 1) * GDN_HEAD]
                k = act[rows, GDN_KEY_DIM + qh * GDN_HEAD:GDN_KEY_DIM + (qh + 1) * GDN_HEAD]
                q = q * lax.rsqrt(jnp.sum(q * q, axis=-1, keepdims=True) + L2_EPS) * (GDN_HEAD ** -0.5)
                k = k * lax.rsqrt(jnp.sum(k * k, axis=-1, keepdims=True) + L2_EPS)
                h0 = 2 * qh
                h1 = h0 + 1
                a0, a1 = GDN_V_HEADS + h0, GDN_V_HEADS + h1
                beta_s = jnp.concatenate([beta_all[:, h0:h0 + 1], beta_all[:, h1:h1 + 1]], axis=0)
                gc_s = jnp.concatenate([gc_all[:, a0:a0 + 1], gc_all[:, a1:a1 + 1]], axis=0)
                gc_row = jnp.where(left, gct_all[a0:a0 + 1, :], gct_all[a1:a1 + 1, :])
                glast0 = gc_all[C - 1:C, a0:a0 + 1]
                glast1 = gc_all[C - 1:C, a1:a1 + 1]
                glast_s = jnp.where(top, glast0, glast1)
                decay = jnp.exp(jnp.where(causal, gc_s - gc_row, -jnp.inf))
                k_s = jnp.concatenate([k, k], axis=0)
                q_s = jnp.concatenate([q, q], axis=0)
                v0 = 2 * GDN_KEY_DIM + h0 * GDN_HEAD
                v_s = jnp.concatenate([act[rows, v0:v0 + GDN_HEAD], act[rows, v0 + GDN_HEAD:v0 + 2 * GDN_HEAD]], axis=0)
                egc = jnp.exp(gc_s)
                kb_s = k_s * beta_s
                pairs.append(dict(
                    r=r, j=j, h0=h0, h1=h1,
                    low=_bdot_nt(kb_s, k_s) * decay,
                    rhs=jnp.concatenate([v_s * beta_s, kb_s * egc], axis=1).astype(BF16),
                    attn=_bdot_nt(q_s, k_s) * decay, qd_s=q_s * egc, kd_s=k_s * jnp.exp(glast_s - gc_s),
                    eg0=jnp.exp(glast0), eg1=jnp.exp(glast1)))

    tail = up_scr[:, L:L + SUBLANES, :]
    up_scr[:, 0:SUBLANES, :] = tail

    @pl.when(step == n_steps - 1)
    def _():
        nbuf_ref[...] = tail[:, pad:SUBLANES, :]

    tinvs = _tri_inv_many([pr["low"] for pr in pairs], C2, C)
    sols = [jnp.dot(t, pr["rhs"], preferred_element_type=F32) for t, pr in zip(tinvs, pairs)]

    def state_in(r, h):
        return s_scr[r, h] if carried else s0_ref[r, h]

    def state_out(r, h, val):
        if carried:
            s_scr[r, h] = val
        else:
            sout_ref[r, h] = val

    for r in range(gb):
        for j in range(gt):
            grp = [(pr, sol) for pr, sol in zip(pairs, sols) if pr["r"] == r and pr["j"] == j]
            rows = slice(j * C, (j + 1) * C)
            sts, wqs = [], []
            for pr, sol in grp:
                w_s = sol[:, GDN_HEAD:]
                st0 = state_in(r, pr["h0"])
                st1 = state_in(r, pr["h1"])
                sts.append((st0, st1))
                wqs.append((_bdot(jnp.concatenate([w_s[:C], pr["qd_s"][:C]], axis=0), st0),
                            _bdot(jnp.concatenate([w_s[C:], pr["qd_s"][C:]], axis=0), st1)))
            for (pr, sol), (st0, st1), (wq0, wq1) in zip(grp, sts, wqs):
                h0, h1 = pr["h0"], pr["h1"]
                u_s = sol[:, :GDN_HEAD]
                vn0 = u_s[:C] - wq0[:C]
                vn1 = u_s[C:] - wq1[:C]
                o_s = _bdot(pr["attn"], jnp.concatenate([vn0, vn1], axis=0))
                o_ref[r, rows, h0 * GDN_HEAD:(h0 + 1) * GDN_HEAD] = o_s[:C] + wq0[C:]
                o_ref[r, rows, h1 * GDN_HEAD:(h1 + 1) * GDN_HEAD] = o_s[C:] + wq1[C:]
                state_out(r, h0, st0 * pr["eg0"] + _bdot_tn(pr["kd_s"][:C], vn0))
                state_out(r, h1, st1 * pr["eg1"] + _bdot_tn(pr["kd_s"][C:], vn1))

    if carried:
        @pl.when(step == n_steps - 1)
        def _():
            sout_ref[...] = s_scr[...]


def _gdn_chunk(qkvz, ba, conv_buf, s0, conv_wt, alog_row, dtb_row, chunk, gb, gt):
    b, t, _ = qkvz.shape
    span = gt * chunk
    n_steps = t // span
    assert b % gb == 0 and t % span == 0 and (gb == 1 or n_steps == 1)
    kern = functools.partial(_gdn_chunk_kernel, chunk=chunk, gb=gb, gt=gt, n_steps=n_steps)
    state = pl.BlockSpec((gb, GDN_V_HEADS, GDN_HEAD, GDN_HEAD), lambda i, c: (i, 0, 0, 0))
    buf = pl.BlockSpec((gb, CONV_W - 1, GDN_CONV_DIM), lambda i, c: (i, 0, 0))
    return pl.pallas_call(
        kern,
        grid=(b // gb, n_steps),
        in_specs=[
            pl.BlockSpec((gb, span, GDN_CONV_DIM), lambda i, c: (i, c, 0)),
            pl.BlockSpec((gb, span, LANES), lambda i, c: (i, c, 0)),
            buf, state,
            pl.BlockSpec((CONV_W, GDN_CONV_DIM), lambda i, c: (0, 0)),
            pl.BlockSpec((1, LANES), lambda i, c: (0, 0)),
            pl.BlockSpec((1, LANES), lambda i, c: (0, 0)),
        ],
        out_specs=[pl.BlockSpec((gb, span, GDN_VALUE_DIM), lambda i, c: (i, c, 0)), state, buf],
        out_shape=[
            jax.ShapeDtypeStruct((b, t, GDN_VALUE_DIM), F32),
            jax.ShapeDtypeStruct((b, GDN_V_HEADS, GDN_HEAD, GDN_HEAD), F32),
            jax.ShapeDtypeStruct((b, CONV_W - 1, GDN_CONV_DIM), F32),
        ],
        scratch_shapes=[
            pltpu.VMEM((gb if n_steps * gt > 1 else 1, GDN_V_HEADS, GDN_HEAD, GDN_HEAD), F32),
            pltpu.VMEM((gb, span + SUBLANES, GDN_CONV_DIM), F32),
        ],
        compiler_params=pltpu.CompilerParams(
            dimension_semantics=("parallel", "arbitrary"), vmem_limit_bytes=VMEM_LIMIT_BYTES),
        name="gdn_chunk",
    )(qkvz, ba, conv_buf, s0, conv_wt, alog_row, dtb_row)


def _gdn_out_kernel(o_ref, z_ref, x_ref, gw_ref, w_ref, y_ref):
    gw = gw_ref[...]
    parts = []
    for h in range(GDN_V_HEADS):
        sl = slice(h * GDN_HEAD, (h + 1) * GDN_HEAD)
        parts.append((_rms(o_ref[:, sl], gw) * _silu(z_ref[:, sl])).astype(BF16))
    gated = jnp.concatenate(parts, axis=1)
    y_ref[...] = x_ref[...] + jnp.dot(gated, w_ref[...], preferred_element_type=F32)


def _gdn_out(o2d, qkvz2d, x2d, gw, w_out, tm):
    m = x2d.shape[0]
    z_block = GDN_CONV_DIM // GDN_VALUE_DIM
    return pl.pallas_call(
        _gdn_out_kernel,
        grid=(m // tm,),
        in_specs=[
            pl.BlockSpec((tm, GDN_VALUE_DIM), lambda i: (i, 0)),
            pl.BlockSpec((tm, GDN_VALUE_DIM), lambda i: (i, z_block)),
            pl.BlockSpec((tm, D_MODEL), lambda i: (i, 0)),
            pl.BlockSpec((1, GDN_HEAD), lambda i: (0, 0)),
            pl.BlockSpec((GDN_VALUE_DIM, D_MODEL), lambda i: (0, 0)),
        ],
        out_specs=pl.BlockSpec((tm, D_MODEL), lambda i: (i, 0)),
        out_shape=jax.ShapeDtypeStruct((m, D_MODEL), F32),
        compiler_params=pltpu.CompilerParams(
            dimension_semantics=("parallel",), vmem_limit_bytes=VMEM_LIMIT_BYTES),
        name="gdn_out",
    )(o2d, qkvz2d, x2d, gw, w_out)


def _head_sum(x, ones_bd):
    parts = []
    for p in range(RWKV_PAIRS):
        hi, lo = _split2(x[:, p * LANES:(p + 1) * LANES])
        parts.append(jnp.dot(hi, ones_bd, preferred_element_type=F32) + jnp.dot(lo, ones_bd, preferred_element_type=F32))
    return jnp.concatenate(parts, axis=1)


def _rwkv_in_kernel(x_ref, xp_ref, sh_ref, nw_ref, mu_ref, wr_ref, wk_ref, wv_ref, wz_ref,
                    w0_ref, w1_ref, w2_ref, a0_ref, a1_ref, a2_ref, kk_ref, ka_ref, rk_ref, ones_ref,
                    r_out, k_out, v_out, ld_out, kn_out, b_out, z_out, bonus_out, last_out, *, tb, tt, n_t):
    t = pl.program_id(1)
    m = tb * tt
    nw = nw_ref[...]
    xn = _rms(x_ref[...].reshape(m, D_MODEL), nw)
    prev_last = _rms(xp_ref[:, SUBLANES - 1:SUBLANES, :], nw)
    first = jnp.where(t == 0, sh_ref[...], prev_last)
    first = jnp.broadcast_to(first, (tb, tt, D_MODEL)).reshape(m, D_MODEL)
    rolled = pltpu.roll(xn, 1, axis=0)
    rows = lax.broadcasted_iota(jnp.int32, (m, 1), 0)
    xprev = jnp.where((rows & (tt - 1)) == 0, first, rolled)
    xx = xprev - xn
    mu = mu_ref[...]

    def mix(i):
        return xn + xx * mu[i:i + 1, :]

    r = _bdot(mix(0), wr_ref[...])
    k = _bdot(mix(1), wk_ref[...])
    v = _bdot(mix(2), wv_ref[...])
    z = _bdot(mix(3), wz_ref[...])
    w = w0_ref[...] + _bdot(jnp.tanh(_bdot(mix(4), w1_ref[...])), w2_ref[...])
    w = -_softplus(-w) - 0.5
    a = jax.nn.sigmoid(a0_ref[...] + _bdot(_bdot(mix(5), a1_ref[...]), a2_ref[...]))
    ones_bd = ones_ref[...]
    kk = k * kk_ref[...]
    kn = kk * lax.rsqrt(_head_sum(kk * kk, ones_bd) + L2_EPS)
    k = k * (1.0 + (a - 1.0) * ka_ref[...])
    r_out[...] = r
    k_out[...] = k
    v_out[...] = v
    z_out[...] = z
    ld_out[...] = -jnp.exp(w)
    kn_out[...] = kn
    b_out[...] = kn * a
    bonus_out[...] = _head_sum(r * k * rk_ref[...], ones_bd) * v

    @pl.when(t == n_t - 1)
    def _():
        last_out[...] = _rms(x_ref[:, tt - 1:tt, :], nw)


def _rwkv_in(x3d, shift, nw, mu, wr, wk, wv, wz, w0, w1, w2, a0, a1, a2, k_k, k_a, r_k, ones_bd, tb, tt):
    b, t, _ = x3d.shape
    n_t = t // tt
    tpb = tt // SUBLANES
    kern = functools.partial(_rwkv_in_kernel, tb=tb, tt=tt, n_t=n_t)
    row = lambda n: pl.BlockSpec((n, D_MODEL), lambda i, j: (0, 0))
    big = pl.BlockSpec((D_MODEL, D_MODEL), lambda i, j: (0, 0))
    tok = pl.BlockSpec((tb * tt, D_MODEL), lambda i, j: (i * n_t + j, 0))
    m = b * t
    outs = pl.pallas_call(
        kern,
        grid=(b // tb, n_t),
        in_specs=[
            pl.BlockSpec((tb, tt, D_MODEL), lambda i, j: (i, j, 0)),
            pl.BlockSpec((tb, SUBLANES, D_MODEL), lambda i, j: (i, jnp.maximum(j * tpb - 1, 0), 0)),
            pl.BlockSpec((tb, 1, D_MODEL), lambda i, j: (i, 0, 0)),
            row(1), row(6), big, big, big, big,
            row(1), pl.BlockSpec((D_MODEL, LORA), lambda i, j: (0, 0)), pl.BlockSpec((LORA, D_MODEL), lambda i, j: (0, 0)),
            row(1), pl.BlockSpec((D_MODEL, LORA), lambda i, j: (0, 0)), pl.BlockSpec((LORA, D_MODEL), lambda i, j: (0, 0)),
            row(1), row(1), row(1),
            pl.BlockSpec((LANES, LANES), lambda i, j: (0, 0)),
        ],
        out_specs=[tok] * 8 + [pl.BlockSpec((tb, 1, D_MODEL), lambda i, j: (i, 0, 0))],
        out_shape=[jax.ShapeDtypeStruct((m, D_MODEL), F32)] * 8 + [jax.ShapeDtypeStruct((b, 1, D_MODEL), F32)],
        compiler_params=pltpu.CompilerParams(
            dimension_semantics=("parallel", "arbitrary"), vmem_limit_bytes=VMEM_LIMIT_BYTES),
        name="rwkv_in",
    )(x3d, x3d, shift, nw, mu, wr, wk, wv, wz, w0, w1, w2, a0, a1, a2, k_k, k_a, r_k, ones_bd)
    return outs


def _rwkv_rec_kernel(r_ref, k_ref, v_ref, ld_ref, kn_ref, b_ref, s0_ref, y_ref, sout_ref, s_scr,
                     *, chunk, gb, gt, n_steps):
    step = pl.program_id(1)
    C = chunk
    C2 = 2 * C
    carried = n_steps * gt > 1

    if carried:
        @pl.when(step == 0)
        def _():
            s_scr[...] = s0_ref[...]

    r1, c1 = _iota2(C, C)
    tril = jnp.where(r1 >= c1, 1.0, 0.0).astype(BF16)
    lane = lax.broadcasted_iota(jnp.int32, (1, LANES), 1)
    m_lo = lane < RWKV_HEAD
    rc, cc = _iota2(C, C2)
    tl = cc & (C - 1)
    strict_pair = rc > tl
    incl_pair = rc >= tl

    def stack(x):
        return jnp.concatenate([jnp.where(m_lo, x, 0.0), jnp.where(m_lo, 0.0, x)], axis=0)

    pairs = []
    for rr in range(gb):
        for j in range(gt):
            rows = slice(j * C, (j + 1) * C)
            for p in range(RWKV_PAIRS):
                sl = slice(p * LANES, (p + 1) * LANES)
                pairs.append(dict(rr=rr, j=j, p=p, rows=rows, sl=sl))
    for pr in pairs:
        pr["gc"] = _dot_exact_lhs(tril, ld_ref[pr["rr"], pr["rows"], pr["sl"]], "nn")
    for pr in pairs:
        at_ref = (pr["rr"], pr["rows"], pr["sl"])
        gc = pr["gc"]
        k = k_ref[at_ref]
        b = b_ref[at_ref]
        glast = gc[C - 1:C, :]
        e_neg = jnp.exp(-gc)
        e_end = jnp.exp(glast - gc)
        rt = r_ref[at_ref] * jnp.exp(gc)
        at_s = stack(-kn_ref[at_ref] * jnp.exp(gc - ld_ref[at_ref]))
        bt_s = stack(b * e_neg)
        pr.update(rt=rt, at_s=at_s, v_s=stack(v_ref[at_ref]).astype(BF16),
                  kh_s=stack(k * e_end), bh_s=stack(b * e_end).astype(BF16), eg=jnp.exp(glast))
        ars = jnp.concatenate([at_s, rt], axis=0)
        g = _bdot_nt(ars, jnp.concatenate([stack(k * e_neg), bt_s], axis=0))
        pr["low"] = -g[:C2, C2:]
        pr["rb"] = jnp.where(incl_pair, g[C2:, C2:], 0.0).astype(BF16)
        pr["rk"] = jnp.where(incl_pair, g[C2:, :C2], 0.0).astype(BF16)
        pr["ak"] = jnp.where(strict_pair, g[:C, :C2] + g[C:C2, :C2], 0.0)
    for pr in pairs:
        akv = jnp.dot(pr["ak"].astype(BF16), pr["v_s"], preferred_element_type=F32)
        pr["rhs"] = jnp.concatenate([pr["at_s"], stack(akv)], axis=1).astype(BF16)

    tinvs = _tri_inv_many([pr["low"] for pr in pairs], C2, C)
    sols = [jnp.dot(t, pr["rhs"], preferred_element_type=F32).astype(BF16) for t, pr in zip(tinvs, pairs)]
    for pr, sol in zip(pairs, sols):
        a2_s = sol[:, :LANES]
        u2_s = sol[:, LANES:]
        d = functools.partial(jnp.dot, preferred_element_type=F32)
        pr["r2"] = pr["rt"] + d(pr["rb"], a2_s)
        pr["y2"] = d(pr["rk"], pr["v_s"]) + d(pr["rb"], u2_s)
        pr["w2"] = _bdot_tn(a2_s, pr["bh_s"])
        pr["n2"] = _bdot_tn(jnp.concatenate([pr["v_s"], u2_s], axis=0),
                            jnp.concatenate([pr["kh_s"].astype(BF16), pr["bh_s"]], axis=0))

    for pr in pairs:
        rr, p = pr["rr"], pr["p"]
        st = s_scr[rr, p] if carried else s0_ref[rr, p]
        y_ref[rr, pr["rows"], pr["sl"]] = _bdot_nt(pr["r2"], st) + pr["y2"]
        new = st * pr["eg"] + (_bdot(st, pr["w2"]) + pr["n2"])
        if carried:
            s_scr[rr, p] = new
        else:
            sout_ref[rr, p] = new

    if carried:
        @pl.when(step == n_steps - 1)
        def _():
            sout_ref[...] = s_scr[...]


def _rwkv_rec(r, k, v, ld, kn, bvec, s0_bd, chunk, gb, gt):
    b, t, _ = r.shape
    span = gt * chunk
    n_steps = t // span
    assert b % gb == 0 and t % span == 0 and (gb == 1 or n_steps == 1)
    kern = functools.partial(_rwkv_rec_kernel, chunk=chunk, gb=gb, gt=gt, n_steps=n_steps)
    tok = pl.BlockSpec((gb, span, D_MODEL), lambda i, c: (i, c, 0))
    st = pl.BlockSpec((gb, RWKV_PAIRS, LANES, LANES), lambda i, c: (i, 0, 0, 0))
    return pl.pallas_call(
        kern,
        grid=(b // gb, n_steps),
        in_specs=[tok] * 6 + [st],
        out_specs=[tok, st],
        out_shape=[
            jax.ShapeDtypeStruct((b, t, D_MODEL), F32),
            jax.ShapeDtypeStruct((b, RWKV_PAIRS, LANES, LANES), F32),
        ],
        scratch_shapes=[pltpu.VMEM((gb if n_steps * gt > 1 else 1, RWKV_PAIRS, LANES, LANES), F32)],
        compiler_params=pltpu.CompilerParams(
            dimension_semantics=("parallel", "arbitrary"), vmem_limit_bytes=VMEM_LIMIT_BYTES),
        name="rwkv_rec",
    )(r, k, v, ld, kn, bvec, s0_bd)


def _rwkv_out_kernel(y_ref, z_ref, bonus_ref, x_ref, lw_ref, lb_ref, avg_ref, w_ref, fw_ref, out_ref):
    y = y_ref[...]
    avg = avg_ref[...]
    mean = _head_sum(y, avg)
    yc = y - mean
    var = _head_sum(yc * yc, avg)
    gn = yc * lax.rsqrt(var + GN_EPS) * lw_ref[...] + lb_ref[...]
    o = (gn + bonus_ref[...]) * _silu(z_ref[...])
    x2 = x_ref[...] + _bdot(o, w_ref[...])
    out_ref[...] = _rms(x2, fw_ref[...])


def _rwkv_out(y2d, z2d, bonus2d, x2d, lnx_w, lnx_b, avg_bd, w_o, fw, tm):
    m = x2d.shape[0]
    tok = pl.BlockSpec((tm, D_MODEL), lambda i: (i, 0))
    row = pl.BlockSpec((1, D_MODEL), lambda i: (0, 0))
    return pl.pallas_call(
        _rwkv_out_kernel,
        grid=(m // tm,),
        in_specs=[tok, tok, tok, tok, row, row,
                  pl.BlockSpec((LANES, LANES), lambda i: (0, 0)),
                  pl.BlockSpec((D_MODEL, D_MODEL), lambda i: (0, 0)), row],
        out_specs=tok,
        out_shape=jax.ShapeDtypeStruct((m, D_MODEL), F32),
        compiler_params=pltpu.CompilerParams(
            dimension_semantics=("parallel",), vmem_limit_bytes=VMEM_LIMIT_BYTES),
        name="rwkv_out",
    )(y2d, z2d, bonus2d, x2d, lnx_w, lnx_b, avg_bd, w_o, fw)


def _pick_tile(m, cap):
    t = min(m, cap)
    while m % t:
        t //= 2
    return t


def _rwkv_state_to_pairs(s):
    b = s.shape[0]
    s = s.reshape(b, RWKV_PAIRS, 2, RWKV_HEAD, RWKV_HEAD)
    z = jnp.zeros((b, RWKV_PAIRS, RWKV_HEAD, RWKV_HEAD), s.dtype)
    top = jnp.concatenate([s[:, :, 0], z], axis=-1)
    bot = jnp.concatenate([z, s[:, :, 1]], axis=-1)
    return jnp.concatenate([top, bot], axis=-2)


def _rwkv_state_from_pairs(s):
    b = s.shape[0]
    s0 = s[:, :, :RWKV_HEAD, :RWKV_HEAD]
    s1 = s[:, :, RWKV_HEAD:, RWKV_HEAD:]
    return jnp.stack([s0, s1], axis=2).reshape(b, RWKV_HEADS, RWKV_HEAD, RWKV_HEAD)


def _step_groups(b, t, chunk):
    n_chunks = t // chunk
    if n_chunks > 1:
        return 1, _pick_tile(n_chunks, CHUNKS_PER_STEP)
    return _pick_tile(b, ROWS_PER_STEP), 1


def _trunk(x, gdn_s, gdn_conv, rwkv_s, rwkv_shift, p, gdn_chunk, rwkv_chunk):
    b, t, _ = x.shape
    m = b * t
    x2d = x.reshape(m, D_MODEL)
    tm = _pick_tile(m, 512)
    qkvz, ba = _gdn_in(x2d, p["nw0"], p["w_qkvz"], p["w_ba"], tm, 1024)
    o, s_gdn, conv_new = _gdn_chunk(qkvz.reshape(b, t, GDN_QKVZ_DIM), ba.reshape(b, t, LANES), gdn_conv, gdn_s,
                                    p["conv_wt"], p["alog_row"], p["dtb_row"], gdn_chunk, *_step_groups(b, t, gdn_chunk))
    tm2 = _pick_tile(m, 256)
    x1 = _gdn_out(o.reshape(m, GDN_VALUE_DIM), qkvz, x2d, p["gn_w"], p["w_out"], tm2)
    if t >= 256:
        tb, tt = 1, 256
    elif t * b <= 256:
        tb, tt = b, t
    else:
        tb, tt = 256 // t, t
    r, k, v, ld, kn, bvec, z, bonus, last = _rwkv_in(
        x1.reshape(b, t, D_MODEL), rwkv_shift.reshape(b, 1, D_MODEL), p["nw1"], p["mu"],
        p["wr"], p["wk"], p["wv"], p["wz"], p["w0"], p["w1"], p["w2"], p["a0"], p["a1"], p["a2"],
        p["k_k"], p["k_a"], p["r_k"], p["ones_bd"], tb, tt)
    sh3 = lambda a: a.reshape(b, t, D_MODEL)
    y, s_rwkv_bd = _rwkv_rec(sh3(r), sh3(k), sh3(v), sh3(ld), sh3(kn), sh3(bvec), _rwkv_state_to_pairs(rwkv_s),
                             rwkv_chunk, *_step_groups(b, t, rwkv_chunk))
    out = _rwkv_out(y.reshape(m, D_MODEL), z, bonus, x1, p["lnx_w"], p["lnx_b"], p["avg_bd"], p["w_o"], p["fw"], tm2)
    return (out.reshape(b, t, D_MODEL), s_gdn, conv_new, _rwkv_state_from_pairs(s_rwkv_bd), last.reshape(b, D_MODEL))


def kernel(x_prompt, x_sample, state_gdn, state_gdn_conv, state_rwkv, state_rwkv_shift, meta_tokens, norm_w, final_norm_w, gdn_w_in, gdn_conv_w, gdn_a_log, gdn_dt_bias, gdn_norm_w, gdn_w_out, rwkv_mu, rwkv_w_rkvz, rwkv_w0, rwkv_w1, rwkv_w2, rwkv_a0, rwkv_a1, rwkv_a2, rwkv_k_k, rwkv_k_a, rwkv_r_k, rwkv_lnx_w, rwkv_lnx_b, rwkv_w_o):
    assert norm_w.shape[0] == 2 and gdn_w_in.shape[0] == 1 and rwkv_mu.shape[0] == 1
    row = lambda a: a.reshape(1, -1).astype(F32)
    w_in = gdn_w_in[0]
    w_ba = jnp.pad(w_in[:, GDN_QKVZ_DIM:], ((0, 0), (0, LANES - 2 * GDN_V_HEADS)))
    gate_row = lambda a: jnp.pad(a.astype(F32), (GDN_V_HEADS, LANES - 2 * GDN_V_HEADS)).reshape(1, LANES)
    rb, cb = jnp.arange(LANES)[:, None] // RWKV_HEAD, jnp.arange(LANES)[None, :] // RWKV_HEAD
    bd = (rb == cb)
    p = {
        "nw0": row(norm_w[0]), "nw1": row(norm_w[1]), "fw": row(final_norm_w),
        "w_qkvz": w_in[:, :GDN_QKVZ_DIM].astype(BF16), "w_ba": w_ba.astype(BF16),
        "conv_wt": gdn_conv_w[0].T.astype(F32),
        "alog_row": gate_row(gdn_a_log[0]), "dtb_row": gate_row(gdn_dt_bias[0]),
        "gn_w": row(gdn_norm_w[0]), "w_out": gdn_w_out[0].astype(BF16),
        "mu": rwkv_mu[0].astype(F32),
        "wr": rwkv_w_rkvz[0, 0].astype(BF16), "wk": rwkv_w_rkvz[0, 1].astype(BF16),
        "wv": rwkv_w_rkvz[0, 2].astype(BF16), "wz": rwkv_w_rkvz[0, 3].astype(BF16),
        "w0": row(rwkv_w0[0]), "w1": rwkv_w1[0].astype(BF16), "w2": rwkv_w2[0].astype(BF16),
        "a0": row(rwkv_a0[0]), "a1": rwkv_a1[0].astype(BF16), "a2": rwkv_a2[0].astype(BF16),
        "k_k": row(rwkv_k_k[0]), "k_a": row(rwkv_k_a[0]), "r_k": row(rwkv_r_k[0]),
        "lnx_w": row(rwkv_lnx_w[0]), "lnx_b": row(rwkv_lnx_b[0]), "w_o": rwkv_w_o[0].astype(BF16),
        "ones_bd": bd.astype(BF16), "avg_bd": (bd.astype(F32) / RWKV_HEAD).astype(BF16),
    }
    bp = x_prompt.shape[0]
    zeros = lambda *s: jnp.zeros(s, F32)
    _, m_gdn, m_conv, m_rwkv, m_shift = _trunk(
        meta_tokens.astype(F32)[None], zeros(1, GDN_V_HEADS, GDN_HEAD, GDN_HEAD), zeros(1, CONV_W - 1, GDN_CONV_DIM),
        zeros(1, RWKV_HEADS, RWKV_HEAD, RWKV_HEAD), zeros(1, D_MODEL), p, N_META, N_META)
    rep = lambda a: jnp.broadcast_to(a, (bp,) + a.shape[1:])
    y_p, p_gdn, p_conv, p_rwkv, p_shift = _trunk(
        x_prompt, rep(m_gdn), rep(m_conv), rep(m_rwkv), rep(m_shift), p, GDN_CHUNK, RWKV_CHUNK)
    ts = x_sample.shape[1]
    y_s, s_gdn, s_conv, s_rwkv, s_shift = _trunk(
        x_sample, state_gdn[0], state_gdn_conv[0], state_rwkv[0], state_rwkv_shift[0], p, ts, ts)
    return (y_p, y_s, p_gdn[None], p_conv[None], p_rwkv[None], p_shift[None],
            s_gdn[None], s_conv[None], s_rwkv[None], s_shift[None])
```

```python
import functools

import jax
import jax.numpy as jnp
from jax import lax
from jax.experimental import pallas as pl
from jax.experimental.pallas import tpu as pltpu

F32 = jnp.float32
BF16 = jnp.bfloat16

D_MODEL = 1024
N_META = 16
GDN_QK_HEADS = 8
GDN_V_HEADS = 16
GDN_HEAD = 128
GDN_KEY_DIM = GDN_QK_HEADS * GDN_HEAD
GDN_VALUE_DIM = GDN_V_HEADS * GDN_HEAD
GDN_CONV_DIM = 2 * GDN_KEY_DIM + GDN_VALUE_DIM
GDN_QKVZ_DIM = GDN_CONV_DIM + GDN_VALUE_DIM
CONV_W = 4
GDN_TN = 1024
GDN_CONV_TILES = GDN_CONV_DIM // GDN_TN
GDN_SLAB = 256
GDN_CHUNK = 64
RWKV_HEAD = 64
RWKV_HEADS = D_MODEL // RWKV_HEAD
RWKV_PAIRS = RWKV_HEADS // 2
RWKV_CHUNK = 64
LORA = 64
RMS_EPS = 1e-6
L2_EPS = 1e-6
GN_EPS = 64e-5
LANES = 128
SUBLANES = 8
VMEM_LIMIT_BYTES = 56 * 1024 * 1024
CHUNKS_PER_STEP = 2
ROWS_PER_STEP = 4


def _bdot(a, b):
    return jnp.dot(a.astype(BF16), b.astype(BF16), preferred_element_type=F32)


def _bdot_nt(a, b):
    return lax.dot_general(a.astype(BF16), b.astype(BF16), (((1,), (1,)), ((), ())), preferred_element_type=F32)


def _bdot_tn(a, b):
    return lax.dot_general(a.astype(BF16), b.astype(BF16), (((0,), (0,)), ((), ())), preferred_element_type=F32)


def _split2(a):
    hi = a.astype(BF16)
    lo = (a - hi.astype(F32)).astype(BF16)
    return hi, lo


def _split3(a):
    hi = a.astype(BF16)
    r1 = a - hi.astype(F32)
    mid = r1.astype(BF16)
    lo = (r1 - mid.astype(F32)).astype(BF16)
    return hi, mid, lo


def _dot_exact_lhs(m_bf16, x, dims):
    xs = _split3(x)
    out = None
    for part in xs:
        if dims == "nn":
            t = jnp.dot(m_bf16, part, preferred_element_type=F32)
        else:
            t = lax.dot_general(part, m_bf16, (((0,), (1,)), ((), ())), preferred_element_type=F32)
        out = t if out is None else out + t
    return out


def _rms(x, w):
    return x * lax.rsqrt(jnp.mean(x * x, axis=-1, keepdims=True) + RMS_EPS) * w


def _silu(x):
    return x * jax.nn.sigmoid(x)


def _softplus(x):
    return jnp.maximum(x, 0.0) + jnp.log1p(jnp.exp(-jnp.abs(x)))


def _iota2(n, m):
    return lax.broadcasted_iota(jnp.int32, (n, m), 0), lax.broadcasted_iota(jnp.int32, (n, m), 1)


def _log2(n):
    l = n.bit_length() - 1
    assert (1 << l) == n, n
    return l


def _tri_inv_many(lows, n, blk):
    row, col = _iota2(n, n)
    eye = jnp.where(row == col, 1.0, 0.0).astype(BF16)
    negs = [(-low).astype(BF16) for low in lows]
    s = 1
    ts = None
    while s < blk:
        sh = _log2(s)
        sub_r = row >> sh
        sub_c = col >> sh
        m = ((sub_r >> 1) == (sub_c >> 1)) & ((sub_r & 1) == 1) & ((sub_c & 1) == 0)
        if ts is None:
            ts = [jnp.where(m, ng, eye) for ng in negs]
        else:
            xs = [jnp.dot(t, ng, preferred_element_type=F32).astype(BF16) for t, ng in zip(ts, negs)]
            zs = [jnp.dot(x, t, preferred_element_type=F32).astype(BF16) for x, t in zip(xs, ts)]
            ts = [jnp.where(m, z, t) for z, t in zip(zs, ts)]
        s *= 2
    return [eye for _ in lows] if ts is None else ts


def _gdn_in_kernel(x_ref, nw_ref, w_ref, wba_ref, prev_ref, cw_ref, act_ref, ba_ref, last_ref, xn_scr, tail_scr,
                   *, tm, groups, tiles_per_row):
    i = pl.program_id(0)
    n = pl.program_id(1)

    @pl.when(n == 0)
    def _():
        xb = _rms(x_ref[...], nw_ref[...]).astype(BF16)
        xn_scr[...] = xb
        ba_ref[...] = jnp.dot(xb, wba_ref[...], preferred_element_type=F32)

    @pl.when((n == 0) & (i == 0))
    def _():
        tail_scr[...] = jnp.zeros(tail_scr.shape, F32)

    tn = w_ref.shape[1]
    rows = lax.broadcasted_iota(jnp.int32, (tm, 1), 0)

    def conv_silu(u, cs):
        y = u * cw_ref[CONV_W - 1:CONV_W, cs]
        if groups == 1:
            prev = jnp.where((i % tiles_per_row) == 0, prev_ref[0, :, cs], tail_scr[n, :, cs])
            for s in range(1, CONV_W):
                rolled = pltpu.roll(u, s, axis=0)
                head = jnp.where(rows[:SUBLANES] < s, pltpu.roll(prev, s, axis=0), rolled[:SUBLANES])
                shifted = jnp.concatenate([head, rolled[SUBLANES:]], axis=0)
                y = y + shifted * cw_ref[CONV_W - 1 - s:CONV_W - s, cs]
            tail_scr[n, :, cs] = u[tm - SUBLANES:tm]
            last_ref[0, :, cs] = u[tm - SUBLANES:tm]
        else:
            prev = prev_ref[:, :, cs].reshape(tm, u.shape[1])
            t_in = rows & (SUBLANES - 1)
            for s in range(1, CONV_W):
                shifted = jnp.where(t_in < s, pltpu.roll(prev, tm - SUBLANES + s, axis=0), pltpu.roll(u, s, axis=0))
                y = y + shifted * cw_ref[CONV_W - 1 - s:CONV_W - s, cs]
            last_ref[:, :, cs] = u.reshape(groups, SUBLANES, u.shape[1])
        return _silu(y)

    def run(finish):
        xb = xn_scr[...]
        slab = lambda c: jnp.dot(xb, w_ref[:, c * GDN_SLAB:(c + 1) * GDN_SLAB], preferred_element_type=F32)
        n_slabs = tn // GDN_SLAB
        u_next = slab(0)
        for c in range(n_slabs):
            u = u_next
            if c + 1 < n_slabs:
                u_next = slab(c + 1)
            cs = slice(c * GDN_SLAB, (c + 1) * GDN_SLAB)
            act_ref[:, cs] = finish(u, cs).astype(BF16)

    @pl.when(n < 2)
    def _():
        scale = jnp.where(n == 0, GDN_HEAD ** -0.5, 1.0)

        def finish(u, cs):
            a = conv_silu(u, cs)
            parts = []
            for h in range(GDN_SLAB // GDN_HEAD):
                ah = a[:, h * GDN_HEAD:(h + 1) * GDN_HEAD]
                parts.append(ah * (lax.rsqrt(jnp.sum(ah * ah, axis=-1, keepdims=True) + L2_EPS) * scale))
            return jnp.concatenate(parts, axis=1)

        run(finish)

    @pl.when((n >= 2) & (n < GDN_CONV_TILES))
    def _():
        run(conv_silu)

    @pl.when(n >= GDN_CONV_TILES)
    def _():
        run(lambda u, cs: _silu(u))


def _gdn_in(x2d, nw, w_in, w_ba, prev8, conv_wt, t):
    m = x2d.shape[0]
    b = m // t
    tn = GDN_TN
    conv_tile = lambda j: jnp.minimum(j, GDN_CONV_TILES - 1)
    if t == SUBLANES:
        tm = _pick_tile(m, 512)
        groups, tiles_per_row = tm // SUBLANES, 1
        before = pl.BlockSpec((groups, SUBLANES, tn), lambda i, j: (i, 0, conv_tile(j)))
    else:
        tm = _pick_tile(t, 512)
        groups, tiles_per_row = 1, t // tm
        assert tm % SUBLANES == 0 and tm > SUBLANES
        before = pl.BlockSpec((1, SUBLANES, tn), lambda i, j: (i // tiles_per_row, 0, conv_tile(j)))
    after = pl.BlockSpec((groups, SUBLANES, tn), lambda i, j: (i, 0, conv_tile(j)))
    kern = functools.partial(_gdn_in_kernel, tm=tm, groups=groups, tiles_per_row=tiles_per_row)
    return pl.pallas_call(
        kern,
        grid=(m // tm, GDN_QKVZ_DIM // tn),
        in_specs=[
            pl.BlockSpec((tm, D_MODEL), lambda i, j: (i, 0)),
            pl.BlockSpec((1, D_MODEL), lambda i, j: (0, 0)),
            pl.BlockSpec((D_MODEL, tn), lambda i, j: (0, j)),
            pl.BlockSpec((D_MODEL, LANES), lambda i, j: (0, 0)),
            before,
            pl.BlockSpec((CONV_W, tn), lambda i, j: (0, conv_tile(j))),
        ],
        out_specs=[
            pl.BlockSpec((tm, tn), lambda i, j: (i, j)),
            pl.BlockSpec((tm, LANES), lambda i, j: (i, 0)),
            after,
        ],
        out_shape=[
            jax.ShapeDtypeStruct((m, GDN_QKVZ_DIM), BF16),
            jax.ShapeDtypeStruct((m, LANES), F32),
            jax.ShapeDtypeStruct((m // tm * groups, SUBLANES, GDN_CONV_DIM), F32),
        ],
        scratch_shapes=[pltpu.VMEM((tm, D_MODEL), BF16), pltpu.VMEM((GDN_CONV_TILES, SUBLANES, tn), F32)],
        compiler_params=pltpu.CompilerParams(
            dimension_semantics=("arbitrary", "arbitrary"), vmem_limit_bytes=VMEM_LIMIT_BYTES),
        name="gdn_in",
    )(x2d, nw, w_in, w_ba, prev8, conv_wt)


def _gdn_chunk_kernel(act_ref, ba_ref, s0_ref, alog_ref, dtb_ref, o_ref, sout_ref, s_scr, *, chunk, gb, gt, n_steps):
    step = pl.program_id(1)
    C = chunk
    C2 = 2 * C
    carried = n_steps * gt > 1

    if carried:
        @pl.when(step == 0)
        def _():
            s_scr[...] = s0_ref[...]

    r1, c1 = _iota2(C, C)
    tril = jnp.where(r1 >= c1, 1.0, 0.0).astype(BF16)
    r2, c2 = _iota2(C2, C)
    tril2 = jnp.where((r2 & (C - 1)) >= c2, 1.0, 0.0).astype(BF16)
    row, col = _iota2(C2, C2)
    causal = ((row >= C) == (col >= C)) & (row >= col)
    top = lax.broadcasted_iota(jnp.int32, (C2, 1), 0) < C
    left = lax.broadcasted_iota(jnp.int32, (1, C2), 1) < C

    pairs = []
    for r in range(gb):
        ba = ba_ref[r]
        beta_l = jax.nn.sigmoid(ba)
        g_l = -jnp.exp(alog_ref[...]) * _softplus(ba + dtb_ref[...])
        for j in range(gt):
            rows = slice(j * C, (j + 1) * C)
            beta_all = beta_l[rows]
            gc_all = _dot_exact_lhs(tril, g_l[rows], "nn")
            gct_all = _dot_exact_lhs(tril2, g_l[rows], "tn")
            for qh in range(GDN_QK_HEADS):
                q = act_ref[r, rows, qh * GDN_HEAD:(qh + 1) * GDN_HEAD].astype(F32)
                k = act_ref[r, rows, GDN_KEY_DIM + qh * GDN_HEAD:GDN_KEY_DIM + (qh + 1) * GDN_HEAD].astype(F32)
                h0 = 2 * qh
                h1 = h0 + 1
                a0, a1 = GDN_V_HEADS + h0, GDN_V_HEADS + h1
                beta_s = jnp.concatenate([beta_all[:, h0:h0 + 1], beta_all[:, h1:h1 + 1]], axis=0)
                gc_s = jnp.concatenate([gc_all[:, a0:a0 + 1], gc_all[:, a1:a1 + 1]], axis=0)
                gc_row = jnp.where(left, gct_all[a0:a0 + 1, :], gct_all[a1:a1 + 1, :])
                glast0 = gc_all[C - 1:C, a0:a0 + 1]
                glast1 = gc_all[C - 1:C, a1:a1 + 1]
                glast_s = jnp.where(top, glast0, glast1)
                decay = jnp.exp(jnp.where(causal, gc_s - gc_row, -jnp.inf))
                k_s = jnp.concatenate([k, k], axis=0)
                q_s = jnp.concatenate([q, q], axis=0)
                v0 = 2 * GDN_KEY_DIM + h0 * GDN_HEAD
                v_s = jnp.concatenate([act_ref[r, rows, v0:v0 + GDN_HEAD],
                                       act_ref[r, rows, v0 + GDN_HEAD:v0 + 2 * GDN_HEAD]], axis=0).astype(F32)
                egc = jnp.exp(gc_s)
                kb_s = k_s * beta_s
                pairs.append(dict(
                    r=r, j=j, h0=h0, h1=h1,
                    low=_bdot_nt(kb_s, k_s) * decay,
                    rhs=jnp.concatenate([v_s * beta_s, kb_s * egc], axis=1).astype(BF16),
                    attn=_bdot_nt(q_s, k_s) * decay, qd_s=q_s * egc, kd_s=k_s * jnp.exp(glast_s - gc_s),
                    eg0=jnp.exp(glast0), eg1=jnp.exp(glast1)))

    tinvs = _tri_inv_many([pr["low"] for pr in pairs], C2, C)
    sols = [jnp.dot(t, pr["rhs"], preferred_element_type=F32) for t, pr in zip(tinvs, pairs)]

    def state_in(r, h):
        return s_scr[r, h] if carried else s0_ref[r, h]

    def state_out(r, h, val):
        if carried:
            s_scr[r, h] = val
        else:
            sout_ref[r, h] = val

    for r in range(gb):
        for j in range(gt):
            grp = [(pr, sol) for pr, sol in zip(pairs, sols) if pr["r"] == r and pr["j"] == j]
            rows = slice(j * C, (j + 1) * C)
            sts, wqs = [], []
            for pr, sol in grp:
                w_s = sol[:, GDN_HEAD:]
                st0 = state_in(r, pr["h0"])
                st1 = state_in(r, pr["h1"])
                sts.append((st0, st1))
                wqs.append((_bdot(jnp.concatenate([w_s[:C], pr["qd_s"][:C]], axis=0), st0),
                            _bdot(jnp.concatenate([w_s[C:], pr["qd_s"][C:]], axis=0), st1)))
            for (pr, sol), (st0, st1), (wq0, wq1) in zip(grp, sts, wqs):
                h0, h1 = pr["h0"], pr["h1"]
                u_s = sol[:, :GDN_HEAD]
                vn0 = u_s[:C] - wq0[:C]
                vn1 = u_s[C:] - wq1[:C]
                o_s = _bdot(pr["attn"], jnp.concatenate([vn0, vn1], axis=0))
                o_ref[r, rows, h0 * GDN_HEAD:(h0 + 1) * GDN_HEAD] = (o_s[:C] + wq0[C:]).astype(BF16)
                o_ref[r, rows, h1 * GDN_HEAD:(h1 + 1) * GDN_HEAD] = (o_s[C:] + wq1[C:]).astype(BF16)
                state_out(r, h0, st0 * pr["eg0"] + _bdot_tn(pr["kd_s"][:C], vn0))
                state_out(r, h1, st1 * pr["eg1"] + _bdot_tn(pr["kd_s"][C:], vn1))

    if carried:
        @pl.when(step == n_steps - 1)
        def _():
            sout_ref[...] = s_scr[...]


def _gdn_chunk(act, ba, s0, alog_row, dtb_row, chunk, gb, gt):
    b, t, _ = act.shape
    span = gt * chunk
    n_steps = t // span
    assert b % gb == 0 and t % span == 0 and (gb == 1 or n_steps == 1)
    kern = functools.partial(_gdn_chunk_kernel, chunk=chunk, gb=gb, gt=gt, n_steps=n_steps)
    state = pl.BlockSpec((gb, GDN_V_HEADS, GDN_HEAD, GDN_HEAD), lambda i, c: (i, 0, 0, 0))
    return pl.pallas_call(
        kern,
        grid=(b // gb, n_steps),
        in_specs=[
            pl.BlockSpec((gb, span, GDN_CONV_DIM), lambda i, c: (i, c, 0)),
            pl.BlockSpec((gb, span, LANES), lambda i, c: (i, c, 0)),
            state,
            pl.BlockSpec((1, LANES), lambda i, c: (0, 0)),
            pl.BlockSpec((1, LANES), lambda i, c: (0, 0)),
        ],
        out_specs=[pl.BlockSpec((gb, span, GDN_VALUE_DIM), lambda i, c: (i, c, 0)), state],
        out_shape=[
            jax.ShapeDtypeStruct((b, t, GDN_VALUE_DIM), BF16),
            jax.ShapeDtypeStruct((b, GDN_V_HEADS, GDN_HEAD, GDN_HEAD), F32),
        ],
        scratch_shapes=[pltpu.VMEM((gb if n_steps * gt > 1 else 1, GDN_V_HEADS, GDN_HEAD, GDN_HEAD), F32)],
        compiler_params=pltpu.CompilerParams(
            dimension_semantics=("parallel", "arbitrary"), vmem_limit_bytes=VMEM_LIMIT_BYTES),
        name="gdn_chunk",
    )(act, ba, s0, alog_row, dtb_row)


def _gdn_out_kernel(o_ref, gate_ref, x_ref, gw_ref, w_ref, y_ref):
    gw = gw_ref[...]
    parts = []
    for h in range(GDN_V_HEADS):
        sl = slice(h * GDN_HEAD, (h + 1) * GDN_HEAD)
        parts.append((_rms(o_ref[:, sl].astype(F32), gw) * gate_ref[:, sl].astype(F32)).astype(BF16))
    gated = jnp.concatenate(parts, axis=1)
    y_ref[...] = x_ref[...] + jnp.dot(gated, w_ref[...], preferred_element_type=F32)


def _gdn_out(o2d, act2d, x2d, gw, w_out, tm):
    m = x2d.shape[0]
    z_block = GDN_CONV_DIM // GDN_VALUE_DIM
    return pl.pallas_call(
        _gdn_out_kernel,
        grid=(m // tm,),
        in_specs=[
            pl.BlockSpec((tm, GDN_VALUE_DIM), lambda i: (i, 0)),
            pl.BlockSpec((tm, GDN_VALUE_DIM), lambda i: (i, z_block)),
            pl.BlockSpec((tm, D_MODEL), lambda i: (i, 0)),
            pl.BlockSpec((1, GDN_HEAD), lambda i: (0, 0)),
            pl.BlockSpec((GDN_VALUE_DIM, D_MODEL), lambda i: (0, 0)),
        ],
        out_specs=pl.BlockSpec((tm, D_MODEL), lambda i: (i, 0)),
        out_shape=jax.ShapeDtypeStruct((m, D_MODEL), F32),
        compiler_params=pltpu.CompilerParams(
            dimension_semantics=("parallel",), vmem_limit_bytes=VMEM_LIMIT_BYTES),
        name="gdn_out",
    )(o2d, act2d, x2d, gw, w_out)


def _head_sum(x, ones_bd):
    parts = []
    for p in range(RWKV_PAIRS):
        hi, lo = _split2(x[:, p * LANES:(p + 1) * LANES])
        parts.append(jnp.dot(hi, ones_bd, preferred_element_type=F32) + jnp.dot(lo, ones_bd, preferred_element_type=F32))
    return jnp.concatenate(parts, axis=1)


def _rwkv_in_kernel(x_ref, xp_ref, sh_ref, nw_ref, mu_ref, wr_ref, wk_ref, wv_ref, wz_ref,
                    w0_ref, w1_ref, w2_ref, a0_ref, a1_ref, a2_ref, kk_ref, ka_ref, rk_ref, ones_ref,
                    r_out, k_out, v_out, ld_out, kn_out, b_out, z_out, bonus_out, last_out, *, tb, tt, n_t):
    t = pl.program_id(1)
    m = tb * tt
    nw = nw_ref[...]
    xn = _rms(x_ref[...].reshape(m, D_MODEL), nw)
    prev_last = _rms(xp_ref[:, SUBLANES - 1:SUBLANES, :], nw)
    first = jnp.where(t == 0, sh_ref[...], prev_last)
    first = jnp.broadcast_to(first, (tb, tt, D_MODEL)).reshape(m, D_MODEL)
    rolled = pltpu.roll(xn, 1, axis=0)
    rows = lax.broadcasted_iota(jnp.int32, (m, 1), 0)
    xprev = jnp.where((rows & (tt - 1)) == 0, first, rolled)
    xx = xprev - xn
    mu = mu_ref[...]

    def mix(i):
        return xn + xx * mu[i:i + 1, :]

    r = _bdot(mix(0), wr_ref[...])
    k = _bdot(mix(1), wk_ref[...])
    v = _bdot(mix(2), wv_ref[...])
    z = _bdot(mix(3), wz_ref[...])
    w = w0_ref[...] + _bdot(jnp.tanh(_bdot(mix(4), w1_ref[...])), w2_ref[...])
    w = -_softplus(-w) - 0.5
    a = jax.nn.sigmoid(a0_ref[...] + _bdot(_bdot(mix(5), a1_ref[...]), a2_ref[...]))
    ones_bd = ones_ref[...]
    kk = k * kk_ref[...]
    kn = kk * lax.rsqrt(_head_sum(kk * kk, ones_bd) + L2_EPS)
    k = k * (1.0 + (a - 1.0) * ka_ref[...])
    r_out[...] = r
    k_out[...] = k
    v_out[...] = v
    z_out[...] = z
    ld_out[...] = -jnp.exp(w)
    kn_out[...] = kn
    b_out[...] = kn * a
    bonus_out[...] = _head_sum(r * k * rk_ref[...], ones_bd) * v

    @pl.when(t == n_t - 1)
    def _():
        last_out[...] = _rms(x_ref[:, tt - 1:tt, :], nw)


def _rwkv_in(x3d, shift, nw, mu, wr, wk, wv, wz, w0, w1, w2, a0, a1, a2, k_k, k_a, r_k, ones_bd, tb, tt):
    b, t, _ = x3d.shape
    n_t = t // tt
    tpb = tt // SUBLANES
    kern = functools.partial(_rwkv_in_kernel, tb=tb, tt=tt, n_t=n_t)
    row = lambda n: pl.BlockSpec((n, D_MODEL), lambda i, j: (0, 0))
    big = pl.BlockSpec((D_MODEL, D_MODEL), lambda i, j: (0, 0))
    tok = pl.BlockSpec((tb * tt, D_MODEL), lambda i, j: (i * n_t + j, 0))
    m = b * t
    outs = pl.pallas_call(
        kern,
        grid=(b // tb, n_t),
        in_specs=[
            pl.BlockSpec((tb, tt, D_MODEL), lambda i, j: (i, j, 0)),
            pl.BlockSpec((tb, SUBLANES, D_MODEL), lambda i, j: (i, jnp.maximum(j * tpb - 1, 0), 0)),
            pl.BlockSpec((tb, 1, D_MODEL), lambda i, j: (i, 0, 0)),
            row(1), row(6), big, big, big, big,
            row(1), pl.BlockSpec((D_MODEL, LORA), lambda i, j: (0, 0)), pl.BlockSpec((LORA, D_MODEL), lambda i, j: (0, 0)),
            row(1), pl.BlockSpec((D_MODEL, LORA), lambda i, j: (0, 0)), pl.BlockSpec((LORA, D_MODEL), lambda i, j: (0, 0)),
            row(1), row(1), row(1),
            pl.BlockSpec((LANES, LANES), lambda i, j: (0, 0)),
        ],
        out_specs=[tok] * 8 + [pl.BlockSpec((tb, 1, D_MODEL), lambda i, j: (i, 0, 0))],
        out_shape=[jax.ShapeDtypeStruct((m, D_MODEL), F32)] * 8 + [jax.ShapeDtypeStruct((b, 1, D_MODEL), F32)],
        compiler_params=pltpu.CompilerParams(
            dimension_semantics=("parallel", "arbitrary"), vmem_limit_bytes=VMEM_LIMIT_BYTES),
        name="rwkv_in",
    )(x3d, x3d, shift, nw, mu, wr, wk, wv, wz, w0, w1, w2, a0, a1, a2, k_k, k_a, r_k, ones_bd)
    return outs


def _rwkv_rec_kernel(r_ref, k_ref, v_ref, ld_ref, kn_ref, b_ref, s0_ref, y_ref, sout_ref, s_scr,
                     *, chunk, gb, gt, n_steps):
    step = pl.program_id(1)
    C = chunk
    C2 = 2 * C
    carried = n_steps * gt > 1

    def pair_state(rr, p):
        z = jnp.zeros((RWKV_HEAD, RWKV_HEAD), F32)
        return jnp.concatenate([jnp.concatenate([s0_ref[rr, 2 * p], z], axis=1),
                                jnp.concatenate([z, s0_ref[rr, 2 * p + 1]], axis=1)], axis=0)

    def put_state(rr, p, val):
        sout_ref[rr, 2 * p] = val[:RWKV_HEAD, :RWKV_HEAD]
        sout_ref[rr, 2 * p + 1] = val[RWKV_HEAD:, RWKV_HEAD:]

    if carried:
        @pl.when(step == 0)
        def _():
            for rr in range(gb):
                for p in range(RWKV_PAIRS):
                    s_scr[rr, p] = pair_state(rr, p)

    r1, c1 = _iota2(C, C)
    tril = jnp.where(r1 >= c1, 1.0, 0.0).astype(BF16)
    lane = lax.broadcasted_iota(jnp.int32, (1, LANES), 1)
    m_lo = lane < RWKV_HEAD
    rc, cc = _iota2(C, C2)
    tl = cc & (C - 1)
    strict_pair = rc > tl
    incl_pair = rc >= tl

    def stack(x):
        return jnp.concatenate([jnp.where(m_lo, x, 0.0), jnp.where(m_lo, 0.0, x)], axis=0)

    pairs = []
    for rr in range(gb):
        for j in range(gt):
            rows = slice(j * C, (j + 1) * C)
            for p in range(RWKV_PAIRS):
                sl = slice(p * LANES, (p + 1) * LANES)
                pairs.append(dict(rr=rr, j=j, p=p, rows=rows, sl=sl))
    for pr in pairs:
        pr["gc"] = _dot_exact_lhs(tril, ld_ref[pr["rr"], pr["rows"], pr["sl"]], "nn")
    for pr in pairs:
        at_ref = (pr["rr"], pr["rows"], pr["sl"])
        gc = pr["gc"]
        k = k_ref[at_ref]
        b = b_ref[at_ref]
        glast = gc[C - 1:C, :]
        e_neg = jnp.exp(-gc)
        e_end = jnp.exp(glast - gc)
        rt = r_ref[at_ref] * jnp.exp(gc)
        at_s = stack(-kn_ref[at_ref] * jnp.exp(gc - ld_ref[at_ref]))
        bt_s = stack(b * e_neg)
        pr.update(rt=rt, at_s=at_s, v_s=stack(v_ref[at_ref]).astype(BF16),
                  kh_s=stack(k * e_end), bh_s=stack(b * e_end).astype(BF16), eg=jnp.exp(glast))
        ars = jnp.concatenate([at_s, rt], axis=0)
        g = _bdot_nt(ars, jnp.concatenate([stack(k * e_neg), bt_s], axis=0))
        pr["low"] = -g[:C2, C2:]
        pr["rb"] = jnp.where(incl_pair, g[C2:, C2:], 0.0).astype(BF16)
        pr["rk"] = jnp.where(incl_pair, g[C2:, :C2], 0.0).astype(BF16)
        pr["ak"] = jnp.where(strict_pair, g[:C, :C2] + g[C:C2, :C2], 0.0)
    for pr in pairs:
        akv = jnp.dot(pr["ak"].astype(BF16), pr["v_s"], preferred_element_type=F32)
        pr["rhs"] = jnp.concatenate([pr["at_s"], stack(akv)], axis=1).astype(BF16)

    tinvs = _tri_inv_many([pr["low"] for pr in pairs], C2, C)
    sols = [jnp.dot(t, pr["rhs"], preferred_element_type=F32).astype(BF16) for t, pr in zip(tinvs, pairs)]
    for pr, sol in zip(pairs, sols):
        a2_s = sol[:, :LANES]
        u2_s = sol[:, LANES:]
        d = functools.partial(jnp.dot, preferred_element_type=F32)
        pr["r2"] = pr["rt"] + d(pr["rb"], a2_s)
        pr["y2"] = d(pr["rk"], pr["v_s"]) + d(pr["rb"], u2_s)
        pr["w2"] = _bdot_tn(a2_s, pr["bh_s"])
        pr["n2"] = _bdot_tn(jnp.concatenate([pr["v_s"], u2_s], axis=0),
                            jnp.concatenate([pr["kh_s"].astype(BF16), pr["bh_s"]], axis=0))

    for pr in pairs:
        rr, p = pr["rr"], pr["p"]
        st = s_scr[rr, p] if carried else pair_state(rr, p)
        y_ref[rr, pr["rows"], pr["sl"]] = _bdot_nt(pr["r2"], st) + pr["y2"]
        new = st * pr["eg"] + (_bdot(st, pr["w2"]) + pr["n2"])
        if carried:
            s_scr[rr, p] = new
        else:
            put_state(rr, p, new)

    if carried:
        @pl.when(step == n_steps - 1)
        def _():
            for rr in range(gb):
                for p in range(RWKV_PAIRS):
                    put_state(rr, p, s_scr[rr, p])


def _rwkv_rec(r, k, v, ld, kn, bvec, s0, chunk, gb, gt):
    b, t, _ = r.shape
    span = gt * chunk
    n_steps = t // span
    assert b % gb == 0 and t % span == 0 and (gb == 1 or n_steps == 1)
    kern = functools.partial(_rwkv_rec_kernel, chunk=chunk, gb=gb, gt=gt, n_steps=n_steps)
    tok = pl.BlockSpec((gb, span, D_MODEL), lambda i, c: (i, c, 0))
    st = pl.BlockSpec((gb, RWKV_HEADS, RWKV_HEAD, RWKV_HEAD), lambda i, c: (i, 0, 0, 0))
    return pl.pallas_call(
        kern,
        grid=(b // gb, n_steps),
        in_specs=[tok] * 6 + [st],
        out_specs=[tok, st],
        out_shape=[
            jax.ShapeDtypeStruct((b, t, D_MODEL), F32),
            jax.ShapeDtypeStruct((b, RWKV_HEADS, RWKV_HEAD, RWKV_HEAD), F32),
        ],
        scratch_shapes=[pltpu.VMEM((gb if n_steps * gt > 1 else 1, RWKV_PAIRS, LANES, LANES), F32)],
        compiler_params=pltpu.CompilerParams(
            dimension_semantics=("parallel", "arbitrary"), vmem_limit_bytes=VMEM_LIMIT_BYTES),
        name="rwkv_rec",
    )(r, k, v, ld, kn, bvec, s0)


def _rwkv_out_kernel(y_ref, z_ref, bonus_ref, x_ref, lw_ref, lb_ref, avg_ref, w_ref, fw_ref, out_ref):
    y = y_ref[...]
    avg = avg_ref[...]
    mean = _head_sum(y, avg)
    yc = y - mean
    var = _head_sum(yc * yc, avg)
    gn = yc * lax.rsqrt(var + GN_EPS) * lw_ref[...] + lb_ref[...]
    o = (gn + bonus_ref[...]) * _silu(z_ref[...])
    x2 = x_ref[...] + _bdot(o, w_ref[...])
    out_ref[...] = _rms(x2, fw_ref[...])


def _rwkv_out(y2d, z2d, bonus2d, x2d, lnx_w, lnx_b, avg_bd, w_o, fw, tm):
    m = x2d.shape[0]
    tok = pl.BlockSpec((tm, D_MODEL), lambda i: (i, 0))
    row = pl.BlockSpec((1, D_MODEL), lambda i: (0, 0))
    return pl.pallas_call(
        _rwkv_out_kernel,
        grid=(m // tm,),
        in_specs=[tok, tok, tok, tok, row, row,
                  pl.BlockSpec((LANES, LANES), lambda i: (0, 0)),
                  pl.BlockSpec((D_MODEL, D_MODEL), lambda i: (0, 0)), row],
        out_specs=tok,
        out_shape=jax.ShapeDtypeStruct((m, D_MODEL), F32),
        compiler_params=pltpu.CompilerParams(
            dimension_semantics=("parallel",), vmem_limit_bytes=VMEM_LIMIT_BYTES),
        name="rwkv_out",
    )(y2d, z2d, bonus2d, x2d, lnx_w, lnx_b, avg_bd, w_o, fw)


def _pick_tile(m, cap):
    t = min(m, cap)
    while m % t:
        t //= 2
    return t


def _step_groups(b, t, chunk):
    n_chunks = t // chunk
    if n_chunks > 1:
        return 1, _pick_tile(n_chunks, CHUNKS_PER_STEP)
    return _pick_tile(b, ROWS_PER_STEP), 1


def _trunk(x, gdn_s, gdn_conv, rwkv_s, rwkv_shift, p, gdn_chunk, rwkv_chunk):
    b, t, _ = x.shape
    m = b * t
    x2d = x.reshape(m, D_MODEL)
    prev8 = jnp.pad(gdn_conv, ((0, 0), (SUBLANES - (CONV_W - 1), 0), (0, 0)))
    act, ba, last8 = _gdn_in(x2d, p["nw0"], p["w_in"], p["w_ba"], prev8, p["conv_wt"], t)
    conv_new = last8.reshape(b, -1, SUBLANES, GDN_CONV_DIM)[:, -1, SUBLANES - (CONV_W - 1):, :]
    o, s_gdn = _gdn_chunk(act.reshape(b, t, GDN_QKVZ_DIM), ba.reshape(b, t, LANES), gdn_s,
                          p["alog_row"], p["dtb_row"], gdn_chunk, *_step_groups(b, t, gdn_chunk))
    tm2 = _pick_tile(m, 256)
    x1 = _gdn_out(o.reshape(m, GDN_VALUE_DIM), act, x2d, p["gn_w"], p["w_out"], tm2)
    if t >= 256:
        tb, tt = 1, 256
    elif t * b <= 256:
        tb, tt = b, t
    else:
        tb, tt = 256 // t, t
    r, k, v, ld, kn, bvec, z, bonus, last = _rwkv_in(
        x1.reshape(b, t, D_MODEL), rwkv_shift.reshape(b, 1, D_MODEL), p["nw1"], p["mu"],
        p["wr"], p["wk"], p["wv"], p["wz"], p["w0"], p["w1"], p["w2"], p["a0"], p["a1"], p["a2"],
        p["k_k"], p["k_a"], p["r_k"], p["ones_bd"], tb, tt)
    sh3 = lambda a: a.reshape(b, t, D_MODEL)
    y, s_rwkv = _rwkv_rec(sh3(r), sh3(k), sh3(v), sh3(ld), sh3(kn), sh3(bvec), rwkv_s,
                          rwkv_chunk, *_step_groups(b, t, rwkv_chunk))
    out = _rwkv_out(y.reshape(m, D_MODEL), z, bonus, x1, p["lnx_w"], p["lnx_b"], p["avg_bd"], p["w_o"], p["fw"], tm2)
    return (out.reshape(b, t, D_MODEL), s_gdn, conv_new, s_rwkv, last.reshape(b, D_MODEL))


def kernel(x_prompt, x_sample, state_gdn, state_gdn_conv, state_rwkv, state_rwkv_shift, meta_tokens, norm_w, final_norm_w, gdn_w_in, gdn_conv_w, gdn_a_log, gdn_dt_bias, gdn_norm_w, gdn_w_out, rwkv_mu, rwkv_w_rkvz, rwkv_w0, rwkv_w1, rwkv_w2, rwkv_a0, rwkv_a1, rwkv_a2, rwkv_k_k, rwkv_k_a, rwkv_r_k, rwkv_lnx_w, rwkv_lnx_b, rwkv_w_o):
    assert norm_w.shape[0] == 2 and gdn_w_in.shape[0] == 1 and rwkv_mu.shape[0] == 1
    row = lambda a: a.reshape(1, -1).astype(F32)
    w_in = gdn_w_in[0]
    w_ba = jnp.pad(w_in[:, GDN_QKVZ_DIM:], ((0, 0), (0, LANES - 2 * GDN_V_HEADS)))
    gate_row = lambda a: jnp.pad(a.astype(F32), (GDN_V_HEADS, LANES - 2 * GDN_V_HEADS)).reshape(1, LANES)
    rb, cb = jnp.arange(LANES)[:, None] // RWKV_HEAD, jnp.arange(LANES)[None, :] // RWKV_HEAD
    bd = (rb == cb)
    p = {
        "nw0": row(norm_w[0]), "nw1": row(norm_w[1]), "fw": row(final_norm_w),
        "w_in": w_in.astype(BF16), "w_ba": w_ba.astype(BF16),
        "conv_wt": gdn_conv_w[0].T.astype(F32),
        "alog_row": gate_row(gdn_a_log[0]), "dtb_row": gate_row(gdn_dt_bias[0]),
        "gn_w": row(gdn_norm_w[0]), "w_out": gdn_w_out[0].astype(BF16),
        "mu": rwkv_mu[0].astype(F32),
        "wr": rwkv_w_rkvz[0, 0].astype(BF16), "wk": rwkv_w_rkvz[0, 1].astype(BF16),
        "wv": rwkv_w_rkvz[0, 2].astype(BF16), "wz": rwkv_w_rkvz[0, 3].astype(BF16),
        "w0": row(rwkv_w0[0]), "w1": rwkv_w1[0].astype(BF16), "w2": rwkv_w2[0].astype(BF16),
        "a0": row(rwkv_a0[0]), "a1": rwkv_a1[0].astype(BF16), "a2": rwkv_a2[0].astype(BF16),
        "k_k": row(rwkv_k_k[0]), "k_a": row(rwkv_k_a[0]), "r_k": row(rwkv_r_k[0]),
        "lnx_w": row(rwkv_lnx_w[0]), "lnx_b": row(rwkv_lnx_b[0]), "w_o": rwkv_w_o[0].astype(BF16),
        "ones_bd": bd.astype(BF16), "avg_bd": (bd.astype(F32) / RWKV_HEAD).astype(BF16),
    }
    bp = x_prompt.shape[0]
    zeros = lambda *s: jnp.zeros(s, F32)
    _, m_gdn, m_conv, m_rwkv, m_shift = _trunk(
        meta_tokens.astype(F32)[None], zeros(1, GDN_V_HEADS, GDN_HEAD, GDN_HEAD), zeros(1, CONV_W - 1, GDN_CONV_DIM),
        zeros(1, RWKV_HEADS, RWKV_HEAD, RWKV_HEAD), zeros(1, D_MODEL), p, N_META, N_META)
    rep = lambda a: jnp.broadcast_to(a, (bp,) + a.shape[1:])
    y_p, p_gdn, p_conv, p_rwkv, p_shift = _trunk(
        x_prompt, rep(m_gdn), rep(m_conv), rep(m_rwkv), rep(m_shift), p, GDN_CHUNK, RWKV_CHUNK)
    ts = x_sample.shape[1]
    y_s, s_gdn, s_conv, s_rwkv, s_shift = _trunk(
        x_sample, state_gdn[0], state_gdn_conv[0], state_rwkv[0], state_rwkv_shift[0], p, ts, ts)
    return (y_p, y_s, p_gdn[None], p_conv[None], p_rwkv[None], p_shift[None],
            s_gdn[None], s_conv[None], s_rwkv[None], s_shift[None])
```

```python
import functools

import jax
import jax.numpy as jnp
from jax import lax
from jax.experimental import pallas as pl
from jax.experimental.pallas import tpu as pltpu

F32 = jnp.float32
BF16 = jnp.bfloat16

D_MODEL = 1024
N_META = 16
GDN_QK_HEADS = 8
GDN_V_HEADS = 16
GDN_HEAD = 128
GDN_KEY_DIM = GDN_QK_HEADS * GDN_HEAD
GDN_VALUE_DIM = GDN_V_HEADS * GDN_HEAD
GDN_CONV_DIM = 2 * GDN_KEY_DIM + GDN_VALUE_DIM
GDN_QKVZ_DIM = GDN_CONV_DIM + GDN_VALUE_DIM
CONV_W = 4
GDN_TN = 1024
GDN_CONV_TILES = GDN_CONV_DIM // GDN_TN
GDN_SLAB = 256
GDN_ROWS = 64
GDN_CHUNK = 64
RWKV_HEAD = 64
RWKV_HEADS = D_MODEL // RWKV_HEAD
RWKV_PAIRS = RWKV_HEADS // 2
RWKV_CHUNK = 64
LORA = 64
RMS_EPS = 1e-6
L2_EPS = 1e-6
GN_EPS = 64e-5
DECAY_SCALE = 0.6065306597126334
LANES = 128
SUBLANES = 8
BF16_ROWS = 16
VMEM_LIMIT_BYTES = 56 * 1024 * 1024
GDN_CHUNKS_PER_STEP = 4
RWKV_CHUNKS_PER_STEP = 2
ROWS_PER_STEP = 8


def _bdot(a, b):
    return jnp.dot(a.astype(BF16), b.astype(BF16), preferred_element_type=F32)


def _bdot_nt(a, b):
    return lax.dot_general(a.astype(BF16), b.astype(BF16), (((1,), (1,)), ((), ())), preferred_element_type=F32)


def _bdot_tn(a, b):
    return lax.dot_general(a.astype(BF16), b.astype(BF16), (((0,), (0,)), ((), ())), preferred_element_type=F32)


def _split3(a):
    hi = a.astype(BF16).astype(F32)
    r1 = a - hi
    mid = r1.astype(BF16).astype(F32)
    return jnp.concatenate([hi, mid, r1 - mid], axis=0).astype(BF16)


def _tril3(n_out, c):
    r, k = _iota2(n_out, 3 * c)
    k = jnp.where(k >= 2 * c, k - 2 * c, jnp.where(k >= c, k - c, k))
    return jnp.where((r & (c - 1)) >= k, 1.0, 0.0).astype(BF16)


def _cumsum_rows(m3, x, dims):
    x3 = _split3(x)
    if dims == "nn":
        return jnp.dot(m3, x3, preferred_element_type=F32)
    return lax.dot_general(x3, m3, (((0,), (1,)), ((), ())), preferred_element_type=F32)


def _rms(x, w):
    return x * lax.rsqrt(jnp.mean(x * x, axis=-1, keepdims=True) + RMS_EPS) * w


def _sigmoid(x):
    return 1.0 / (1.0 + jnp.exp(-x))


def _silu(x):
    return x * _sigmoid(x)


def _softplus(x):
    return jnp.maximum(x, 0.0) + jnp.log1p(jnp.exp(-jnp.abs(x)))


def _iota2(n, m):
    return lax.broadcasted_iota(jnp.int32, (n, m), 0), lax.broadcasted_iota(jnp.int32, (n, m), 1)


def _log2(n):
    l = n.bit_length() - 1
    assert (1 << l) == n, n
    return l


def _tri_inv_many(lows, n, blk, half_rows):
    row, col = _iota2(n, n)
    eye = jnp.where(row == col, 1.0, 0.0).astype(BF16)
    negs = [(-low).astype(BF16) for low in lows]
    d = lambda a, b: jnp.dot(a, b, preferred_element_type=F32).astype(BF16)
    s = 1
    ts = None
    while s < blk:
        sh = _log2(s)
        if half_rows and s >= BF16_ROWS:
            hr, hc = _iota2(n // 2, n)
            row_blk = ((hr >> sh) << 1) + 1
            m = (row_blk >> 1 == (hc >> sh) >> 1) & (((hc >> sh) & 1) == 0)
            lower = lambda t: jnp.concatenate([t[(2 * j + 1) * s:(2 * j + 2) * s] for j in range(n // (2 * s))], axis=0)
            tl = [lower(t) for t in ts]
            xs = [d(t, ng) for t, ng in zip(tl, negs)]
            zs = [d(x, t) for x, t in zip(xs, ts)]
            new = [jnp.where(m, z, t) for z, t in zip(zs, tl)]
            ts = [jnp.concatenate([piece for j in range(n // (2 * s))
                                   for piece in (t[2 * j * s:(2 * j + 1) * s], nw[j * s:(j + 1) * s])], axis=0)
                  for t, nw in zip(ts, new)]
        else:
            sub_r = row >> sh
            sub_c = col >> sh
            m = ((sub_r >> 1) == (sub_c >> 1)) & ((sub_r & 1) == 1) & ((sub_c & 1) == 0)
            if ts is None:
                ts = [jnp.where(m, ng, eye) for ng in negs]
            else:
                xs = [d(t, ng) for t, ng in zip(ts, negs)]
                zs = [d(x, t) for x, t in zip(xs, ts)]
                ts = [jnp.where(m, z, t) for z, t in zip(zs, ts)]
        s *= 2
    return [eye for _ in lows] if ts is None else ts


def _gdn_in_kernel(x_ref, nw_ref, w_ref, wba_ref, prev_ref, cw_ref, act_ref, ba_ref, last_ref, xn_scr, tail_scr, u_scr,
                   *, tm, groups, tiles_per_row):
    i = pl.program_id(0)
    n = pl.program_id(1)

    @pl.when(n == 0)
    def _():
        xb = _rms(x_ref[...], nw_ref[...]).astype(BF16)
        xn_scr[...] = xb
        ba_ref[...] = jnp.dot(xb, wba_ref[...], preferred_element_type=F32)

    @pl.when((n == 0) & (i == 0))
    def _():
        tail_scr[...] = jnp.zeros(tail_scr.shape, F32)

    tn = w_ref.shape[1]
    n_slabs = tn // GDN_SLAB
    col = lambda c: slice(c * GDN_SLAB, (c + 1) * GDN_SLAB)

    def slab(c):
        return jnp.dot(xn_scr[...], w_ref[:, col(c)], preferred_element_type=F32)

    def l2n(a, scale):
        parts = []
        for h in range(GDN_SLAB // GDN_HEAD):
            ah = a[:, h * GDN_HEAD:(h + 1) * GDN_HEAD]
            parts.append(ah * (lax.rsqrt(jnp.sum(ah * ah, axis=-1, keepdims=True) + L2_EPS) * scale))
        return jnp.concatenate(parts, axis=1)

    def run_conv(scale):
        if groups == 1:
            def fill(c):
                u_scr[c % 2, SUBLANES:SUBLANES + tm, :] = slab(c)
            fill(0)
            for c in range(n_slabs):
                cs, buf = col(c), c % 2
                if c + 1 < n_slabs:
                    fill(c + 1)
                u_scr[buf, 0:SUBLANES, :] = jnp.where((i % tiles_per_row) == 0, prev_ref[0, :, cs], tail_scr[n, :, cs])
                end = u_scr[buf, tm:tm + SUBLANES, :]
                tail_scr[n, :, cs] = end
                last_ref[0, :, cs] = end
                nr = min(GDN_ROWS, tm)
                for rb in range(tm // nr):
                    r0 = SUBLANES + rb * nr
                    y = u_scr[buf, r0:r0 + nr, :] * cw_ref[CONV_W - 1:CONV_W, cs]
                    for s in range(1, CONV_W):
                        y = y + u_scr[buf, r0 - s:r0 - s + nr, :] * cw_ref[CONV_W - 1 - s:CONV_W - s, cs]
                    a = _silu(y)
                    if scale is not None:
                        a = l2n(a, scale)
                    act_ref[rb * nr:(rb + 1) * nr, cs] = a.astype(BF16)
        else:
            rows = lax.broadcasted_iota(jnp.int32, (tm, 1), 0)
            t_in = rows & (SUBLANES - 1)
            u_next = slab(0)
            for c in range(n_slabs):
                cs, u = col(c), u_next
                if c + 1 < n_slabs:
                    u_next = slab(c + 1)
                prev = prev_ref[:, :, cs].reshape(tm, GDN_SLAB)
                y = u * cw_ref[CONV_W - 1:CONV_W, cs]
                for s in range(1, CONV_W):
                    shifted = jnp.where(t_in < s, pltpu.roll(prev, tm - SUBLANES + s, axis=0), pltpu.roll(u, s, axis=0))
                    y = y + shifted * cw_ref[CONV_W - 1 - s:CONV_W - s, cs]
                last_ref[:, :, cs] = u.reshape(groups, SUBLANES, GDN_SLAB)
                a = _silu(y)
                if scale is not None:
                    a = l2n(a, scale)
                act_ref[:, cs] = a.astype(BF16)

    @pl.when(n < 2)
    def _():
        run_conv(jnp.where(n == 0, GDN_HEAD ** -0.5, 1.0))

    @pl.when((n >= 2) & (n < GDN_CONV_TILES))
    def _():
        run_conv(None)

    @pl.when(n >= GDN_CONV_TILES)
    def _():
        for c in range(n_slabs):
            act_ref[:, col(c)] = _silu(slab(c)).astype(BF16)


def _gdn_in(x2d, nw, w_in, w_ba, prev8, conv_wt, t):
    m = x2d.shape[0]
    b = m // t
    tn = GDN_TN
    conv_tile = lambda j: jnp.minimum(j, GDN_CONV_TILES - 1)
    if t == SUBLANES:
        tm = _pick_tile(m, 512)
        groups, tiles_per_row = tm // SUBLANES, 1
        before = pl.BlockSpec((groups, SUBLANES, tn), lambda i, j: (i, 0, conv_tile(j)))
    else:
        tm = _pick_tile(t, 512)
        groups, tiles_per_row = 1, t // tm
        assert tm % SUBLANES == 0 and tm > SUBLANES
        before = pl.BlockSpec((1, SUBLANES, tn), lambda i, j: (i // tiles_per_row, 0, conv_tile(j)))
    after = pl.BlockSpec((groups, SUBLANES, tn), lambda i, j: (i, 0, conv_tile(j)))
    kern = functools.partial(_gdn_in_kernel, tm=tm, groups=groups, tiles_per_row=tiles_per_row)
    return pl.pallas_call(
        kern,
        grid=(m // tm, GDN_QKVZ_DIM // tn),
        in_specs=[
            pl.BlockSpec((tm, D_MODEL), lambda i, j: (i, 0)),
            pl.BlockSpec((1, D_MODEL), lambda i, j: (0, 0)),
            pl.BlockSpec((D_MODEL, tn), lambda i, j: (0, j)),
            pl.BlockSpec((D_MODEL, LANES), lambda i, j: (0, 0)),
            before,
            pl.BlockSpec((CONV_W, tn), lambda i, j: (0, conv_tile(j))),
        ],
        out_specs=[
            pl.BlockSpec((tm, tn), lambda i, j: (i, j)),
            pl.BlockSpec((tm, LANES), lambda i, j: (i, 0)),
            after,
        ],
        out_shape=[
            jax.ShapeDtypeStruct((m, GDN_QKVZ_DIM), BF16),
            jax.ShapeDtypeStruct((m, LANES), F32),
            jax.ShapeDtypeStruct((m // tm * groups, SUBLANES, GDN_CONV_DIM), F32),
        ],
        scratch_shapes=[pltpu.VMEM((tm, D_MODEL), BF16), pltpu.VMEM((GDN_CONV_TILES, SUBLANES, tn), F32),
                        pltpu.VMEM((2, tm + SUBLANES, GDN_SLAB), F32)],
        compiler_params=pltpu.CompilerParams(
            dimension_semantics=("arbitrary", "arbitrary"), vmem_limit_bytes=VMEM_LIMIT_BYTES),
        name="gdn_in",
    )(x2d, nw, w_in, w_ba, prev8, conv_wt)


def _gdn_chunk_kernel(act_ref, ba_ref, s0_ref, alog_ref, dtb_ref, o_ref, sout_ref, s_scr, *, chunk, gb, gt, n_steps):
    step = pl.program_id(1)
    C = chunk
    C2 = 2 * C
    carried = n_steps * gt > 1

    if carried:
        @pl.when(step == 0)
        def _():
            s_scr[...] = s0_ref[...]

    tril = _tril3(C, C)
    tril2 = _tril3(C2, C)
    row, col = _iota2(C2, C2)
    causal = ((row >= C) == (col >= C)) & (row >= col)
    top = lax.broadcasted_iota(jnp.int32, (C2, 1), 0) < C
    left = lax.broadcasted_iota(jnp.int32, (1, C2), 1) < C

    pairs = []
    for r in range(gb):
        ba = ba_ref[r]
        beta_l = _sigmoid(ba)
        g_l = -jnp.exp(alog_ref[...]) * _softplus(ba + dtb_ref[...])
        for j in range(gt):
            rows = slice(j * C, (j + 1) * C)
            beta_all = beta_l[rows]
            gc_all = _cumsum_rows(tril, g_l[rows], "nn")
            gct_all = _cumsum_rows(tril2, g_l[rows], "tn")
            for qh in range(GDN_QK_HEADS):
                q = act_ref[r, rows, qh * GDN_HEAD:(qh + 1) * GDN_HEAD].astype(F32)
                k = act_ref[r, rows, GDN_KEY_DIM + qh * GDN_HEAD:GDN_KEY_DIM + (qh + 1) * GDN_HEAD].astype(F32)
                h0 = 2 * qh
                h1 = h0 + 1
                a0, a1 = GDN_V_HEADS + h0, GDN_V_HEADS + h1
                beta_s = jnp.concatenate([beta_all[:, h0:h0 + 1], beta_all[:, h1:h1 + 1]], axis=0)
                gc_s = jnp.concatenate([gc_all[:, a0:a0 + 1], gc_all[:, a1:a1 + 1]], axis=0)
                gc_row = jnp.where(left, gct_all[a0:a0 + 1, :], gct_all[a1:a1 + 1, :])
                glast0 = gc_all[C - 1:C, a0:a0 + 1]
                glast1 = gc_all[C - 1:C, a1:a1 + 1]
                glast_s = jnp.where(top, glast0, glast1)
                decay = jnp.exp(jnp.where(causal, gc_s - gc_row, -jnp.inf))
                k_s = jnp.concatenate([k, k], axis=0)
                q_s = jnp.concatenate([q, q], axis=0)
                v0 = 2 * GDN_KEY_DIM + h0 * GDN_HEAD
                v_s = jnp.concatenate([act_ref[r, rows, v0:v0 + GDN_HEAD],
                                       act_ref[r, rows, v0 + GDN_HEAD:v0 + 2 * GDN_HEAD]], axis=0).astype(F32)
                egc = jnp.exp(gc_s)
                kb_s = k_s * beta_s
                pairs.append(dict(
                    r=r, j=j, h0=h0, h1=h1,
                    low=_bdot_nt(kb_s, k_s) * decay,
                    rhs=jnp.concatenate([v_s * beta_s, kb_s * egc], axis=1).astype(BF16),
                    attn=_bdot_nt(q_s, k_s) * decay, qd_s=q_s * egc, kd_s=k_s * jnp.exp(glast_s - gc_s),
                    eg0=jnp.exp(glast0), eg1=jnp.exp(glast1)))

    tinvs = _tri_inv_many([pr["low"] for pr in pairs], C2, C, half_rows=True)
    sols = [jnp.dot(t, pr["rhs"], preferred_element_type=F32) for t, pr in zip(tinvs, pairs)]

    def state_in(r, h):
        return s_scr[r, h] if carried else s0_ref[r, h]

    def state_out(r, h, val):
        if carried:
            s_scr[r, h] = val
        else:
            sout_ref[r, h] = val

    for r in range(gb):
        for j in range(gt):
            grp = [(pr, sol) for pr, sol in zip(pairs, sols) if pr["r"] == r and pr["j"] == j]
            rows = slice(j * C, (j + 1) * C)
            sts, wqs = [], []
            for pr, sol in grp:
                w_s = sol[:, GDN_HEAD:]
                st0 = state_in(r, pr["h0"])
                st1 = state_in(r, pr["h1"])
                sts.append((st0, st1))
                wqs.append((_bdot(jnp.concatenate([w_s[:C], pr["qd_s"][:C]], axis=0), st0),
                            _bdot(jnp.concatenate([w_s[C:], pr["qd_s"][C:]], axis=0), st1)))
            for (pr, sol), (st0, st1), (wq0, wq1) in zip(grp, sts, wqs):
                h0, h1 = pr["h0"], pr["h1"]
                u_s = sol[:, :GDN_HEAD]
                vn0 = u_s[:C] - wq0[:C]
                vn1 = u_s[C:] - wq1[:C]
                o_s = _bdot(pr["attn"], jnp.concatenate([vn0, vn1], axis=0))
                o_ref[r, rows, h0 * GDN_HEAD:(h0 + 1) * GDN_HEAD] = (o_s[:C] + wq0[C:]).astype(BF16)
                o_ref[r, rows, h1 * GDN_HEAD:(h1 + 1) * GDN_HEAD] = (o_s[C:] + wq1[C:]).astype(BF16)
                state_out(r, h0, st0 * pr["eg0"] + _bdot_tn(pr["kd_s"][:C], vn0))
                state_out(r, h1, st1 * pr["eg1"] + _bdot_tn(pr["kd_s"][C:], vn1))

    if carried:
        @pl.when(step == n_steps - 1)
        def _():
            sout_ref[...] = s_scr[...]


def _gdn_chunk(act, ba, s0, alog_row, dtb_row, chunk, gb, gt):
    b, t, _ = act.shape
    span = gt * chunk
    n_steps = t // span
    assert b % gb == 0 and t % span == 0 and (gb == 1 or n_steps == 1)
    kern = functools.partial(_gdn_chunk_kernel, chunk=chunk, gb=gb, gt=gt, n_steps=n_steps)
    state = pl.BlockSpec((gb, GDN_V_HEADS, GDN_HEAD, GDN_HEAD), lambda i, c: (i, 0, 0, 0))
    return pl.pallas_call(
        kern,
        grid=(b // gb, n_steps),
        in_specs=[
            pl.BlockSpec((gb, span, GDN_CONV_DIM), lambda i, c: (i, c, 0)),
            pl.BlockSpec((gb, span, LANES), lambda i, c: (i, c, 0)),
            state,
            pl.BlockSpec((1, LANES), lambda i, c: (0, 0)),
            pl.BlockSpec((1, LANES), lambda i, c: (0, 0)),
        ],
        out_specs=[pl.BlockSpec((gb, span, GDN_VALUE_DIM), lambda i, c: (i, c, 0)), state],
        out_shape=[
            jax.ShapeDtypeStruct((b, t, GDN_VALUE_DIM), BF16),
            jax.ShapeDtypeStruct((b, GDN_V_HEADS, GDN_HEAD, GDN_HEAD), F32),
        ],
        scratch_shapes=[pltpu.VMEM((gb if n_steps * gt > 1 else 1, GDN_V_HEADS, GDN_HEAD, GDN_HEAD), F32)],
        compiler_params=pltpu.CompilerParams(
            dimension_semantics=("parallel", "arbitrary"), vmem_limit_bytes=VMEM_LIMIT_BYTES),
        name="gdn_chunk",
    )(act, ba, s0, alog_row, dtb_row)


def _gdn_out_kernel(o_ref, gate_ref, x_ref, gw_ref, w_ref, y_ref):
    gw = gw_ref[...]
    parts = []
    for h in range(GDN_V_HEADS):
        sl = slice(h * GDN_HEAD, (h + 1) * GDN_HEAD)
        parts.append((_rms(o_ref[:, sl].astype(F32), gw) * gate_ref[:, sl].astype(F32)).astype(BF16))
    gated = jnp.concatenate(parts, axis=1)
    y_ref[...] = x_ref[...] + jnp.dot(gated, w_ref[...], preferred_element_type=F32)


def _gdn_out(o2d, act2d, x2d, gw, w_out, tm):
    m = x2d.shape[0]
    z_block = GDN_CONV_DIM // GDN_VALUE_DIM
    return pl.pallas_call(
        _gdn_out_kernel,
        grid=(m // tm,),
        in_specs=[
            pl.BlockSpec((tm, GDN_VALUE_DIM), lambda i: (i, 0)),
            pl.BlockSpec((tm, GDN_VALUE_DIM), lambda i: (i, z_block)),
            pl.BlockSpec((tm, D_MODEL), lambda i: (i, 0)),
            pl.BlockSpec((1, GDN_HEAD), lambda i: (0, 0)),
            pl.BlockSpec((GDN_VALUE_DIM, D_MODEL), lambda i: (0, 0)),
        ],
        out_specs=pl.BlockSpec((tm, D_MODEL), lambda i: (i, 0)),
        out_shape=jax.ShapeDtypeStruct((m, D_MODEL), F32),
        compiler_params=pltpu.CompilerParams(
            dimension_semantics=("parallel",), vmem_limit_bytes=VMEM_LIMIT_BYTES),
        name="gdn_out",
    )(o2d, act2d, x2d, gw, w_out)


def _head_sum(x, ones_bd):
    return jnp.concatenate([_bdot(x[:, p * LANES:(p + 1) * LANES], ones_bd) for p in range(RWKV_PAIRS)], axis=1)


def _rwkv_in_kernel(x_ref, xp_ref, sh_ref, nw_ref, mu_ref, wr_ref, wk_ref, wv_ref, wz_ref,
                    w0_ref, w1_ref, w2_ref, a0_ref, a1_ref, a2_ref, kk_ref, ka_ref, rk_ref, ones_ref,
                    r_out, k_out, v_out, ld_out, kn_out, b_out, z_out, bonus_out, last_out, *, tb, tt, n_t):
    t = pl.program_id(1)
    m = tb * tt
    nw = nw_ref[...]
    xn = _rms(x_ref[...].reshape(m, D_MODEL), nw)
    prev_last = _rms(xp_ref[:, SUBLANES - 1:SUBLANES, :], nw)
    first = jnp.where(t == 0, sh_ref[...], prev_last)
    first = jnp.broadcast_to(first, (tb, tt, D_MODEL)).reshape(m, D_MODEL)
    rolled = pltpu.roll(xn, 1, axis=0)
    rows = lax.broadcasted_iota(jnp.int32, (m, 1), 0)
    xprev = jnp.where((rows & (tt - 1)) == 0, first, rolled)
    xx = xprev - xn
    mu = mu_ref[...]

    def mix(i):
        return xn + xx * mu[i:i + 1, :]

    r = _bdot(mix(0), wr_ref[...])
    k = _bdot(mix(1), wk_ref[...])
    v = _bdot(mix(2), wv_ref[...])
    z = _bdot(mix(3), wz_ref[...])
    ld = -DECAY_SCALE * _sigmoid(w0_ref[...] + _bdot(jnp.tanh(_bdot(mix(4), w1_ref[...])), w2_ref[...]))
    a = _sigmoid(a0_ref[...] +_bdot(_bdot(mix(5), a1_ref[...]), a2_ref[...]))
    ones_bd = ones_ref[...]
    kk = k * kk_ref[...]
    kn = kk * lax.rsqrt(_head_sum(kk * kk, ones_bd) + L2_EPS)
    k = k * (1.0 + (a - 1.0) * ka_ref[...])
    r_out[...] = r
    k_out[...] = k
    v_out[...] = v
    z_out[...] = z
    ld_out[...] = ld
    kn_out[...] = kn
    b_out[...] = kn * a
    bonus_out[...] = _head_sum(r * k * rk_ref[...], ones_bd) * v

    @pl.when(t == n_t - 1)
    def _():
        last_out[...] = _rms(x_ref[:, tt - 1:tt, :], nw)


def _rwkv_in(x3d, shift, nw, mu, wr, wk, wv, wz, w0, w1, w2, a0, a1, a2, k_k, k_a, r_k, ones_bd, tb, tt):
    b, t, _ = x3d.shape
    n_t = t // tt
    tpb = tt // SUBLANES
    kern = functools.partial(_rwkv_in_kernel, tb=tb, tt=tt, n_t=n_t)
    row = lambda n: pl.BlockSpec((n, D_MODEL), lambda i, j: (0, 0))
    big = pl.BlockSpec((D_MODEL, D_MODEL), lambda i, j: (0, 0))
    tok = pl.BlockSpec((tb * tt, D_MODEL), lambda i, j: (i * n_t + j, 0))
    m = b * t
    outs = pl.pallas_call(
        kern,
        grid=(b // tb, n_t),
        in_specs=[
            pl.BlockSpec((tb, tt, D_MODEL), lambda i, j: (i, j, 0)),
            pl.BlockSpec((tb, SUBLANES, D_MODEL), lambda i, j: (i, jnp.maximum(j * tpb - 1, 0), 0)),
            pl.BlockSpec((tb, 1, D_MODEL), lambda i, j: (i, 0, 0)),
            row(1), row(6), big, big, big, big,
            row(1), pl.BlockSpec((D_MODEL, LORA), lambda i, j: (0, 0)), pl.BlockSpec((LORA, D_MODEL), lambda i, j: (0, 0)),
            row(1), pl.BlockSpec((D_MODEL, LORA), lambda i, j: (0, 0)), pl.BlockSpec((LORA, D_MODEL), lambda i, j: (0, 0)),
            row(1), row(1), row(1),
            pl.BlockSpec((LANES, LANES), lambda i, j: (0, 0)),
        ],
        out_specs=[tok] * 8 + [pl.BlockSpec((tb, 1, D_MODEL), lambda i, j: (i, 0, 0))],
        out_shape=[jax.ShapeDtypeStruct((m, D_MODEL), F32)] * 8 + [jax.ShapeDtypeStruct((b, 1, D_MODEL), F32)],
        compiler_params=pltpu.CompilerParams(
            dimension_semantics=("parallel", "arbitrary"), vmem_limit_bytes=VMEM_LIMIT_BYTES),
        name="rwkv_in",
    )(x3d, x3d, shift, nw, mu, wr, wk, wv, wz, w0, w1, w2, a0, a1, a2, k_k, k_a, r_k, ones_bd)
    return outs


def _rwkv_rec_kernel(r_ref, k_ref, v_ref, ld_ref, kn_ref, b_ref, s0_ref, y_ref, sout_ref, s_scr,
                     *, chunk, gb, gt, n_steps):
    step = pl.program_id(1)
    C = chunk
    C2 = 2 * C
    carried = n_steps * gt > 1

    def pair_state(rr, p):
        z = jnp.zeros((RWKV_HEAD, RWKV_HEAD), F32)
        return jnp.concatenate([jnp.concatenate([s0_ref[rr, 2 * p], z], axis=1),
                                jnp.concatenate([z, s0_ref[rr, 2 * p + 1]], axis=1)], axis=0)

    def put_state(rr, p, val):
        sout_ref[rr, 2 * p] = val[:RWKV_HEAD, :RWKV_HEAD]
        sout_ref[rr, 2 * p + 1] = val[RWKV_HEAD:, RWKV_HEAD:]

    if carried:
        @pl.when(step == 0)
        def _():
            for rr in range(gb):
                for p in range(RWKV_PAIRS):
                    s_scr[rr, p] = pair_state(rr, p)

    tril = _tril3(C, C)
    lane = lax.broadcasted_iota(jnp.int32, (1, LANES), 1)
    m_lo = lane < RWKV_HEAD
    rc, cc = _iota2(C, C2)
    tl = cc & (C - 1)
    strict_pair = rc > tl
    incl_pair = rc >= tl

    def stack(x):
        return jnp.concatenate([jnp.where(m_lo, x, 0.0), jnp.where(m_lo, 0.0, x)], axis=0)

    pairs = []
    for rr in range(gb):
        for j in range(gt):
            rows = slice(j * C, (j + 1) * C)
            for p in range(RWKV_PAIRS):
                sl = slice(p * LANES, (p + 1) * LANES)
                pairs.append(dict(rr=rr, j=j, p=p, rows=rows, sl=sl))
    for pr in pairs:
        pr["gc"] = _cumsum_rows(tril, ld_ref[pr["rr"], pr["rows"], pr["sl"]], "nn")
    for pr in pairs:
        at_ref = (pr["rr"], pr["rows"], pr["sl"])
        gc = pr["gc"]
        k = k_ref[at_ref]
        b = b_ref[at_ref]
        glast = gc[C - 1:C, :]
        e_neg = jnp.exp(-gc)
        e_end = jnp.exp(glast - gc)
        rt = r_ref[at_ref] * jnp.exp(gc)
        at = -kn_ref[at_ref] * jnp.exp(gc - ld_ref[at_ref])
        at_s = stack(at)
        pr.update(rt=rt, at_s=at_s, v_s=stack(v_ref[at_ref]).astype(BF16),
                  kh_s=stack(k * e_end), bh_s=stack(b * e_end).astype(BF16), eg=jnp.exp(glast))
        g = _bdot_nt(jnp.concatenate([at_s, rt], axis=0),
                     jnp.concatenate([stack(k * e_neg), stack(b * e_neg)], axis=0))
        pr["low"] = -g[:C2, C2:]
        pr["rb"] = jnp.where(incl_pair, g[C2:, C2:], 0.0).astype(BF16)
        pr["rk"] = jnp.where(incl_pair, g[C2:, :C2], 0.0).astype(BF16)
        pr["ak"] = jnp.where(strict_pair, g[:C, :C2] + g[C:C2, :C2], 0.0)
    for pr in pairs:
        akv = jnp.dot(pr["ak"].astype(BF16), pr["v_s"], preferred_element_type=F32)
        pr["rhs"] = jnp.concatenate([pr["at_s"], stack(akv)], axis=1).astype(BF16)

    tinvs = _tri_inv_many([pr["low"] for pr in pairs], C2, C, half_rows=False)
    sols = [jnp.dot(t, pr["rhs"], preferred_element_type=F32).astype(BF16) for t, pr in zip(tinvs, pairs)]
    for pr, sol in zip(pairs, sols):
        a2_s = sol[:, :LANES]
        u2_s = sol[:, LANES:]
        d = functools.partial(jnp.dot, preferred_element_type=F32)
        pr["r2"] = pr["rt"] + d(pr["rb"], a2_s)
        pr["y2"] = d(pr["rk"], pr["v_s"]) + d(pr["rb"], u2_s)
        pr["w2"] = _bdot_tn(a2_s, pr["bh_s"])
        pr["n2"] = _bdot_tn(jnp.concatenate([pr["v_s"], u2_s], axis=0),
                            jnp.concatenate([pr["kh_s"].astype(BF16), pr["bh_s"]], axis=0))

    for pr in pairs:
        rr, p = pr["rr"], pr["p"]
        st = s_scr[rr, p] if carried else pair_state(rr, p)
        y_ref[rr, pr["rows"], pr["sl"]] = _bdot_nt(pr["r2"], st) + pr["y2"]
        new = st * pr["eg"] + (_bdot(st, pr["w2"]) + pr["n2"])
        if carried:
            s_scr[rr, p] = new
        else:
            put_state(rr, p, new)

    if carried:
        @pl.when(step == n_steps - 1)
        def _():
            for rr in range(gb):
                for p in range(RWKV_PAIRS):
                    put_state(rr, p, s_scr[rr, p])


def _rwkv_rec(r, k, v, ld, kn, bvec, s0, chunk, gb, gt):
    b, t, _ = r.shape
    span = gt * chunk
    n_steps = t // span
    assert b % gb == 0 and t % span == 0 and (gb == 1 or n_steps == 1)
    kern = functools.partial(_rwkv_rec_kernel, chunk=chunk, gb=gb, gt=gt, n_steps=n_steps)
    tok = pl.BlockSpec((gb, span, D_MODEL), lambda i, c: (i, c, 0))
    st = pl.BlockSpec((gb, RWKV_HEADS, RWKV_HEAD, RWKV_HEAD), lambda i, c: (i, 0, 0, 0))
    return pl.pallas_call(
        kern,
        grid=(b // gb, n_steps),
        in_specs=[tok] * 6 + [st],
        out_specs=[tok, st],
        out_shape=[
            jax.ShapeDtypeStruct((b, t, D_MODEL), F32),
            jax.ShapeDtypeStruct((b, RWKV_HEADS, RWKV_HEAD, RWKV_HEAD), F32),
        ],
        scratch_shapes=[pltpu.VMEM((gb if n_steps * gt > 1 else 1, RWKV_PAIRS, LANES, LANES), F32)],
        compiler_params=pltpu.CompilerParams(
            dimension_semantics=("parallel", "arbitrary"), vmem_limit_bytes=VMEM_LIMIT_BYTES),
        name="rwkv_rec",
    )(r, k, v, ld, kn, bvec, s0)


def _rwkv_out_kernel(y_ref, z_ref, bonus_ref, x_ref, lw_ref, lb_ref, avg_ref, w_ref, fw_ref, out_ref):
    y = y_ref[...]
    avg = avg_ref[...]
    mean = _head_sum(y, avg)
    yc = y - mean
    var = _head_sum(yc * yc, avg)
    gn = yc * lax.rsqrt(var + GN_EPS) * lw_ref[...] + lb_ref[...]
    o = (gn + bonus_ref[...]) * _silu(z_ref[...])
    x2 = x_ref[...] + _bdot(o, w_ref[...])
    out_ref[...] = _rms(x2, fw_ref[...])


def _rwkv_out(y2d, z2d, bonus2d, x2d, lnx_w, lnx_b, avg_bd, w_o, fw, tm):
    m = x2d.shape[0]
    tok = pl.BlockSpec((tm, D_MODEL), lambda i: (i, 0))
    row = pl.BlockSpec((1, D_MODEL), lambda i: (0, 0))
    return pl.pallas_call(
        _rwkv_out_kernel,
        grid=(m // tm,),
        in_specs=[tok, tok, tok, tok, row, row,
                  pl.BlockSpec((LANES, LANES), lambda i: (0, 0)),
                  pl.BlockSpec((D_MODEL, D_MODEL), lambda i: (0, 0)), row],
        out_specs=tok,
        out_shape=jax.ShapeDtypeStruct((m, D_MODEL), F32),
        compiler_params=pltpu.CompilerParams(
            dimension_semantics=("parallel",), vmem_limit_bytes=VMEM_LIMIT_BYTES),
        name="rwkv_out",
    )(y2d, z2d, bonus2d, x2d, lnx_w, lnx_b, avg_bd, w_o, fw)


def _pick_tile(m, cap):
    t = min(m, cap)
    while m % t:
        t //= 2
    return t


def _step_groups(b, t, chunk, chunks_per_step):
    n_chunks = t // chunk
    if n_chunks > 1:
        return 1, _pick_tile(n_chunks, chunks_per_step)
    return _pick_tile(b, ROWS_PER_STEP), 1


def _trunk(x, gdn_s, gdn_conv, rwkv_s, rwkv_shift, p, gdn_chunk, rwkv_chunk):
    b, t, _ = x.shape
    m = b * t
    x2d = x.reshape(m, D_MODEL)
    prev8 = jnp.pad(gdn_conv, ((0, 0), (SUBLANES - (CONV_W - 1), 0), (0, 0)))
    act, ba, last8 = _gdn_in(x2d, p["nw0"], p["w_in"], p["w_ba"], prev8, p["conv_wt"], t)
    conv_new = last8.reshape(b, -1, SUBLANES, GDN_CONV_DIM)[:, -1, SUBLANES - (CONV_W - 1):, :]
    o, s_gdn = _gdn_chunk(act.reshape(b, t, GDN_QKVZ_DIM), ba.reshape(b, t, LANES), gdn_s,
                          p["alog_row"], p["dtb_row"], gdn_chunk, *_step_groups(b, t, gdn_chunk, GDN_CHUNKS_PER_STEP))
    tm2 = _pick_tile(m, 256)
    x1 = _gdn_out(o.reshape(m, GDN_VALUE_DIM), act, x2d, p["gn_w"], p["w_out"], tm2)
    if t >= 256:
        tb, tt = 1, 256
    elif t * b <= 256:
        tb, tt = b, t
    else:
        tb, tt = 256 // t, t
    r, k, v, ld, kn, bvec, z, bonus, last = _rwkv_in(
        x1.reshape(b, t, D_MODEL), rwkv_shift.reshape(b, 1, D_MODEL), p["nw1"], p["mu"],
        p["wr"], p["wk"], p["wv"], p["wz"], p["w0"], p["w1"], p["w2"], p["a0"], p["a1"], p["a2"],
        p["k_k"], p["k_a"], p["r_k"], p["ones_bd"], tb, tt)
    sh3 = lambda a: a.reshape(b, t, D_MODEL)
    y, s_rwkv = _rwkv_rec(sh3(r), sh3(k), sh3(v), sh3(ld), sh3(kn), sh3(bvec), rwkv_s,
                          rwkv_chunk, *_step_groups(b, t, rwkv_chunk, RWKV_CHUNKS_PER_STEP))
    out = _rwkv_out(y.reshape(m, D_MODEL), z, bonus, x1, p["lnx_w"], p["lnx_b"], p["avg_bd"], p["w_o"], p["fw"], tm2)
    return (out.reshape(b, t, D_MODEL), s_gdn, conv_new, s_rwkv, last.reshape(b, D_MODEL))


def kernel(x_prompt, x_sample, state_gdn, state_gdn_conv, state_rwkv, state_rwkv_shift, meta_tokens, norm_w, final_norm_w, gdn_w_in, gdn_conv_w, gdn_a_log, gdn_dt_bias, gdn_norm_w, gdn_w_out, rwkv_mu, rwkv_w_rkvz, rwkv_w0, rwkv_w1, rwkv_w2, rwkv_a0, rwkv_a1, rwkv_a2, rwkv_k_k, rwkv_k_a, rwkv_r_k, rwkv_lnx_w, rwkv_lnx_b, rwkv_w_o):
    assert norm_w.shape[0] == 2 and gdn_w_in.shape[0] == 1 and rwkv_mu.shape[0] == 1
    row = lambda a: a.reshape(1, -1).astype(F32)
    w_in = gdn_w_in[0]
    w_ba = jnp.pad(w_in[:, GDN_QKVZ_DIM:], ((0, 0), (0, LANES - 2 * GDN_V_HEADS)))
    gate_row = lambda a: jnp.pad(a.astype(F32), (GDN_V_HEADS, LANES - 2 * GDN_V_HEADS)).reshape(1, LANES)
    rb, cb = jnp.arange(LANES)[:, None] // RWKV_HEAD, jnp.arange(LANES)[None, :] // RWKV_HEAD
    bd = (rb == cb)
    p = {
        "nw0": row(norm_w[0]), "nw1": row(norm_w[1]), "fw": row(final_norm_w),
        "w_in": w_in.astype(BF16), "w_ba": w_ba.astype(BF16),
        "conv_wt": gdn_conv_w[0].T.astype(F32),
        "alog_row": gate_row(gdn_a_log[0]), "dtb_row": gate_row(gdn_dt_bias[0]),
        "gn_w": row(gdn_norm_w[0]), "w_out": gdn_w_out[0].astype(BF16),
        "mu": rwkv_mu[0].astype(F32),
        "wr": rwkv_w_rkvz[0, 0].astype(BF16), "wk": rwkv_w_rkvz[0, 1].astype(BF16),
        "wv": rwkv_w_rkvz[0, 2].astype(BF16), "wz": rwkv_w_rkvz[0, 3].astype(BF16),
        "w0": row(rwkv_w0[0]), "w1": rwkv_w1[0].astype(BF16), "w2": rwkv_w2[0].astype(BF16),
        "a0": row(rwkv_a0[0]), "a1": rwkv_a1[0].astype(BF16), "a2": rwkv_a2[0].astype(BF16),
        "k_k": row(rwkv_k_k[0]), "k_a": row(rwkv_k_a[0]), "r_k": row(rwkv_r_k[0]),
        "lnx_w": row(rwkv_lnx_w[0]), "lnx_b": row(rwkv_lnx_b[0]), "w_o": rwkv_w_o[0].astype(BF16),
        "ones_bd": bd.astype(BF16), "avg_bd": (bd.astype(F32) / RWKV_HEAD).astype(BF16),
    }
    bp = x_prompt.shape[0]
    zeros = lambda *s: jnp.zeros(s, F32)
    _, m_gdn, m_conv, m_rwkv, m_shift = _trunk(
        meta_tokens.astype(F32)[None], zeros(1, GDN_V_HEADS, GDN_HEAD, GDN_HEAD), zeros(1, CONV_W - 1, GDN_CONV_DIM),
        zeros(1, RWKV_HEADS, RWKV_HEAD, RWKV_HEAD), zeros(1, D_MODEL), p, N_META, N_META)
    rep = lambda a: jnp.broadcast_to(a, (bp,) + a.shape[1:])
    y_p, p_gdn, p_conv, p_rwkv, p_shift = _trunk(
        x_prompt, rep(m_gdn), rep(m_conv), rep(m_rwkv), rep(m_shift), p, GDN_CHUNK, RWKV_CHUNK)
    ts = x_sample.shape[1]
    y_s, s_gdn, s_conv, s_rwkv, s_shift = _trunk(
        x_sample, state_gdn[0], state_gdn_conv[0], state_rwkv[0], state_rwkv_shift[0], p, ts, ts)
    return (y_p, y_s, p_gdn[None], p_conv[None], p_rwkv[None], p_shift[None],
            s_gdn[None], s_conv[None], s_rwkv[None], s_shift[None])
```

```python
import functools

import jax
import jax.numpy as jnp
from jax import lax
from jax.experimental import pallas as pl
from jax.experimental.pallas import tpu as pltpu

F32 = jnp.float32
BF16 = jnp.bfloat16

D_MODEL = 1024
N_META = 16
GDN_QK_HEADS = 8
GDN_V_HEADS = 16
GDN_HEAD = 128
GDN_KEY_DIM = GDN_QK_HEADS * GDN_HEAD
GDN_VALUE_DIM = GDN_V_HEADS * GDN_HEAD
GDN_CONV_DIM = 2 * GDN_KEY_DIM + GDN_VALUE_DIM
GDN_QKVZ_DIM = GDN_CONV_DIM + GDN_VALUE_DIM
CONV_W = 4
GDN_TN = 1024
GDN_CONV_TILES = GDN_CONV_DIM // GDN_TN
GDN_SLAB = 256
GDN_ROWS = 128
GDN_TM = 1024
GDN_CHUNK = 64
RWKV_HEAD = 64
RWKV_HEADS = D_MODEL // RWKV_HEAD
RWKV_PAIRS = RWKV_HEADS // 2
RWKV_CHUNK = 64
LORA = 64
RMS_EPS = 1e-6
L2_EPS = 1e-6
GN_EPS = 64e-5
DECAY_SCALE = 0.6065306597126334
LOG2_E = 1.4426950408889634
LANES = 128
SUBLANES = 8
BF16_ROWS = 16
VMEM_LIMIT_BYTES = 56 * 1024 * 1024
GDN_CHUNKS_PER_STEP = 4
RWKV_CHUNKS_PER_STEP = 2
ROWS_PER_STEP = 8


def _bdot(a, b):
    return jnp.dot(a.astype(BF16), b.astype(BF16), preferred_element_type=F32)


def _bdot_nt(a, b):
    return lax.dot_general(a.astype(BF16), b.astype(BF16), (((1,), (1,)), ((), ())), preferred_element_type=F32)


def _bdot_tn(a, b):
    return lax.dot_general(a.astype(BF16), b.astype(BF16), (((0,), (0,)), ((), ())), preferred_element_type=F32)


def _split3(a):
    hi = a.astype(BF16).astype(F32)
    r1 = a - hi
    mid = r1.astype(BF16).astype(F32)
    return jnp.concatenate([hi, mid, r1 - mid], axis=0).astype(BF16)


def _tril3(n_out, c):
    r, k = _iota2(n_out, 3 * c)
    k = jnp.where(k >= 2 * c, k - 2 * c, jnp.where(k >= c, k - c, k))
    return jnp.where((r & (c - 1)) >= k, 1.0, 0.0).astype(BF16)


def _cumsum_rows(m3, x, dims):
    x3 = _split3(x)
    if dims == "nn":
        return jnp.dot(m3, x3, preferred_element_type=F32)
    return lax.dot_general(x3, m3, (((0,), (1,)), ((), ())), preferred_element_type=F32)


def _rms(x, w):
    return x * lax.rsqrt(jnp.mean(x * x, axis=-1, keepdims=True) + RMS_EPS) * w


def _sigmoid(x):
    return 1.0 / (1.0 + jnp.exp2(x * -LOG2_E))


def _silu(x):
    return x * _sigmoid(x)


def _softplus(x):
    return jnp.maximum(x, 0.0) + jnp.log1p(jnp.exp(-jnp.abs(x)))


def _iota2(n, m):
    return lax.broadcasted_iota(jnp.int32, (n, m), 0), lax.broadcasted_iota(jnp.int32, (n, m), 1)


def _log2(n):
    l = n.bit_length() - 1
    assert (1 << l) == n, n
    return l


def _tri_inv_many(lows, n, blk, half_rows):
    row, col = _iota2(n, n)
    eye = jnp.where(row == col, 1.0, 0.0).astype(BF16)
    negs = [(-low).astype(BF16) for low in lows]
    d = lambda a, b: jnp.dot(a, b, preferred_element_type=F32).astype(BF16)
    s = 1
    ts = None
    while s < blk:
        sh = _log2(s)
        if half_rows and s >= BF16_ROWS:
            hr, hc = _iota2(n // 2, n)
            row_blk = ((hr >> sh) << 1) + 1
            m = (row_blk >> 1 == (hc >> sh) >> 1) & (((hc >> sh) & 1) == 0)
            lower = lambda t: jnp.concatenate([t[(2 * j + 1) * s:(2 * j + 2) * s] for j in range(n // (2 * s))], axis=0)
            tl = [lower(t) for t in ts]
            xs = [d(t, ng) for t, ng in zip(tl, negs)]
            zs = [d(x, t) for x, t in zip(xs, ts)]
            new = [jnp.where(m, z, t) for z, t in zip(zs, tl)]
            ts = [jnp.concatenate([piece for j in range(n // (2 * s))
                                   for piece in (t[2 * j * s:(2 * j + 1) * s], nw[j * s:(j + 1) * s])], axis=0)
                  for t, nw in zip(ts, new)]
        else:
            sub_r = row >> sh
            sub_c = col >> sh
            m = ((sub_r >> 1) == (sub_c >> 1)) & ((sub_r & 1) == 1) & ((sub_c & 1) == 0)
            if ts is None:
                ts = [jnp.where(m, ng, eye) for ng in negs]
            else:
                xs = [d(t, ng) for t, ng in zip(ts, negs)]
                zs = [d(x, t) for x, t in zip(xs, ts)]
                ts = [jnp.where(m, z, t) for z, t in zip(zs, ts)]
        s *= 2
    return [eye for _ in lows] if ts is None else ts


def _gdn_in_kernel(x_ref, nw_ref, w_ref, wba_ref, prev_ref, cw_ref, act_ref, ba_ref, last_ref, xn_scr, tail_scr, u_scr,
                   *, tm, groups, tiles_per_row):
    i = pl.program_id(0)
    n = pl.program_id(1)

    @pl.when(n == 0)
    def _():
        xb = _rms(x_ref[...], nw_ref[...]).astype(BF16)
        xn_scr[...] = xb
        ba_ref[...] = jnp.dot(xb, wba_ref[...], preferred_element_type=F32)

    @pl.when((n == 0) & (i == 0))
    def _():
        tail_scr[...] = jnp.zeros(tail_scr.shape, F32)

    tn = w_ref.shape[1]
    n_slabs = tn // GDN_SLAB
    col = lambda c: slice(c * GDN_SLAB, (c + 1) * GDN_SLAB)

    def slab(c):
        return jnp.dot(xn_scr[...], w_ref[:, col(c)], preferred_element_type=F32)

    def l2n(a, scale):
        parts = []
        for h in range(GDN_SLAB // GDN_HEAD):
            ah = a[:, h * GDN_HEAD:(h + 1) * GDN_HEAD]
            parts.append(ah * (lax.rsqrt(jnp.sum(ah * ah, axis=-1, keepdims=True) + L2_EPS) * scale))
        return jnp.concatenate(parts, axis=1)

    def run_conv(scale):
        if groups == 1:
            def fill(c):
                u_scr[c % 2, SUBLANES:SUBLANES + tm, :] = slab(c)
            fill(0)
            for c in range(n_slabs):
                cs, buf = col(c), c % 2
                if c + 1 < n_slabs:
                    fill(c + 1)
                u_scr[buf, 0:SUBLANES, :] = jnp.where((i % tiles_per_row) == 0, prev_ref[0, :, cs], tail_scr[n, :, cs])
                end = u_scr[buf, tm:tm + SUBLANES, :]
                tail_scr[n, :, cs] = end
                last_ref[0, :, cs] = end
                nr = min(GDN_ROWS, tm)
                for rb in range(tm // nr):
                    r0 = SUBLANES + rb * nr
                    y = u_scr[buf, r0:r0 + nr, :] * cw_ref[CONV_W - 1:CONV_W, cs]
                    for s in range(1, CONV_W):
                        y = y + u_scr[buf, r0 - s:r0 - s + nr, :] * cw_ref[CONV_W - 1 - s:CONV_W - s, cs]
                    a = _silu(y)
                    if scale is not None:
                        a = l2n(a, scale)
                    act_ref[rb * nr:(rb + 1) * nr, cs] = a.astype(BF16)
        else:
            rows = lax.broadcasted_iota(jnp.int32, (tm, 1), 0)
            t_in = rows & (SUBLANES - 1)
            u_next = slab(0)
            for c in range(n_slabs):
                cs, u = col(c), u_next
                if c + 1 < n_slabs:
                    u_next = slab(c + 1)
                prev = prev_ref[:, :, cs].reshape(tm, GDN_SLAB)
                y = u * cw_ref[CONV_W - 1:CONV_W, cs]
                for s in range(1, CONV_W):
                    shifted = jnp.where(t_in < s, pltpu.roll(prev, tm - SUBLANES + s, axis=0), pltpu.roll(u, s, axis=0))
                    y = y + shifted * cw_ref[CONV_W - 1 - s:CONV_W - s, cs]
                last_ref[:, :, cs] = u.reshape(groups, SUBLANES, GDN_SLAB)
                a = _silu(y)
                if scale is not None:
                    a = l2n(a, scale)
                act_ref[:, cs] = a.astype(BF16)

    @pl.when(n < 2)
    def _():
        run_conv(jnp.where(n == 0, GDN_HEAD ** -0.5, 1.0))

    @pl.when((n >= 2) & (n < GDN_CONV_TILES))
    def _():
        run_conv(None)

    @pl.when(n >= GDN_CONV_TILES)
    def _():
        for c in range(n_slabs):
            act_ref[:, col(c)] = _silu(slab(c)).astype(BF16)


def _gdn_in(x2d, nw, w_in, w_ba, prev8, conv_wt, t):
    m = x2d.shape[0]
    b = m // t
    tn = GDN_TN
    conv_tile = lambda j: jnp.minimum(j, GDN_CONV_TILES - 1)
    if t == SUBLANES:
        tm = _pick_tile(m, GDN_TM)
        groups, tiles_per_row = tm // SUBLANES, 1
        before = pl.BlockSpec((groups, SUBLANES, tn), lambda i, j: (i, 0, conv_tile(j)))
    else:
        tm = _pick_tile(t, GDN_TM)
        groups, tiles_per_row = 1, t // tm
        assert tm % SUBLANES == 0 and tm > SUBLANES
        before = pl.BlockSpec((1, SUBLANES, tn), lambda i, j: (i // tiles_per_row, 0, conv_tile(j)))
    after = pl.BlockSpec((groups, SUBLANES, tn), lambda i, j: (i, 0, conv_tile(j)))
    kern = functools.partial(_gdn_in_kernel, tm=tm, groups=groups, tiles_per_row=tiles_per_row)
    return pl.pallas_call(
        kern,
        grid=(m // tm, GDN_QKVZ_DIM // tn),
        in_specs=[
            pl.BlockSpec((tm, D_MODEL), lambda i, j: (i, 0)),
            pl.BlockSpec((1, D_MODEL), lambda i, j: (0, 0)),
            pl.BlockSpec((D_MODEL, tn), lambda i, j: (0, j)),
            pl.BlockSpec((D_MODEL, LANES), lambda i, j: (0, 0)),
            before,
            pl.BlockSpec((CONV_W, tn), lambda i, j: (0, conv_tile(j))),
        ],
        out_specs=[
            pl.BlockSpec((tm, tn), lambda i, j: (i, j)),
            pl.BlockSpec((tm, LANES), lambda i, j: (i, 0)),
            after,
        ],
        out_shape=[
            jax.ShapeDtypeStruct((m, GDN_QKVZ_DIM), BF16),
            jax.ShapeDtypeStruct((m, LANES), F32),
            jax.ShapeDtypeStruct((m // tm * groups, SUBLANES, GDN_CONV_DIM), F32),
        ],
        scratch_shapes=[pltpu.VMEM((tm, D_MODEL), BF16), pltpu.VMEM((GDN_CONV_TILES, SUBLANES, tn), F32),
                        pltpu.VMEM((2, tm + SUBLANES, GDN_SLAB), F32)],
        compiler_params=pltpu.CompilerParams(
            dimension_semantics=("arbitrary", "arbitrary"), vmem_limit_bytes=VMEM_LIMIT_BYTES),
        name="gdn_in",
    )(x2d, nw, w_in, w_ba, prev8, conv_wt)


def _gdn_chunk_kernel(act_ref, ba_ref, s0_ref, alog_ref, dtb_ref, o_ref, sout_ref, s_scr, *, chunk, gb, gt, n_steps):
    step = pl.program_id(1)
    C = chunk
    C2 = 2 * C
    carried = n_steps * gt > 1

    if carried:
        @pl.when(step == 0)
        def _():
            s_scr[...] = s0_ref[...]

    tril = _tril3(C, C)
    tril2 = _tril3(C2, C)
    row, col = _iota2(C2, C2)
    causal = ((row >= C) == (col >= C)) & (row >= col)
    top = lax.broadcasted_iota(jnp.int32, (C2, 1), 0) < C
    left = lax.broadcasted_iota(jnp.int32, (1, C2), 1) < C

    pairs = []
    for r in range(gb):
        ba = ba_ref[r]
        beta_l = _sigmoid(ba)
        g_l = -jnp.exp(alog_ref[...]) * _softplus(ba + dtb_ref[...])
        for j in range(gt):
            rows = slice(j * C, (j + 1) * C)
            beta_all = beta_l[rows]
            gc_all = _cumsum_rows(tril, g_l[rows], "nn")
            gct_all = _cumsum_rows(tril2, g_l[rows], "tn")
            for qh in range(GDN_QK_HEADS):
                q = act_ref[r, rows, qh * GDN_HEAD:(qh + 1) * GDN_HEAD].astype(F32)
                k = act_ref[r, rows, GDN_KEY_DIM + qh * GDN_HEAD:GDN_KEY_DIM + (qh + 1) * GDN_HEAD].astype(F32)
                h0 = 2 * qh
                h1 = h0 + 1
                a0, a1 = GDN_V_HEADS + h0, GDN_V_HEADS + h1
                beta_s = jnp.concatenate([beta_all[:, h0:h0 + 1], beta_all[:, h1:h1 + 1]], axis=0)
                gc_s = jnp.concatenate([gc_all[:, a0:a0 + 1], gc_all[:, a1:a1 + 1]], axis=0)
                gc_row = jnp.where(left, gct_all[a0:a0 + 1, :], gct_all[a1:a1 + 1, :])
                glast0 = gc_all[C - 1:C, a0:a0 + 1]
                glast1 = gc_all[C - 1:C, a1:a1 + 1]
                glast_s = jnp.where(top, glast0, glast1)
                decay = jnp.exp(jnp.where(causal, gc_s - gc_row, -jnp.inf))
                k_s = jnp.concatenate([k, k], axis=0)
                q_s = jnp.concatenate([q, q], axis=0)
                v0 = 2 * GDN_KEY_DIM + h0 * GDN_HEAD
                v_s = jnp.concatenate([act_ref[r, rows, v0:v0 + GDN_HEAD],
                                       act_ref[r, rows, v0 + GDN_HEAD:v0 + 2 * GDN_HEAD]], axis=0).astype(F32)
                egc = jnp.exp(gc_s)
                kb_s = k_s * beta_s
                pairs.append(dict(
                    r=r, j=j, h0=h0, h1=h1,
                    low=_bdot_nt(kb_s, k_s) * decay,
                    rhs=jnp.concatenate([v_s * beta_s, kb_s * egc], axis=1).astype(BF16),
                    attn=_bdot_nt(q_s, k_s) * decay, qd_s=q_s * egc, kd_s=k_s * jnp.exp(glast_s - gc_s),
                    eg0=jnp.exp(glast0), eg1=jnp.exp(glast1)))

    tinvs = _tri_inv_many([pr["low"] for pr in pairs], C2, C, half_rows=True)
    sols = [jnp.dot(t, pr["rhs"], preferred_element_type=F32) for t, pr in zip(tinvs, pairs)]

    def state_in(r, h):
        return s_scr[r, h] if carried else s0_ref[r, h]

    def state_out(r, h, val):
        if carried:
            s_scr[r, h] = val
        else:
            sout_ref[r, h] = val

    for r in range(gb):
        for j in range(gt):
            grp = [(pr, sol) for pr, sol in zip(pairs, sols) if pr["r"] == r and pr["j"] == j]
            rows = slice(j * C, (j + 1) * C)
            sts, wqs = [], []
            for pr, sol in grp:
                w_s = sol[:, GDN_HEAD:]
                st0 = state_in(r, pr["h0"])
                st1 = state_in(r, pr["h1"])
                sts.append((st0, st1))
                wqs.append((_bdot(jnp.concatenate([w_s[:C], pr["qd_s"][:C]], axis=0), st0),
                            _bdot(jnp.concatenate([w_s[C:], pr["qd_s"][C:]], axis=0), st1)))
            for (pr, sol), (st0, st1), (wq0, wq1) in zip(grp, sts, wqs):
                h0, h1 = pr["h0"], pr["h1"]
                u_s = sol[:, :GDN_HEAD]
                vn0 = u_s[:C] - wq0[:C]
                vn1 = u_s[C:] - wq1[:C]
                o_s = _bdot(pr["attn"], jnp.concatenate([vn0, vn1], axis=0))
                o_ref[r, rows, h0 * GDN_HEAD:(h0 + 1) * GDN_HEAD] = (o_s[:C] + wq0[C:]).astype(BF16)
                o_ref[r, rows, h1 * GDN_HEAD:(h1 + 1) * GDN_HEAD] = (o_s[C:] + wq1[C:]).astype(BF16)
                state_out(r, h0, st0 * pr["eg0"] + _bdot_tn(pr["kd_s"][:C], vn0))
                state_out(r, h1, st1 * pr["eg1"] + _bdot_tn(pr["kd_s"][C:], vn1))

    if carried:
        @pl.when(step == n_steps - 1)
        def _():
            sout_ref[...] = s_scr[...]


def _gdn_chunk(act, ba, s0, alog_row, dtb_row, chunk, gb, gt):
    b, t, _ = act.shape
    span = gt * chunk
    n_steps = t // span
    assert b % gb == 0 and t % span == 0 and (gb == 1 or n_steps == 1)
    kern = functools.partial(_gdn_chunk_kernel, chunk=chunk, gb=gb, gt=gt, n_steps=n_steps)
    state = pl.BlockSpec((gb, GDN_V_HEADS, GDN_HEAD, GDN_HEAD), lambda i, c: (i, 0, 0, 0))
    return pl.pallas_call(
        kern,
        grid=(b // gb, n_steps),
        in_specs=[
            pl.BlockSpec((gb, span, GDN_CONV_DIM), lambda i, c: (i, c, 0)),
            pl.BlockSpec((gb, span, LANES), lambda i, c: (i, c, 0)),
            state,
            pl.BlockSpec((1, LANES), lambda i, c: (0, 0)),
            pl.BlockSpec((1, LANES), lambda i, c: (0, 0)),
        ],
        out_specs=[pl.BlockSpec((gb, span, GDN_VALUE_DIM), lambda i, c: (i, c, 0)), state],
        out_shape=[
            jax.ShapeDtypeStruct((b, t, GDN_VALUE_DIM), BF16),
            jax.ShapeDtypeStruct((b, GDN_V_HEADS, GDN_HEAD, GDN_HEAD), F32),
        ],
        scratch_shapes=[pltpu.VMEM((gb if n_steps * gt > 1 else 1, GDN_V_HEADS, GDN_HEAD, GDN_HEAD), F32)],
        compiler_params=pltpu.CompilerParams(
            dimension_semantics=("parallel", "arbitrary"), vmem_limit_bytes=VMEM_LIMIT_BYTES),
        name="gdn_chunk",
    )(act, ba, s0, alog_row, dtb_row)


def _gdn_out_kernel(o_ref, gate_ref, x_ref, gw_ref, w_ref, y_ref):
    gw = gw_ref[...]
    parts = []
    for h in range(GDN_V_HEADS):
        sl = slice(h * GDN_HEAD, (h + 1) * GDN_HEAD)
        parts.append((_rms(o_ref[:, sl].astype(F32), gw) * gate_ref[:, sl].astype(F32)).astype(BF16))
    gated = jnp.concatenate(parts, axis=1)
    y_ref[...] = x_ref[...] + jnp.dot(gated, w_ref[...], preferred_element_type=F32)


def _gdn_out(o2d, act2d, x2d, gw, w_out, tm):
    m = x2d.shape[0]
    z_block = GDN_CONV_DIM // GDN_VALUE_DIM
    return pl.pallas_call(
        _gdn_out_kernel,
        grid=(m // tm,),
        in_specs=[
            pl.BlockSpec((tm, GDN_VALUE_DIM), lambda i: (i, 0)),
            pl.BlockSpec((tm, GDN_VALUE_DIM), lambda i: (i, z_block)),
            pl.BlockSpec((tm, D_MODEL), lambda i: (i, 0)),
            pl.BlockSpec((1, GDN_HEAD), lambda i: (0, 0)),
            pl.BlockSpec((GDN_VALUE_DIM, D_MODEL), lambda i: (0, 0)),
        ],
        out_specs=pl.BlockSpec((tm, D_MODEL), lambda i: (i, 0)),
        out_shape=jax.ShapeDtypeStruct((m, D_MODEL), F32),
        compiler_params=pltpu.CompilerParams(
            dimension_semantics=("parallel",), vmem_limit_bytes=VMEM_LIMIT_BYTES),
        name="gdn_out",
    )(o2d, act2d, x2d, gw, w_out)


def _head_sum(x, ones_bd):
    return jnp.concatenate([_bdot(x[:, p * LANES:(p + 1) * LANES], ones_bd) for p in range(RWKV_PAIRS)], axis=1)


def _rwkv_in_kernel(x_ref, xp_ref, sh_ref, nw_ref, mu_ref, wr_ref, wk_ref, wv_ref, wz_ref,
                    w0_ref, w1_ref, w2_ref, a0_ref, a1_ref, a2_ref, kk_ref, ka_ref, rk_ref, ones_ref,
                    r_out, k_out, v_out, ld_out, kn_out, b_out, z_out, bonus_out, last_out, *, tb, tt, n_t):
    t = pl.program_id(1)
    m = tb * tt
    nw = nw_ref[...]
    xn = _rms(x_ref[...].reshape(m, D_MODEL), nw)
    prev_last = _rms(xp_ref[:, SUBLANES - 1:SUBLANES, :], nw)
    first = jnp.where(t == 0, sh_ref[...], prev_last)
    first = jnp.broadcast_to(first, (tb, tt, D_MODEL)).reshape(m, D_MODEL)
    rolled = pltpu.roll(xn, 1, axis=0)
    rows = lax.broadcasted_iota(jnp.int32, (m, 1), 0)
    xprev = jnp.where((rows & (tt - 1)) == 0, first, rolled)
    xx = xprev - xn
    mu = mu_ref[...]

    def mix(i):
        return xn + xx * mu[i:i + 1, :]

    r = _bdot(mix(0), wr_ref[...])
    k = _bdot(mix(1), wk_ref[...])
    v = _bdot(mix(2), wv_ref[...])
    z = _bdot(mix(3), wz_ref[...])
    ld = -DECAY_SCALE * _sigmoid(w0_ref[...] + _bdot(jnp.tanh(_bdot(mix(4), w1_ref[...])), w2_ref[...]))
    a = _sigmoid(a0_ref[...] +_bdot(_bdot(mix(5), a1_ref[...]), a2_ref[...]))
    ones_bd = ones_ref[...]
    kk = k * kk_ref[...]
    kn = kk * lax.rsqrt(_head_sum(kk * kk, ones_bd) + L2_EPS)
    k = k * (1.0 + (a - 1.0) * ka_ref[...])
    r_out[...] = r
    k_out[...] = k
    v_out[...] = v
    z_out[...] = z
    ld_out[...] = ld
    kn_out[...] = kn
    b_out[...] = kn * a
    bonus_out[...] = _head_sum(r * k * rk_ref[...], ones_bd) * v

    @pl.when(t == n_t - 1)
    def _():
        last_out[...] = _rms(x_ref[:, tt - 1:tt, :], nw)


def _rwkv_in(x3d, shift, nw, mu, wr, wk, wv, wz, w0, w1, w2, a0, a1, a2, k_k, k_a, r_k, ones_bd, tb, tt):
    b, t, _ = x3d.shape
    n_t = t // tt
    tpb = tt // SUBLANES
    kern = functools.partial(_rwkv_in_kernel, tb=tb, tt=tt, n_t=n_t)
    row = lambda n: pl.BlockSpec((n, D_MODEL), lambda i, j: (0, 0))
    big = pl.BlockSpec((D_MODEL, D_MODEL), lambda i, j: (0, 0))
    tok = pl.BlockSpec((tb * tt, D_MODEL), lambda i, j: (i * n_t + j, 0))
    m = b * t
    outs = pl.pallas_call(
        kern,
        grid=(b // tb, n_t),
        in_specs=[
            pl.BlockSpec((tb, tt, D_MODEL), lambda i, j: (i, j, 0)),
            pl.BlockSpec((tb, SUBLANES, D_MODEL), lambda i, j: (i, jnp.maximum(j * tpb - 1, 0), 0)),
            pl.BlockSpec((tb, 1, D_MODEL), lambda i, j: (i, 0, 0)),
            row(1), row(6), big, big, big, big,
            row(1), pl.BlockSpec((D_MODEL, LORA), lambda i, j: (0, 0)), pl.BlockSpec((LORA, D_MODEL), lambda i, j: (0, 0)),
            row(1), pl.BlockSpec((D_MODEL, LORA), lambda i, j: (0, 0)), pl.BlockSpec((LORA, D_MODEL), lambda i, j: (0, 0)),
            row(1), row(1), row(1),
            pl.BlockSpec((LANES, LANES), lambda i, j: (0, 0)),
        ],
        out_specs=[tok] * 8 + [pl.BlockSpec((tb, 1, D_MODEL), lambda i, j: (i, 0, 0))],
        out_shape=[jax.ShapeDtypeStruct((m, D_MODEL), F32)] * 8 + [jax.ShapeDtypeStruct((b, 1, D_MODEL), F32)],
        compiler_params=pltpu.CompilerParams(
            dimension_semantics=("parallel", "arbitrary"), vmem_limit_bytes=VMEM_LIMIT_BYTES),
        name="rwkv_in",
    )(x3d, x3d, shift, nw, mu, wr, wk, wv, wz, w0, w1, w2, a0, a1, a2, k_k, k_a, r_k, ones_bd)
    return outs


def _rwkv_rec_kernel(r_ref, k_ref, v_ref, ld_ref, kn_ref, b_ref, s0_ref, y_ref, sout_ref, s_scr,
                     *, chunk, gb, gt, n_steps):
    step = pl.program_id(1)
    C = chunk
    C2 = 2 * C
    carried = n_steps * gt > 1

    def pair_state(rr, p):
        z = jnp.zeros((RWKV_HEAD, RWKV_HEAD), F32)
        return jnp.concatenate([jnp.concatenate([s0_ref[rr, 2 * p], z], axis=1),
                                jnp.concatenate([z, s0_ref[rr, 2 * p + 1]], axis=1)], axis=0)

    def put_state(rr, p, val):
        sout_ref[rr, 2 * p] = val[:RWKV_HEAD, :RWKV_HEAD]
        sout_ref[rr, 2 * p + 1] = val[RWKV_HEAD:, RWKV_HEAD:]

    if carried:
        @pl.when(step == 0)
        def _():
            for rr in range(gb):
                for p in range(RWKV_PAIRS):
                    s_scr[rr, p] = pair_state(rr, p)

    tril = _tril3(C, C)
    lane = lax.broadcasted_iota(jnp.int32, (1, LANES), 1)
    m_lo = lane < RWKV_HEAD
    rc, cc = _iota2(C, C2)
    tl = cc & (C - 1)
    strict_pair = rc > tl
    incl_pair = rc >= tl

    def stack(x):
        return jnp.concatenate([jnp.where(m_lo, x, 0.0), jnp.where(m_lo, 0.0, x)], axis=0)

    pairs = []
    for rr in range(gb):
        for j in range(gt):
            rows = slice(j * C, (j + 1) * C)
            for p in range(RWKV_PAIRS):
                sl = slice(p * LANES, (p + 1) * LANES)
                pairs.append(dict(rr=rr, j=j, p=p, rows=rows, sl=sl))
    for pr in pairs:
        pr["gc"] = _cumsum_rows(tril, ld_ref[pr["rr"], pr["rows"], pr["sl"]], "nn")
    for pr in pairs:
        at_ref = (pr["rr"], pr["rows"], pr["sl"])
        gc = pr["gc"]
        k = k_ref[at_ref]
        b = b_ref[at_ref]
        glast = gc[C - 1:C, :]
        e_neg = jnp.exp(-gc)
        e_end = jnp.exp(glast - gc)
        rt = r_ref[at_ref] * jnp.exp(gc)
        at = -kn_ref[at_ref] * jnp.exp(gc - ld_ref[at_ref])
        at_s = stack(at)
        pr.update(rt=rt, at_s=at_s, v_s=stack(v_ref[at_ref]).astype(BF16),
                  kh_s=stack(k * e_end), bh_s=stack(b * e_end).astype(BF16), eg=jnp.exp(glast))
        g = _bdot_nt(jnp.concatenate([at_s, rt], axis=0),
                     jnp.concatenate([stack(k * e_neg), stack(b * e_neg)], axis=0))
        pr["low"] = -g[:C2, C2:]
        pr["rb"] = jnp.where(incl_pair, g[C2:, C2:], 0.0).astype(BF16)
        pr["rk"] = jnp.where(incl_pair, g[C2:, :C2], 0.0).astype(BF16)
        pr["ak"] = jnp.where(strict_pair, g[:C, :C2] + g[C:C2, :C2], 0.0)
    for pr in pairs:
        akv = jnp.dot(pr["ak"].astype(BF16), pr["v_s"], preferred_element_type=F32)
        pr["rhs"] = jnp.concatenate([pr["at_s"], stack(akv)], axis=1).astype(BF16)

    tinvs = _tri_inv_many([pr["low"] for pr in pairs], C2, C, half_rows=False)
    sols = [jnp.dot(t, pr["rhs"], preferred_element_type=F32).astype(BF16) for t, pr in zip(tinvs, pairs)]
    for pr, sol in zip(pairs, sols):
        a2_s = sol[:, :LANES]
        u2_s = sol[:, LANES:]
        d = functools.partial(jnp.dot, preferred_element_type=F32)
        pr["r2"] = pr["rt"] + d(pr["rb"], a2_s)
        pr["y2"] = d(pr["rk"], pr["v_s"]) + d(pr["rb"], u2_s)
        pr["w2"] = _bdot_tn(a2_s, pr["bh_s"])
        pr["n2"] = _bdot_tn(jnp.concatenate([pr["v_s"], u2_s], axis=0),
                            jnp.concatenate([pr["kh_s"].astype(BF16), pr["bh_s"]], axis=0))

    for pr in pairs:
        rr, p = pr["rr"], pr["p"]
        st = s_scr[rr, p] if carried else pair_state(rr, p)
        y_ref[rr, pr["rows"], pr["sl"]] = _bdot_nt(pr["r2"], st) + pr["y2"]
        new = st * pr["eg"] + (_bdot(st, pr["w2"]) + pr["n2"])
        if carried:
            s_scr[rr, p] = new
        else:
            put_state(rr, p, new)

    if carried:
        @pl.when(step == n_steps - 1)
        def _():
            for rr in range(gb):
                for p in range(RWKV_PAIRS):
                    put_state(rr, p, s_scr[rr, p])


def _rwkv_rec(r, k, v, ld, kn, bvec, s0, chunk, gb, gt):
    b, t, _ = r.shape
    span = gt * chunk
    n_steps = t // span
    assert b % gb == 0 and t % span == 0 and (gb == 1 or n_steps == 1)
    kern = functools.partial(_rwkv_rec_kernel, chunk=chunk, gb=gb, gt=gt, n_steps=n_steps)
    tok = pl.BlockSpec((gb, span, D_MODEL), lambda i, c: (i, c, 0))
    st = pl.BlockSpec((gb, RWKV_HEADS, RWKV_HEAD, RWKV_HEAD), lambda i, c: (i, 0, 0, 0))
    return pl.pallas_call(
        kern,
        grid=(b // gb, n_steps),
        in_specs=[tok] * 6 + [st],
        out_specs=[tok, st],
        out_shape=[
            jax.ShapeDtypeStruct((b, t, D_MODEL), F32),
            jax.ShapeDtypeStruct((b, RWKV_HEADS, RWKV_HEAD, RWKV_HEAD), F32),
        ],
        scratch_shapes=[pltpu.VMEM((gb if n_steps * gt > 1 else 1, RWKV_PAIRS, LANES, LANES), F32)],
        compiler_params=pltpu.CompilerParams(
            dimension_semantics=("parallel", "arbitrary"), vmem_limit_bytes=VMEM_LIMIT_BYTES),
        name="rwkv_rec",
    )(r, k, v, ld, kn, bvec, s0)


def _rwkv_out_kernel(y_ref, z_ref, bonus_ref, x_ref, lw_ref, lb_ref, avg_ref, w_ref, fw_ref, out_ref):
    y = y_ref[...]
    avg = avg_ref[...]
    mean = _head_sum(y, avg)
    yc = y - mean
    var = _head_sum(yc * yc, avg)
    gn = yc * lax.rsqrt(var + GN_EPS) * lw_ref[...] + lb_ref[...]
    o = (gn + bonus_ref[...]) * _silu(z_ref[...])
    x2 = x_ref[...] + _bdot(o, w_ref[...])
    out_ref[...] = _rms(x2, fw_ref[...])


def _rwkv_out(y2d, z2d, bonus2d, x2d, lnx_w, lnx_b, avg_bd, w_o, fw, tm):
    m = x2d.shape[0]
    tok = pl.BlockSpec((tm, D_MODEL), lambda i: (i, 0))
    row = pl.BlockSpec((1, D_MODEL), lambda i: (0, 0))
    return pl.pallas_call(
        _rwkv_out_kernel,
        grid=(m // tm,),
        in_specs=[tok, tok, tok, tok, row, row,
                  pl.BlockSpec((LANES, LANES), lambda i: (0, 0)),
                  pl.BlockSpec((D_MODEL, D_MODEL), lambda i: (0, 0)), row],
        out_specs=tok,
        out_shape=jax.ShapeDtypeStruct((m, D_MODEL), F32),
        compiler_params=pltpu.CompilerParams(
            dimension_semantics=("parallel",), vmem_limit_bytes=VMEM_LIMIT_BYTES),
        name="rwkv_out",
    )(y2d, z2d, bonus2d, x2d, lnx_w, lnx_b, avg_bd, w_o, fw)


def _pick_tile(m, cap):
    t = min(m, cap)
    while m % t:
        t //= 2
    return t


def _step_groups(b, t, chunk, chunks_per_step):
    n_chunks = t // chunk
    if n_chunks > 1:
        return 1, _pick_tile(n_chunks, chunks_per_step)
    return _pick_tile(b, ROWS_PER_STEP), 1


def _trunk(x, gdn_s, gdn_conv, rwkv_s, rwkv_shift, p, gdn_chunk, rwkv_chunk):
    b, t, _ = x.shape
    m = b * t
    x2d = x.reshape(m, D_MODEL)
    prev8 = jnp.pad(gdn_conv, ((0, 0), (SUBLANES - (CONV_W - 1), 0), (0, 0)))
    act, ba, last8 = _gdn_in(x2d, p["nw0"], p["w_in"], p["w_ba"], prev8, p["conv_wt"], t)
    conv_new = last8.reshape(b, -1, SUBLANES, GDN_CONV_DIM)[:, -1, SUBLANES - (CONV_W - 1):, :]
    o, s_gdn = _gdn_chunk(act.reshape(b, t, GDN_QKVZ_DIM), ba.reshape(b, t, LANES), gdn_s,
                          p["alog_row"], p["dtb_row"], gdn_chunk, *_step_groups(b, t, gdn_chunk, GDN_CHUNKS_PER_STEP))
    tm2 = _pick_tile(m, 512)
    x1 = _gdn_out(o.reshape(m, GDN_VALUE_DIM), act, x2d, p["gn_w"], p["w_out"], tm2)
    if t >= 256:
        tb, tt = 1, 256
    elif t * b <= 256:
        tb, tt = b, t
    else:
        tb, tt = 256 // t, t
    r, k, v, ld, kn, bvec, z, bonus, last = _rwkv_in(
        x1.reshape(b, t, D_MODEL), rwkv_shift.reshape(b, 1, D_MODEL), p["nw1"], p["mu"],
        p["wr"], p["wk"], p["wv"], p["wz"], p["w0"], p["w1"], p["w2"], p["a0"], p["a1"], p["a2"],
        p["k_k"], p["k_a"], p["r_k"], p["ones_bd"], tb, tt)
    sh3 = lambda a: a.reshape(b, t, D_MODEL)
    y, s_rwkv = _rwkv_rec(sh3(r), sh3(k), sh3(v), sh3(ld), sh3(kn), sh3(bvec), rwkv_s,
                          rwkv_chunk, *_step_groups(b, t, rwkv_chunk, RWKV_CHUNKS_PER_STEP))
    out = _rwkv_out(y.reshape(m, D_MODEL), z, bonus, x1, p["lnx_w"], p["lnx_b"], p["avg_bd"], p["w_o"], p["fw"], tm2)
    return (out.reshape(b, t, D_MODEL), s_gdn, conv_new, s_rwkv, last.reshape(b, D_MODEL))


def kernel(x_prompt, x_sample, state_gdn, state_gdn_conv, state_rwkv, state_rwkv_shift, meta_tokens, norm_w, final_norm_w, gdn_w_in, gdn_conv_w, gdn_a_log, gdn_dt_bias, gdn_norm_w, gdn_w_out, rwkv_mu, rwkv_w_rkvz, rwkv_w0, rwkv_w1, rwkv_w2, rwkv_a0, rwkv_a1, rwkv_a2, rwkv_k_k, rwkv_k_a, rwkv_r_k, rwkv_lnx_w, rwkv_lnx_b, rwkv_w_o):
    assert norm_w.shape[0] == 2 and gdn_w_in.shape[0] == 1 and rwkv_mu.shape[0] == 1
    row = lambda a: a.reshape(1, -1).astype(F32)
    w_in = gdn_w_in[0]
    w_ba = jnp.pad(w_in[:, GDN_QKVZ_DIM:], ((0, 0), (0, LANES - 2 * GDN_V_HEADS)))
    gate_row = lambda a: jnp.pad(a.astype(F32), (GDN_V_HEADS, LANES - 2 * GDN_V_HEADS)).reshape(1, LANES)
    rb, cb = jnp.arange(LANES)[:, None] // RWKV_HEAD, jnp.arange(LANES)[None, :] // RWKV_HEAD
    bd = (rb == cb)
    p = {
        "nw0": row(norm_w[0]), "nw1": row(norm_w[1]), "fw": row(final_norm_w),
        "w_in": w_in.astype(BF16), "w_ba": w_ba.astype(BF16),
        "conv_wt": gdn_conv_w[0].T.astype(F32),
        "alog_row": gate_row(gdn_a_log[0]), "dtb_row": gate_row(gdn_dt_bias[0]),
        "gn_w": row(gdn_norm_w[0]), "w_out": gdn_w_out[0].astype(BF16),
        "mu": rwkv_mu[0].astype(F32),
        "wr": rwkv_w_rkvz[0, 0].astype(BF16), "wk": rwkv_w_rkvz[0, 1].astype(BF16),
        "wv": rwkv_w_rkvz[0, 2].astype(BF16), "wz": rwkv_w_rkvz[0, 3].astype(BF16),
        "w0": row(rwkv_w0[0]), "w1": rwkv_w1[0].astype(BF16), "w2": rwkv_w2[0].astype(BF16),
        "a0": row(rwkv_a0[0]), "a1": rwkv_a1[0].astype(BF16), "a2": rwkv_a2[0].astype(BF16),
        "k_k": row(rwkv_k_k[0]), "k_a": row(rwkv_k_a[0]), "r_k": row(rwkv_r_k[0]),
        "lnx_w": row(rwkv_lnx_w[0]), "lnx_b": row(rwkv_lnx_b[0]), "w_o": rwkv_w_o[0].astype(BF16),
        "ones_bd": bd.astype(BF16), "avg_bd": (bd.astype(F32) / RWKV_HEAD).astype(BF16),
    }
    bp = x_prompt.shape[0]
    zeros = lambda *s: jnp.zeros(s, F32)
    _, m_gdn, m_conv, m_rwkv, m_shift = _trunk(
        meta_tokens.astype(F32)[None], zeros(1, GDN_V_HEADS, GDN_HEAD, GDN_HEAD), zeros(1, CONV_W - 1, GDN_CONV_DIM),
        zeros(1, RWKV_HEADS, RWKV_HEAD, RWKV_HEAD), zeros(1, D_MODEL), p, N_META, N_META)
    rep = lambda a: jnp.broadcast_to(a, (bp,) + a.shape[1:])
    y_p, p_gdn, p_conv, p_rwkv, p_shift = _trunk(
        x_prompt, rep(m_gdn), rep(m_conv), rep(m_rwkv), rep(m_shift), p, GDN_CHUNK, RWKV_CHUNK)
    ts = x_sample.shape[1]
    y_s, s_gdn, s_conv, s_rwkv, s_shift = _trunk(
        x_sample, state_gdn[0], state_gdn_conv[0], state_rwkv[0], state_rwkv_shift[0], p, ts, ts)
    return (y_p, y_s, p_gdn[None], p_conv[None], p_rwkv[None], p_shift[None],
            s_gdn[None], s_conv[None], s_rwkv[None], s_shift[None])
```

```python
import functools

import jax
import jax.numpy as jnp
from jax import lax
from jax.experimental import pallas as pl
from jax.experimental.pallas import tpu as pltpu

F32 = jnp.float32
BF16 = jnp.bfloat16

D_MODEL = 1024
N_META = 16
GDN_QK_HEADS = 8
GDN_V_HEADS = 16
GDN_HEAD = 128
GDN_KEY_DIM = GDN_QK_HEADS * GDN_HEAD
GDN_VALUE_DIM = GDN_V_HEADS * GDN_HEAD
GDN_CONV_DIM = 2 * GDN_KEY_DIM + GDN_VALUE_DIM
GDN_QKVZ_DIM = GDN_CONV_DIM + GDN_VALUE_DIM
CONV_W = 4
GDN_TN = 1024
GDN_CONV_TILES = GDN_CONV_DIM // GDN_TN
GDN_SLAB = 256
GDN_ROWS = 128
GDN_TM = 1024
RWKV_SLAB = 256
RWKV_TM = 256
GDN_CHUNK = 64
RWKV_HEAD = 64
RWKV_HEADS = D_MODEL // RWKV_HEAD
RWKV_PAIRS = RWKV_HEADS // 2
RWKV_CHUNK = 64
LORA = 64
RMS_EPS = 1e-6
L2_EPS = 1e-6
GN_EPS = 64e-5
DECAY_SCALE = 0.6065306597126334
LOG2_E = 1.4426950408889634
LANES = 128
SUBLANES = 8
BF16_ROWS = 16
VMEM_LIMIT_BYTES = 56 * 1024 * 1024
GDN_CHUNKS_PER_STEP = 4
RWKV_CHUNKS_PER_STEP = 2
ROWS_PER_STEP = 8


def _bdot(a, b):
    return jnp.dot(a.astype(BF16), b.astype(BF16), preferred_element_type=F32)


def _bdot_nt(a, b):
    return lax.dot_general(a.astype(BF16), b.astype(BF16), (((1,), (1,)), ((), ())), preferred_element_type=F32)


def _bdot_tn(a, b):
    return lax.dot_general(a.astype(BF16), b.astype(BF16), (((0,), (0,)), ((), ())), preferred_element_type=F32)


def _split3(a):
    hi = a.astype(BF16).astype(F32)
    r1 = a - hi
    mid = r1.astype(BF16).astype(F32)
    return jnp.concatenate([hi, mid, r1 - mid], axis=0).astype(BF16)


def _tril3(n_out, c):
    r, k = _iota2(n_out, 3 * c)
    k = jnp.where(k >= 2 * c, k - 2 * c, jnp.where(k >= c, k - c, k))
    return jnp.where((r & (c - 1)) >= k, 1.0, 0.0).astype(BF16)


def _cumsum_rows(m3, x, dims):
    x3 = _split3(x)
    if dims == "nn":
        return jnp.dot(m3, x3, preferred_element_type=F32)
    return lax.dot_general(x3, m3, (((0,), (1,)), ((), ())), preferred_element_type=F32)


def _rms(x, w):
    return x * lax.rsqrt(jnp.mean(x * x, axis=-1, keepdims=True) + RMS_EPS) * w


def _sigmoid(x):
    return 1.0 / (1.0 + jnp.exp2(x * -LOG2_E))


def _silu(x):
    return x * _sigmoid(x)


def _softplus(x):
    return jnp.maximum(x, 0.0) + jnp.log1p(jnp.exp(-jnp.abs(x)))


def _iota2(n, m):
    return lax.broadcasted_iota(jnp.int32, (n, m), 0), lax.broadcasted_iota(jnp.int32, (n, m), 1)


def _log2(n):
    l = n.bit_length() - 1
    assert (1 << l) == n, n
    return l


def _tri_inv_many(lows, n, blk, half_rows):
    row, col = _iota2(n, n)
    eye = jnp.where(row == col, 1.0, 0.0).astype(BF16)
    negs = [(-low).astype(BF16) for low in lows]
    d = lambda a, b: jnp.dot(a, b, preferred_element_type=F32).astype(BF16)
    s = 1
    ts = None
    while s < blk:
        sh = _log2(s)
        if half_rows and s >= BF16_ROWS:
            hr, hc = _iota2(n // 2, n)
            row_blk = ((hr >> sh) << 1) + 1
            m = (row_blk >> 1 == (hc >> sh) >> 1) & (((hc >> sh) & 1) == 0)
            lower = lambda t: jnp.concatenate([t[(2 * j + 1) * s:(2 * j + 2) * s] for j in range(n // (2 * s))], axis=0)
            tl = [lower(t) for t in ts]
            xs = [d(t, ng) for t, ng in zip(tl, negs)]
            zs = [d(x, t) for x, t in zip(xs, ts)]
            new = [jnp.where(m, z, t) for z, t in zip(zs, tl)]
            ts = [jnp.concatenate([piece for j in range(n // (2 * s))
                                   for piece in (t[2 * j * s:(2 * j + 1) * s], nw[j * s:(j + 1) * s])], axis=0)
                  for t, nw in zip(ts, new)]
        else:
            sub_r = row >> sh
            sub_c = col >> sh
            m = ((sub_r >> 1) == (sub_c >> 1)) & ((sub_r & 1) == 1) & ((sub_c & 1) == 0)
            if ts is None:
                ts = [jnp.where(m, ng, eye) for ng in negs]
            else:
                xs = [d(t, ng) for t, ng in zip(ts, negs)]
                zs = [d(x, t) for x, t in zip(xs, ts)]
                ts = [jnp.where(m, z, t) for z, t in zip(zs, ts)]
        s *= 2
    return [eye for _ in lows] if ts is None else ts


def _gdn_in_kernel(x_ref, nw_ref, w_ref, wba_ref, prev_ref, cw_ref, act_ref, ba_ref, last_ref, xn_scr, tail_scr, u_scr,
                   *, tm, groups, tiles_per_row):
    i = pl.program_id(0)
    n = pl.program_id(1)

    @pl.when(n == 0)
    def _():
        xb = _rms(x_ref[...], nw_ref[...]).astype(BF16)
        xn_scr[...] = xb
        ba_ref[...] = jnp.dot(xb, wba_ref[...], preferred_element_type=F32)

    @pl.when((n == 0) & (i == 0))
    def _():
        tail_scr[...] = jnp.zeros(tail_scr.shape, F32)

    tn = w_ref.shape[1]
    n_slabs = tn // GDN_SLAB
    col = lambda c: slice(c * GDN_SLAB, (c + 1) * GDN_SLAB)

    def slab(c):
        return jnp.dot(xn_scr[...], w_ref[:, col(c)], preferred_element_type=F32)

    def l2n(a, scale):
        parts = []
        for h in range(GDN_SLAB // GDN_HEAD):
            ah = a[:, h * GDN_HEAD:(h + 1) * GDN_HEAD]
            parts.append(ah * (lax.rsqrt(jnp.sum(ah * ah, axis=-1, keepdims=True) + L2_EPS) * scale))
        return jnp.concatenate(parts, axis=1)

    def run_conv(scale):
        if groups == 1:
            def fill(c):
                u_scr[c % 2, SUBLANES:SUBLANES + tm, :] = slab(c)
            fill(0)
            for c in range(n_slabs):
                cs, buf = col(c), c % 2
                if c + 1 < n_slabs:
                    fill(c + 1)
                u_scr[buf, 0:SUBLANES, :] = jnp.where((i % tiles_per_row) == 0, prev_ref[0, :, cs], tail_scr[n, :, cs])
                end = u_scr[buf, tm:tm + SUBLANES, :]
                tail_scr[n, :, cs] = end
                last_ref[0, :, cs] = end
                nr = min(GDN_ROWS, tm)
                for rb in range(tm // nr):
                    r0 = SUBLANES + rb * nr
                    y = u_scr[buf, r0:r0 + nr, :] * cw_ref[CONV_W - 1:CONV_W, cs]
                    for s in range(1, CONV_W):
                        y = y + u_scr[buf, r0 - s:r0 - s + nr, :] * cw_ref[CONV_W - 1 - s:CONV_W - s, cs]
                    a = _silu(y)
                    if scale is not None:
                        a = l2n(a, scale)
                    act_ref[rb * nr:(rb + 1) * nr, cs] = a.astype(BF16)
        else:
            rows = lax.broadcasted_iota(jnp.int32, (tm, 1), 0)
            t_in = rows & (SUBLANES - 1)
            u_next = slab(0)
            for c in range(n_slabs):
                cs, u = col(c), u_next
                if c + 1 < n_slabs:
                    u_next = slab(c + 1)
                prev = prev_ref[:, :, cs].reshape(tm, GDN_SLAB)
                y = u * cw_ref[CONV_W - 1:CONV_W, cs]
                for s in range(1, CONV_W):
                    shifted = jnp.where(t_in < s, pltpu.roll(prev, tm - SUBLANES + s, axis=0), pltpu.roll(u, s, axis=0))
                    y = y + shifted * cw_ref[CONV_W - 1 - s:CONV_W - s, cs]
                last_ref[:, :, cs] = u.reshape(groups, SUBLANES, GDN_SLAB)
                a = _silu(y)
                if scale is not None:
                    a = l2n(a, scale)
                act_ref[:, cs] = a.astype(BF16)

    @pl.when(n < 2)
    def _():
        run_conv(jnp.where(n == 0, GDN_HEAD ** -0.5, 1.0))

    @pl.when((n >= 2) & (n < GDN_CONV_TILES))
    def _():
        run_conv(None)

    @pl.when(n >= GDN_CONV_TILES)
    def _():
        for c in range(n_slabs):
            act_ref[:, col(c)] = _silu(slab(c)).astype(BF16)


def _gdn_in(x2d, nw, w_in, w_ba, prev8, conv_wt, t):
    m = x2d.shape[0]
    b = m // t
    tn = GDN_TN
    conv_tile = lambda j: jnp.minimum(j, GDN_CONV_TILES - 1)
    if t == SUBLANES:
        tm = _pick_tile(m, GDN_TM)
        groups, tiles_per_row = tm // SUBLANES, 1
        before = pl.BlockSpec((groups, SUBLANES, tn), lambda i, j: (i, 0, conv_tile(j)))
    else:
        tm = _pick_tile(t, GDN_TM)
        groups, tiles_per_row = 1, t // tm
        assert tm % SUBLANES == 0 and tm > SUBLANES
        before = pl.BlockSpec((1, SUBLANES, tn), lambda i, j: (i // tiles_per_row, 0, conv_tile(j)))
    after = pl.BlockSpec((groups, SUBLANES, tn), lambda i, j: (i, 0, conv_tile(j)))
    kern = functools.partial(_gdn_in_kernel, tm=tm, groups=groups, tiles_per_row=tiles_per_row)
    return pl.pallas_call(
        kern,
        grid=(m // tm, GDN_QKVZ_DIM // tn),
        in_specs=[
            pl.BlockSpec((tm, D_MODEL), lambda i, j: (i, 0)),
            pl.BlockSpec((1, D_MODEL), lambda i, j: (0, 0)),
            pl.BlockSpec((D_MODEL, tn), lambda i, j: (0, j)),
            pl.BlockSpec((D_MODEL, LANES), lambda i, j: (0, 0)),
            before,
            pl.BlockSpec((CONV_W, tn), lambda i, j: (0, conv_tile(j))),
        ],
        out_specs=[
            pl.BlockSpec((tm, tn), lambda i, j: (i, j)),
            pl.BlockSpec((tm, LANES), lambda i, j: (i, 0)),
            after,
        ],
        out_shape=[
            jax.ShapeDtypeStruct((m, GDN_QKVZ_DIM), BF16),
            jax.ShapeDtypeStruct((m, LANES), F32),
            jax.ShapeDtypeStruct((m // tm * groups, SUBLANES, GDN_CONV_DIM), F32),
        ],
        scratch_shapes=[pltpu.VMEM((tm, D_MODEL), BF16), pltpu.VMEM((GDN_CONV_TILES, SUBLANES, tn), F32),
                        pltpu.VMEM((2, tm + SUBLANES, GDN_SLAB), F32)],
        compiler_params=pltpu.CompilerParams(
            dimension_semantics=("arbitrary", "arbitrary"), vmem_limit_bytes=VMEM_LIMIT_BYTES),
        name="gdn_in",
    )(x2d, nw, w_in, w_ba, prev8, conv_wt)


def _gdn_chunk_kernel(act_ref, ba_ref, s0_ref, alog_ref, dtb_ref, o_ref, sout_ref, s_scr, *, chunk, gb, gt, n_steps):
    step = pl.program_id(1)
    C = chunk
    C2 = 2 * C
    carried = n_steps * gt > 1

    if carried:
        @pl.when(step == 0)
        def _():
            s_scr[...] = s0_ref[...]

    tril = _tril3(C, C)
    tril2 = _tril3(C2, C)
    row, col = _iota2(C2, C2)
    causal = ((row >= C) == (col >= C)) & (row >= col)
    top = lax.broadcasted_iota(jnp.int32, (C2, 1), 0) < C
    left = lax.broadcasted_iota(jnp.int32, (1, C2), 1) < C

    pairs = []
    for r in range(gb):
        ba = ba_ref[r]
        beta_l = _sigmoid(ba)
        g_l = -jnp.exp(alog_ref[...]) * _softplus(ba + dtb_ref[...])
        for j in range(gt):
            rows = slice(j * C, (j + 1) * C)
            beta_all = beta_l[rows]
            gc_all = _cumsum_rows(tril, g_l[rows], "nn")
            gct_all = _cumsum_rows(tril2, g_l[rows], "tn")
            for qh in range(GDN_QK_HEADS):
                q = act_ref[r, rows, qh * GDN_HEAD:(qh + 1) * GDN_HEAD].astype(F32)
                k = act_ref[r, rows, GDN_KEY_DIM + qh * GDN_HEAD:GDN_KEY_DIM + (qh + 1) * GDN_HEAD].astype(F32)
                h0 = 2 * qh
                h1 = h0 + 1
                a0, a1 = GDN_V_HEADS + h0, GDN_V_HEADS + h1
                beta_s = jnp.concatenate([beta_all[:, h0:h0 + 1], beta_all[:, h1:h1 + 1]], axis=0)
                gc_s = jnp.concatenate([gc_all[:, a0:a0 + 1], gc_all[:, a1:a1 + 1]], axis=0)
                gc_row = jnp.where(left, gct_all[a0:a0 + 1, :], gct_all[a1:a1 + 1, :])
                glast0 = gc_all[C - 1:C, a0:a0 + 1]
                glast1 = gc_all[C - 1:C, a1:a1 + 1]
                glast_s = jnp.where(top, glast0, glast1)
                decay = jnp.exp(jnp.where(causal, gc_s - gc_row, -jnp.inf))
                k_s = jnp.concatenate([k, k], axis=0)
                q_s = jnp.concatenate([q, q], axis=0)
                v0 = 2 * GDN_KEY_DIM + h0 * GDN_HEAD
                v_s = jnp.concatenate([act_ref[r, rows, v0:v0 + GDN_HEAD],
                                       act_ref[r, rows, v0 + GDN_HEAD:v0 + 2 * GDN_HEAD]], axis=0).astype(F32)
                egc = jnp.exp(gc_s)
                kb_s = k_s * beta_s
                pairs.append(dict(
                    r=r, j=j, h0=h0, h1=h1,
                    low=_bdot_nt(kb_s, k_s) * decay,
                    rhs=jnp.concatenate([v_s * beta_s, kb_s * egc], axis=1).astype(BF16),
                    attn=_bdot_nt(q_s, k_s) * decay, qd_s=q_s * egc, kd_s=k_s * jnp.exp(glast_s - gc_s),
                    eg0=jnp.exp(glast0), eg1=jnp.exp(glast1)))

    tinvs = _tri_inv_many([pr["low"] for pr in pairs], C2, C, half_rows=True)
    sols = [jnp.dot(t, pr["rhs"], preferred_element_type=F32) for t, pr in zip(tinvs, pairs)]

    def state_in(r, h):
        return s_scr[r, h] if carried else s0_ref[r, h]

    def state_out(r, h, val):
        if carried:
            s_scr[r, h] = val
        else:
            sout_ref[r, h] = val

    for r in range(gb):
        for j in range(gt):
            grp = [(pr, sol) for pr, sol in zip(pairs, sols) if pr["r"] == r and pr["j"] == j]
            rows = slice(j * C, (j + 1) * C)
            sts, wqs = [], []
            for pr, sol in grp:
                w_s = sol[:, GDN_HEAD:]
                st0 = state_in(r, pr["h0"])
                st1 = state_in(r, pr["h1"])
                sts.append((st0, st1))
                wqs.append((_bdot(jnp.concatenate([w_s[:C], pr["qd_s"][:C]], axis=0), st0),
                            _bdot(jnp.concatenate([w_s[C:], pr["qd_s"][C:]], axis=0), st1)))
            for (pr, sol), (st0, st1), (wq0, wq1) in zip(grp, sts, wqs):
                h0, h1 = pr["h0"], pr["h1"]
                u_s = sol[:, :GDN_HEAD]
                vn0 = u_s[:C] - wq0[:C]
                vn1 = u_s[C:] - wq1[:C]
                o_s = _bdot(pr["attn"], jnp.concatenate([vn0, vn1], axis=0))
                o_ref[r, rows, h0 * GDN_HEAD:(h0 + 1) * GDN_HEAD] = (o_s[:C] + wq0[C:]).astype(BF16)
                o_ref[r, rows, h1 * GDN_HEAD:(h1 + 1) * GDN_HEAD] = (o_s[C:] + wq1[C:]).astype(BF16)
                state_out(r, h0, st0 * pr["eg0"] + _bdot_tn(pr["kd_s"][:C], vn0))
                state_out(r, h1, st1 * pr["eg1"] + _bdot_tn(pr["kd_s"][C:], vn1))

    if carried:
        @pl.when(step == n_steps - 1)
        def _():
            sout_ref[...] = s_scr[...]


def _gdn_chunk(act, ba, s0, alog_row, dtb_row, chunk, gb, gt):
    b, t, _ = act.shape
    span = gt * chunk
    n_steps = t // span
    assert b % gb == 0 and t % span == 0 and (gb == 1 or n_steps == 1)
    kern = functools.partial(_gdn_chunk_kernel, chunk=chunk, gb=gb, gt=gt, n_steps=n_steps)
    state = pl.BlockSpec((gb, GDN_V_HEADS, GDN_HEAD, GDN_HEAD), lambda i, c: (i, 0, 0, 0))
    return pl.pallas_call(
        kern,
        grid=(b // gb, n_steps),
        in_specs=[
            pl.BlockSpec((gb, span, GDN_CONV_DIM), lambda i, c: (i, c, 0)),
            pl.BlockSpec((gb, span, LANES), lambda i, c: (i, c, 0)),
            state,
            pl.BlockSpec((1, LANES), lambda i, c: (0, 0)),
            pl.BlockSpec((1, LANES), lambda i, c: (0, 0)),
        ],
        out_specs=[pl.BlockSpec((gb, span, GDN_VALUE_DIM), lambda i, c: (i, c, 0)), state],
        out_shape=[
            jax.ShapeDtypeStruct((b, t, GDN_VALUE_DIM), BF16),
            jax.ShapeDtypeStruct((b, GDN_V_HEADS, GDN_HEAD, GDN_HEAD), F32),
        ],
        scratch_shapes=[pltpu.VMEM((gb if n_steps * gt > 1 else 1, GDN_V_HEADS, GDN_HEAD, GDN_HEAD), F32)],
        compiler_params=pltpu.CompilerParams(
            dimension_semantics=("parallel", "arbitrary"), vmem_limit_bytes=VMEM_LIMIT_BYTES),
        name="gdn_chunk",
    )(act, ba, s0, alog_row, dtb_row)


def _gdn_out_kernel(o_ref, gate_ref, x_ref, gw_ref, w_ref, y_ref):
    gw = gw_ref[...]
    parts = []
    for h in range(GDN_V_HEADS):
        sl = slice(h * GDN_HEAD, (h + 1) * GDN_HEAD)
        parts.append((_rms(o_ref[:, sl].astype(F32), gw) * gate_ref[:, sl].astype(F32)).astype(BF16))
    gated = jnp.concatenate(parts, axis=1)
    y_ref[...] = x_ref[...] + jnp.dot(gated, w_ref[...], preferred_element_type=F32)


def _gdn_out(o2d, act2d, x2d, gw, w_out, tm):
    m = x2d.shape[0]
    z_block = GDN_CONV_DIM // GDN_VALUE_DIM
    return pl.pallas_call(
        _gdn_out_kernel,
        grid=(m // tm,),
        in_specs=[
            pl.BlockSpec((tm, GDN_VALUE_DIM), lambda i: (i, 0)),
            pl.BlockSpec((tm, GDN_VALUE_DIM), lambda i: (i, z_block)),
            pl.BlockSpec((tm, D_MODEL), lambda i: (i, 0)),
            pl.BlockSpec((1, GDN_HEAD), lambda i: (0, 0)),
            pl.BlockSpec((GDN_VALUE_DIM, D_MODEL), lambda i: (0, 0)),
        ],
        out_specs=pl.BlockSpec((tm, D_MODEL), lambda i: (i, 0)),
        out_shape=jax.ShapeDtypeStruct((m, D_MODEL), F32),
        compiler_params=pltpu.CompilerParams(
            dimension_semantics=("parallel",), vmem_limit_bytes=VMEM_LIMIT_BYTES),
        name="gdn_out",
    )(o2d, act2d, x2d, gw, w_out)


def _head_sum(x, ones_bd):
    return jnp.concatenate([_bdot(x[:, p * LANES:(p + 1) * LANES], ones_bd) for p in range(RWKV_PAIRS)], axis=1)


def _rwkv_in_kernel(x_ref, xp_ref, sh_ref, nw_ref, mu_ref, wr_ref, wk_ref, wv_ref, wz_ref,
                    w0_ref, w1_ref, w2_ref, a0_ref, a1_ref, a2_ref, kk_ref, ka_ref, rk_ref, ones_ref,
                    r_out, k_out, v_out, ld_out, kn_out, b_out, z_out, bonus_out, last_out, *, tb, tt, n_t):
    t = pl.program_id(1)
    m = tb * tt
    nw = nw_ref[...]
    xn = _rms(x_ref[...].reshape(m, D_MODEL), nw)
    prev_last = _rms(xp_ref[:, SUBLANES - 1:SUBLANES, :], nw)
    first = jnp.where(t == 0, sh_ref[...], prev_last)
    first = jnp.broadcast_to(first, (tb, tt, D_MODEL)).reshape(m, D_MODEL)
    rolled = pltpu.roll(xn, 1, axis=0)
    rows = lax.broadcasted_iota(jnp.int32, (m, 1), 0)
    xprev = jnp.where((rows & (tt - 1)) == 0, first, rolled)
    xx = xprev - xn
    mu = mu_ref[...]

    mixed = [(xn + xx * mu[i:i + 1, :]).astype(BF16) for i in range(6)]
    w_mid = jnp.tanh(jnp.dot(mixed[4], w1_ref[...], preferred_element_type=F32)).astype(BF16)
    a_mid = jnp.dot(mixed[5], a1_ref[...], preferred_element_type=F32).astype(BF16)
    ones_bd = ones_ref[...]
    d = functools.partial(jnp.dot, preferred_element_type=F32)

    def project(cs):
        return (d(mixed[0], wr_ref[:, cs]), d(mixed[1], wk_ref[:, cs]), d(mixed[2], wv_ref[:, cs]),
                d(mixed[3], wz_ref[:, cs]), d(w_mid, w2_ref[:, cs]), d(a_mid, a2_ref[:, cs]))

    def head_sum(x):
        return jnp.concatenate([_bdot(x[:, p * LANES:(p + 1) * LANES], ones_bd) for p in range(RWKV_SLAB // LANES)], axis=1)

    n_slabs = D_MODEL // RWKV_SLAB
    col = lambda c: slice(c * RWKV_SLAB, (c + 1) * RWKV_SLAB)
    nxt = project(col(0))
    for c in range(n_slabs):
        cs = col(c)
        r, k, v, z, w_lo, a_lo = nxt
        if c + 1 < n_slabs:
            nxt = project(col(c + 1))
        a = _sigmoid(a0_ref[:, cs] + a_lo)
        kk = k * kk_ref[:, cs]
        kn = kk * lax.rsqrt(head_sum(kk * kk) + L2_EPS)
        k = k * (1.0 + (a - 1.0) * ka_ref[:, cs])
        r_out[:, cs] = r
        k_out[:, cs] = k
        v_out[:, cs] = v
        z_out[:, cs] = z
        ld_out[:, cs] = -DECAY_SCALE * _sigmoid(w0_ref[:, cs] + w_lo)
        kn_out[:, cs] = kn
        b_out[:, cs] = kn * a
        bonus_out[:, cs] = head_sum(r * k * rk_ref[:, cs]) * v

    @pl.when(t == n_t - 1)
    def _():
        last_out[...] = _rms(x_ref[:, tt - 1:tt, :], nw)


def _rwkv_in(x3d, shift, nw, mu, wr, wk, wv, wz, w0, w1, w2, a0, a1, a2, k_k, k_a, r_k, ones_bd, tb, tt):
    b, t, _ = x3d.shape
    n_t = t // tt
    tpb = tt // SUBLANES
    kern = functools.partial(_rwkv_in_kernel, tb=tb, tt=tt, n_t=n_t)
    row = lambda n: pl.BlockSpec((n, D_MODEL), lambda i, j: (0, 0))
    big = pl.BlockSpec((D_MODEL, D_MODEL), lambda i, j: (0, 0))
    tok = pl.BlockSpec((tb * tt, D_MODEL), lambda i, j: (i * n_t + j, 0))
    m = b * t
    outs = pl.pallas_call(
        kern,
        grid=(b // tb, n_t),
        in_specs=[
            pl.BlockSpec((tb, tt, D_MODEL), lambda i, j: (i, j, 0)),
            pl.BlockSpec((tb, SUBLANES, D_MODEL), lambda i, j: (i, jnp.maximum(j * tpb - 1, 0), 0)),
            pl.BlockSpec((tb, 1, D_MODEL), lambda i, j: (i, 0, 0)),
            row(1), row(6), big, big, big, big,
            row(1), pl.BlockSpec((D_MODEL, LORA), lambda i, j: (0, 0)), pl.BlockSpec((LORA, D_MODEL), lambda i, j: (0, 0)),
            row(1), pl.BlockSpec((D_MODEL, LORA), lambda i, j: (0, 0)), pl.BlockSpec((LORA, D_MODEL), lambda i, j: (0, 0)),
            row(1), row(1), row(1),
            pl.BlockSpec((LANES, LANES), lambda i, j: (0, 0)),
        ],
        out_specs=[tok] * 8 + [pl.BlockSpec((tb, 1, D_MODEL), lambda i, j: (i, 0, 0))],
        out_shape=[jax.ShapeDtypeStruct((m, D_MODEL), F32)] * 8 + [jax.ShapeDtypeStruct((b, 1, D_MODEL), F32)],
        compiler_params=pltpu.CompilerParams(
            dimension_semantics=("parallel", "arbitrary"), vmem_limit_bytes=VMEM_LIMIT_BYTES),
        name="rwkv_in",
    )(x3d, x3d, shift, nw, mu, wr, wk, wv, wz, w0, w1, w2, a0, a1, a2, k_k, k_a, r_k, ones_bd)
    return outs


def _rwkv_rec_kernel(r_ref, k_ref, v_ref, ld_ref, kn_ref, b_ref, s0_ref, y_ref, sout_ref, s_scr,
                     *, chunk, gb, gt, n_steps):
    step = pl.program_id(1)
    C = chunk
    C2 = 2 * C
    carried = n_steps * gt > 1

    def pair_state(rr, p):
        z = jnp.zeros((RWKV_HEAD, RWKV_HEAD), F32)
        return jnp.concatenate([jnp.concatenate([s0_ref[rr, 2 * p], z], axis=1),
                                jnp.concatenate([z, s0_ref[rr, 2 * p + 1]], axis=1)], axis=0)

    def put_state(rr, p, val):
        sout_ref[rr, 2 * p] = val[:RWKV_HEAD, :RWKV_HEAD]
        sout_ref[rr, 2 * p + 1] = val[RWKV_HEAD:, RWKV_HEAD:]

    if carried:
        @pl.when(step == 0)
        def _():
            for rr in range(gb):
                for p in range(RWKV_PAIRS):
                    s_scr[rr, p] = pair_state(rr, p)

    tril = _tril3(C, C)
    lane = lax.broadcasted_iota(jnp.int32, (1, LANES), 1)
    m_lo = lane < RWKV_HEAD
    rc, cc = _iota2(C, C2)
    tl = cc & (C - 1)
    strict_pair = rc > tl
    incl_pair = rc >= tl

    def stack(x):
        return jnp.concatenate([jnp.where(m_lo, x, 0.0), jnp.where(m_lo, 0.0, x)], axis=0)

    pairs = []
    for rr in range(gb):
        for j in range(gt):
            rows = slice(j * C, (j + 1) * C)
            for p in range(RWKV_PAIRS):
                sl = slice(p * LANES, (p + 1) * LANES)
                pairs.append(dict(rr=rr, j=j, p=p, rows=rows, sl=sl))
    for pr in pairs:
        pr["gc"] = _cumsum_rows(tril, ld_ref[pr["rr"], pr["rows"], pr["sl"]], "nn")
    for pr in pairs:
        at_ref = (pr["rr"], pr["rows"], pr["sl"])
        gc = pr["gc"]
        k = k_ref[at_ref]
        b = b_ref[at_ref]
        glast = gc[C - 1:C, :]
        e_neg = jnp.exp(-gc)
        e_end = jnp.exp(glast - gc)
        rt = r_ref[at_ref] * jnp.exp(gc)
        at = -kn_ref[at_ref] * jnp.exp(gc - ld_ref[at_ref])
        at_s = stack(at)
        pr.update(rt=rt, at_s=at_s, v_s=stack(v_ref[at_ref]).astype(BF16),
                  kh_s=stack(k * e_end), bh_s=stack(b * e_end).astype(BF16), eg=jnp.exp(glast))
        g = _bdot_nt(jnp.concatenate([at, rt], axis=0),
                     jnp.concatenate([stack(k * e_neg), stack(b * e_neg)], axis=0))
        ab = g[:C, C2:]
        pr["low"] = -jnp.concatenate([ab, ab], axis=0)
        pr["rb"] = jnp.where(incl_pair, g[C:, C2:], 0.0).astype(BF16)
        pr["rk"] = jnp.where(incl_pair, g[C:, :C2], 0.0).astype(BF16)
        pr["ak"] = jnp.where(strict_pair, g[:C, :C2], 0.0)
    for pr in pairs:
        akv = jnp.dot(pr["ak"].astype(BF16), pr["v_s"], preferred_element_type=F32)
        pr["rhs"] = jnp.concatenate([pr["at_s"], stack(akv)], axis=1).astype(BF16)

    tinvs = _tri_inv_many([pr["low"] for pr in pairs], C2, C, half_rows=False)
    sols = [jnp.dot(t, pr["rhs"], preferred_element_type=F32).astype(BF16) for t, pr in zip(tinvs, pairs)]
    for pr, sol in zip(pairs, sols):
        a2_s = sol[:, :LANES]
        u2_s = sol[:, LANES:]
        d = functools.partial(jnp.dot, preferred_element_type=F32)
        pr["r2"] = pr["rt"] + d(pr["rb"], a2_s)
        pr["y2"] = d(pr["rk"], pr["v_s"]) + d(pr["rb"], u2_s)
        pr["w2"] = _bdot_tn(a2_s, pr["bh_s"])
        pr["n2"] = _bdot_tn(jnp.concatenate([pr["v_s"], u2_s], axis=0),
                            jnp.concatenate([pr["kh_s"].astype(BF16), pr["bh_s"]], axis=0))

    for pr in pairs:
        rr, p = pr["rr"], pr["p"]
        st = s_scr[rr, p] if carried else pair_state(rr, p)
        y_ref[rr, pr["rows"], pr["sl"]] = _bdot_nt(pr["r2"], st) + pr["y2"]
        new = st * pr["eg"] + (_bdot(st, pr["w2"]) + pr["n2"])
        if carried:
            s_scr[rr, p] = new
        else:
            put_state(rr, p, new)

    if carried:
        @pl.when(step == n_steps - 1)
        def _():
            for rr in range(gb):
                for p in range(RWKV_PAIRS):
                    put_state(rr, p, s_scr[rr, p])


def _rwkv_rec(r, k, v, ld, kn, bvec, s0, chunk, gb, gt):
    b, t, _ = r.shape
    span = gt * chunk
    n_steps = t // span
    assert b % gb == 0 and t % span == 0 and (gb == 1 or n_steps == 1)
    kern = functools.partial(_rwkv_rec_kernel, chunk=chunk, gb=gb, gt=gt, n_steps=n_steps)
    tok = pl.BlockSpec((gb, span, D_MODEL), lambda i, c: (i, c, 0))
    st = pl.BlockSpec((gb, RWKV_HEADS, RWKV_HEAD, RWKV_HEAD), lambda i, c: (i, 0, 0, 0))
    return pl.pallas_call(
        kern,
        grid=(b // gb, n_steps),
        in_specs=[tok] * 6 + [st],
        out_specs=[tok, st],
        out_shape=[
            jax.ShapeDtypeStruct((b, t, D_MODEL), F32),
            jax.ShapeDtypeStruct((b, RWKV_HEADS, RWKV_HEAD, RWKV_HEAD), F32),
        ],
        scratch_shapes=[pltpu.VMEM((gb if n_steps * gt > 1 else 1, RWKV_PAIRS, LANES, LANES), F32)],
        compiler_params=pltpu.CompilerParams(
            dimension_semantics=("parallel", "arbitrary"), vmem_limit_bytes=VMEM_LIMIT_BYTES),
        name="rwkv_rec",
    )(r, k, v, ld, kn, bvec, s0)


def _rwkv_out_kernel(y_ref, z_ref, bonus_ref, x_ref, lw_ref, lb_ref, avg_ref, w_ref, fw_ref, out_ref):
    y = y_ref[...]
    avg = avg_ref[...]
    mean = _head_sum(y, avg)
    yc = y - mean
    var = _head_sum(yc * yc, avg)
    gn = yc * lax.rsqrt(var + GN_EPS) * lw_ref[...] + lb_ref[...]
    o = (gn + bonus_ref[...]) * _silu(z_ref[...])
    x2 = x_ref[...] + _bdot(o, w_ref[...])
    out_ref[...] = _rms(x2, fw_ref[...])


def _rwkv_out(y2d, z2d, bonus2d, x2d, lnx_w, lnx_b, avg_bd, w_o, fw, tm):
    m = x2d.shape[0]
    tok = pl.BlockSpec((tm, D_MODEL), lambda i: (i, 0))
    row = pl.BlockSpec((1, D_MODEL), lambda i: (0, 0))
    return pl.pallas_call(
        _rwkv_out_kernel,
        grid=(m // tm,),
        in_specs=[tok, tok, tok, tok, row, row,
                  pl.BlockSpec((LANES, LANES), lambda i: (0, 0)),
                  pl.BlockSpec((D_MODEL, D_MODEL), lambda i: (0, 0)), row],
        out_specs=tok,
        out_shape=jax.ShapeDtypeStruct((m, D_MODEL), F32),
        compiler_params=pltpu.CompilerParams(
            dimension_semantics=("parallel",), vmem_limit_bytes=VMEM_LIMIT_BYTES),
        name="rwkv_out",
    )(y2d, z2d, bonus2d, x2d, lnx_w, lnx_b, avg_bd, w_o, fw)


def _pick_tile(m, cap):
    t = min(m, cap)
    while m % t:
        t //= 2
    return t


def _step_groups(b, t, chunk, chunks_per_step):
    n_chunks = t // chunk
    if n_chunks > 1:
        return 1, _pick_tile(n_chunks, chunks_per_step)
    return _pick_tile(b, ROWS_PER_STEP), 1


def _trunk(x, gdn_s, gdn_conv, rwkv_s, rwkv_shift, p, gdn_chunk, rwkv_chunk):
    b, t, _ = x.shape
    m = b * t
    x2d = x.reshape(m, D_MODEL)
    prev8 = jnp.pad(gdn_conv, ((0, 0), (SUBLANES - (CONV_W - 1), 0), (0, 0)))
    act, ba, last8 = _gdn_in(x2d, p["nw0"], p["w_in"], p["w_ba"], prev8, p["conv_wt"], t)
    conv_new = last8.reshape(b, -1, SUBLANES, GDN_CONV_DIM)[:, -1, SUBLANES - (CONV_W - 1):, :]
    o, s_gdn = _gdn_chunk(act.reshape(b, t, GDN_QKVZ_DIM), ba.reshape(b, t, LANES), gdn_s,
                          p["alog_row"], p["dtb_row"], gdn_chunk, *_step_groups(b, t, gdn_chunk, GDN_CHUNKS_PER_STEP))
    tm2 = _pick_tile(m, 512)
    x1 = _gdn_out(o.reshape(m, GDN_VALUE_DIM), act, x2d, p["gn_w"], p["w_out"], tm2)
    if t >= RWKV_TM:
        tb, tt = 1, RWKV_TM
    elif t * b <= RWKV_TM:
        tb, tt = b, t
    else:
        tb, tt = RWKV_TM // t, t
    r, k, v, ld, kn, bvec, z, bonus, last = _rwkv_in(
        x1.reshape(b, t, D_MODEL), rwkv_shift.reshape(b, 1, D_MODEL), p["nw1"], p["mu"],
        p["wr"], p["wk"], p["wv"], p["wz"], p["w0"], p["w1"], p["w2"], p["a0"], p["a1"], p["a2"],
        p["k_k"], p["k_a"], p["r_k"], p["ones_bd"], tb, tt)
    sh3 = lambda a: a.reshape(b, t, D_MODEL)
    y, s_rwkv = _rwkv_rec(sh3(r), sh3(k), sh3(v), sh3(ld), sh3(kn), sh3(bvec), rwkv_s,
                          rwkv_chunk, *_step_groups(b, t, rwkv_chunk, RWKV_CHUNKS_PER_STEP))
    out = _rwkv_out(y.reshape(m, D_MODEL), z, bonus, x1, p["lnx_w"], p["lnx_b"], p["avg_bd"], p["w_o"], p["fw"], tm2)
    return (out.reshape(b, t, D_MODEL), s_gdn, conv_new, s_rwkv, last.reshape(b, D_MODEL))


def kernel(x_prompt, x_sample, state_gdn, state_gdn_conv, state_rwkv, state_rwkv_shift, meta_tokens, norm_w, final_norm_w, gdn_w_in, gdn_conv_w, gdn_a_log, gdn_dt_bias, gdn_norm_w, gdn_w_out, rwkv_mu, rwkv_w_rkvz, rwkv_w0, rwkv_w1, rwkv_w2, rwkv_a0, rwkv_a1, rwkv_a2, rwkv_k_k, rwkv_k_a, rwkv_r_k, rwkv_lnx_w, rwkv_lnx_b, rwkv_w_o):
    assert norm_w.shape[0] == 2 and gdn_w_in.shape[0] == 1 and rwkv_mu.shape[0] == 1
    row = lambda a: a.reshape(1, -1).astype(F32)
    w_in = gdn_w_in[0]
    w_ba = jnp.pad(w_in[:, GDN_QKVZ_DIM:], ((0, 0), (0, LANES - 2 * GDN_V_HEADS)))
    gate_row = lambda a: jnp.pad(a.astype(F32), (GDN_V_HEADS, LANES - 2 * GDN_V_HEADS)).reshape(1, LANES)
    rb, cb = jnp.arange(LANES)[:, None] // RWKV_HEAD, jnp.arange(LANES)[None, :] // RWKV_HEAD
    bd = (rb == cb)
    p = {
        "nw0": row(norm_w[0]), "nw1": row(norm_w[1]), "fw": row(final_norm_w),
        "w_in": w_in.astype(BF16), "w_ba": w_ba.astype(BF16),
        "conv_wt": gdn_conv_w[0].T.astype(F32),
        "alog_row": gate_row(gdn_a_log[0]), "dtb_row": gate_row(gdn_dt_bias[0]),
        "gn_w": row(gdn_norm_w[0]), "w_out": gdn_w_out[0].astype(BF16),
        "mu": rwkv_mu[0].astype(F32),
        "wr": rwkv_w_rkvz[0, 0].astype(BF16), "wk": rwkv_w_rkvz[0, 1].astype(BF16),
        "wv": rwkv_w_rkvz[0, 2].astype(BF16), "wz": rwkv_w_rkvz[0, 3].astype(BF16),
        "w0": row(rwkv_w0[0]), "w1": rwkv_w1[0].astype(BF16), "w2": rwkv_w2[0].astype(BF16),
        "a0": row(rwkv_a0[0]), "a1": rwkv_a1[0].astype(BF16), "a2": rwkv_a2[0].astype(BF16),
        "k_k": row(rwkv_k_k[0]), "k_a": row(rwkv_k_a[0]), "r_k": row(rwkv_r_k[0]),
        "lnx_w": row(rwkv_lnx_w[0]), "lnx_b": row(rwkv_lnx_b[0]), "w_o": rwkv_w_o[0].astype(BF16),
        "ones_bd": bd.astype(BF16), "avg_bd": (bd.astype(F32) / RWKV_HEAD).astype(BF16),
    }
    bp = x_prompt.shape[0]
    zeros = lambda *s: jnp.zeros(s, F32)
    _, m_gdn, m_conv, m_rwkv, m_shift = _trunk(
        meta_tokens.astype(F32)[None], zeros(1, GDN_V_HEADS, GDN_HEAD, GDN_HEAD), zeros(1, CONV_W - 1, GDN_CONV_DIM),
        zeros(1, RWKV_HEADS, RWKV_HEAD, RWKV_HEAD), zeros(1, D_MODEL), p, N_META, N_META)
    rep = lambda a: jnp.broadcast_to(a, (bp,) + a.shape[1:])
    y_p, p_gdn, p_conv, p_rwkv, p_shift = _trunk(
        x_prompt, rep(m_gdn), rep(m_conv), rep(m_rwkv), rep(m_shift), p, GDN_CHUNK, RWKV_CHUNK)
    ts = x_sample.shape[1]
    y_s, s_gdn, s_conv, s_rwkv, s_shift = _trunk(
        x_sample, state_gdn[0], state_gdn_conv[0], state_rwkv[0], state_rwkv_shift[0], p, ts, ts)
    return (y_p, y_s, p_gdn[None], p_conv[None], p_rwkv[None], p_shift[None],
            s_gdn[None], s_conv[None], s_rwkv[None], s_shift[None])
```

```python
import functools

import jax
import jax.numpy as jnp
from jax import lax
from jax.experimental import pallas as pl
from jax.experimental.pallas import tpu as pltpu

F32 = jnp.float32
BF16 = jnp.bfloat16

D_MODEL = 1024
N_META = 16
GDN_QK_HEADS = 8
GDN_V_HEADS = 16
GDN_HEAD = 128
GDN_KEY_DIM = GDN_QK_HEADS * GDN_HEAD
GDN_VALUE_DIM = GDN_V_HEADS * GDN_HEAD
GDN_CONV_DIM = 2 * GDN_KEY_DIM + GDN_VALUE_DIM
GDN_QKVZ_DIM = GDN_CONV_DIM + GDN_VALUE_DIM
CONV_W = 4
GDN_SLAB = 256
GDN_ROWS = 128
GDN_TM = 512
GDN_TM_ROWS8 = 256
RWKV_SLAB = 256
RWKV_TM = 256
OUT_TM = 512
GDN_CHUNK = 64
RWKV_HEAD = 64
RWKV_HEADS = D_MODEL // RWKV_HEAD
RWKV_PAIRS = RWKV_HEADS // 2
RWKV_CHUNK = 64
LORA = 64
RMS_EPS = 1e-6
L2_EPS = 1e-6
GN_EPS = 64e-5
DECAY_SCALE = 0.6065306597126334
LOG2_E = 1.4426950408889634
LANES = 128
SUBLANES = 8
BF16_ROWS = 16
VMEM_LIMIT_BYTES = 56 * 1024 * 1024
GDN_CHUNKS_PER_STEP = 4
RWKV_CHUNKS_PER_STEP = 2
ROWS_PER_STEP = 8


def _bdot(a, b):
    return jnp.dot(a.astype(BF16), b.astype(BF16), preferred_element_type=F32)


def _bdot_nt(a, b):
    return lax.dot_general(a.astype(BF16), b.astype(BF16), (((1,), (1,)), ((), ())), preferred_element_type=F32)


def _bdot_tn(a, b):
    return lax.dot_general(a.astype(BF16), b.astype(BF16), (((0,), (0,)), ((), ())), preferred_element_type=F32)


def _split3(a):
    hi = a.astype(BF16).astype(F32)
    r1 = a - hi
    mid = r1.astype(BF16).astype(F32)
    return jnp.concatenate([hi, mid, r1 - mid], axis=0).astype(BF16)


def _tril3(n_out, c):
    r, k = _iota2(n_out, 3 * c)
    k = jnp.where(k >= 2 * c, k - 2 * c, jnp.where(k >= c, k - c, k))
    return jnp.where((r & (c - 1)) >= k, 1.0, 0.0).astype(BF16)


def _cumsum_rows(m3, x, dims):
    x3 = _split3(x)
    if dims == "nn":
        return jnp.dot(m3, x3, preferred_element_type=F32)
    return lax.dot_general(x3, m3, (((0,), (1,)), ((), ())), preferred_element_type=F32)


def _rms(x, w):
    return x * lax.rsqrt(jnp.mean(x * x, axis=-1, keepdims=True) + RMS_EPS) * w


def _sigmoid(x):
    return 1.0 / (1.0 + jnp.exp2(x * -LOG2_E))


def _silu(x):
    return x * _sigmoid(x)


def _softplus(x):
    return jnp.maximum(x, 0.0) + jnp.log1p(jnp.exp(-jnp.abs(x)))


def _iota2(n, m):
    return lax.broadcasted_iota(jnp.int32, (n, m), 0), lax.broadcasted_iota(jnp.int32, (n, m), 1)


def _log2(n):
    l = n.bit_length() - 1
    assert (1 << l) == n, n
    return l


def _tri_inv_many(lows, n, blk, half_rows):
    row, col = _iota2(n, n)
    eye = jnp.where(row == col, 1.0, 0.0).astype(BF16)
    negs = [(-low).astype(BF16) for low in lows]
    d = lambda a, b: jnp.dot(a, b, preferred_element_type=F32).astype(BF16)
    s = 1
    ts = None
    while s < blk:
        sh = _log2(s)
        if half_rows and s >= BF16_ROWS:
            hr, hc = _iota2(n // 2, n)
            row_blk = ((hr >> sh) << 1) + 1
            m = (row_blk >> 1 == (hc >> sh) >> 1) & (((hc >> sh) & 1) == 0)
            lower = lambda t: jnp.concatenate([t[(2 * j + 1) * s:(2 * j + 2) * s] for j in range(n // (2 * s))], axis=0)
            tl = [lower(t) for t in ts]
            xs = [d(t, ng) for t, ng in zip(tl, negs)]
            zs = [d(x, t) for x, t in zip(xs, ts)]
            new = [jnp.where(m, z, t) for z, t in zip(zs, tl)]
            ts = [jnp.concatenate([piece for j in range(n // (2 * s))
                                   for piece in (t[2 * j * s:(2 * j + 1) * s], nw[j * s:(j + 1) * s])], axis=0)
                  for t, nw in zip(ts, new)]
        else:
            sub_r = row >> sh
            sub_c = col >> sh
            m = ((sub_r >> 1) == (sub_c >> 1)) & ((sub_r & 1) == 1) & ((sub_c & 1) == 0)
            if ts is None:
                ts = [jnp.where(m, ng, eye) for ng in negs]
            else:
                xs = [d(t, ng) for t, ng in zip(ts, negs)]
                zs = [d(x, t) for x, t in zip(xs, ts)]
                ts = [jnp.where(m, z, t) for z, t in zip(zs, ts)]
        s *= 2
    return [eye for _ in lows] if ts is None else ts


def _gdn_in_kernel(x_ref, nw_ref, w_ref, wba_ref, prev_ref, cw_ref, act_ref, ba_ref, last_ref, xn_scr, tail_scr, u_scr,
                   *, tm, groups, tiles_per_row):
    i = pl.program_id(0)
    xb = _rms(x_ref[...], nw_ref[...]).astype(BF16)
    xn_scr[...] = xb
    ba_ref[...] = jnp.dot(xb, wba_ref[...], preferred_element_type=F32)

    if groups == 1:
        @pl.when(i == 0)
        def _():
            tail_scr[...] = jnp.zeros(tail_scr.shape, F32)

    col = lambda c: slice(c * GDN_SLAB, (c + 1) * GDN_SLAB)
    n_conv = GDN_CONV_DIM // GDN_SLAB
    n_key = GDN_KEY_DIM // GDN_SLAB
    n_all = GDN_QKVZ_DIM // GDN_SLAB
    in_scratch = lambda c: groups == 1 and c < n_conv

    def l2n(a, scale):
        parts = []
        for h in range(GDN_SLAB // GDN_HEAD):
            ah = a[:, h * GDN_HEAD:(h + 1) * GDN_HEAD]
            parts.append(ah * (lax.rsqrt(jnp.sum(ah * ah, axis=-1, keepdims=True) + L2_EPS) * scale))
        return jnp.concatenate(parts, axis=1)

    def finish(a, c):
        if c < n_key:
            return l2n(a, GDN_HEAD ** -0.5)
        return l2n(a, 1.0) if c < 2 * n_key else a

    def issue(c):
        u = jnp.dot(xn_scr[...], w_ref[:, col(c)], preferred_element_type=F32)
        if in_scratch(c):
            u_scr[c % 2, SUBLANES:SUBLANES + tm, :] = u
            return None
        return u

    def complete(c, u):
        cs = col(c)
        if c >= n_conv:
            act_ref[:, cs] = _silu(u).astype(BF16)
        elif groups == 1:
            buf = c % 2
            u_scr[buf, 0:SUBLANES, :] = jnp.where((i % tiles_per_row) == 0, prev_ref[0, :, cs], tail_scr[:, cs])
            end = u_scr[buf, tm:tm + SUBLANES, :]
            tail_scr[:, cs] = end
            last_ref[0, :, cs] = end
            nr = min(GDN_ROWS, tm)
            for rb in range(tm // nr):
                r0 = SUBLANES + rb * nr
                y = u_scr[buf, r0:r0 + nr, :] * cw_ref[CONV_W - 1:CONV_W, cs]
                for s in range(1, CONV_W):
                    y = y + u_scr[buf, r0 - s:r0 - s + nr, :] * cw_ref[CONV_W - 1 - s:CONV_W - s, cs]
                act_ref[rb * nr:(rb + 1) * nr, cs] = finish(_silu(y), c).astype(BF16)
        else:
            rows = lax.broadcasted_iota(jnp.int32, (tm, 1), 0)
            t_in = rows & (SUBLANES - 1)
            prev = prev_ref[:, :, cs].reshape(tm, GDN_SLAB)
            y = u * cw_ref[CONV_W - 1:CONV_W, cs]
            for s in range(1, CONV_W):
                shifted = jnp.where(t_in < s, pltpu.roll(prev, tm - SUBLANES + s, axis=0), pltpu.roll(u, s, axis=0))
                y = y + shifted * cw_ref[CONV_W - 1 - s:CONV_W - s, cs]
            last_ref[:, :, cs] = u.reshape(groups, SUBLANES, GDN_SLAB)
            act_ref[:, cs] = finish(_silu(y), c).astype(BF16)

    conv_slabs, z_slabs = list(range(n_conv)), list(range(n_conv, n_all))
    order = []
    while conv_slabs or z_slabs:
        order += conv_slabs[:2]
        conv_slabs = conv_slabs[2:]
        order += z_slabs[:1]
        z_slabs = z_slabs[1:]
    pending = issue(order[0])
    for pos, c in enumerate(order):
        u = pending
        if pos + 1 < len(order):
            pending = issue(order[pos + 1])
        complete(c, u)


def _gdn_in(x2d, nw, w_in, w_ba, prev8, conv_wt, t):
    m = x2d.shape[0]
    if t == SUBLANES:
        tm = _pick_tile(m, GDN_TM_ROWS8)
        groups, tiles_per_row = tm // SUBLANES, 1
        before = pl.BlockSpec((groups, SUBLANES, GDN_CONV_DIM), lambda i: (i, 0, 0))
    else:
        tm = _pick_tile(t, GDN_TM)
        groups, tiles_per_row = 1, t // tm
        assert tm % SUBLANES == 0 and tm > SUBLANES and tm % min(GDN_ROWS, tm) == 0
        before = pl.BlockSpec((1, SUBLANES, GDN_CONV_DIM), lambda i: (i // tiles_per_row, 0, 0))
    kern = functools.partial(_gdn_in_kernel, tm=tm, groups=groups, tiles_per_row=tiles_per_row)
    return pl.pallas_call(
        kern,
        grid=(m // tm,),
        in_specs=[
            pl.BlockSpec((tm, D_MODEL), lambda i: (i, 0)),
            pl.BlockSpec((1, D_MODEL), lambda i: (0, 0)),
            pl.BlockSpec((D_MODEL, GDN_QKVZ_DIM), lambda i: (0, 0)),
            pl.BlockSpec((D_MODEL, LANES), lambda i: (0, 0)),
            before,
            pl.BlockSpec((CONV_W, GDN_CONV_DIM), lambda i: (0, 0)),
        ],
        out_specs=[
            pl.BlockSpec((tm, GDN_QKVZ_DIM), lambda i: (i, 0)),
            pl.BlockSpec((tm, LANES), lambda i: (i, 0)),
            pl.BlockSpec((groups, SUBLANES, GDN_CONV_DIM), lambda i: (i, 0, 0)),
        ],
        out_shape=[
            jax.ShapeDtypeStruct((m, GDN_QKVZ_DIM), BF16),
            jax.ShapeDtypeStruct((m, LANES), F32),
            jax.ShapeDtypeStruct((m // tm * groups, SUBLANES, GDN_CONV_DIM), F32),
        ],
        scratch_shapes=[pltpu.VMEM((tm, D_MODEL), BF16), pltpu.VMEM((SUBLANES, GDN_CONV_DIM), F32),
                        pltpu.VMEM((2, tm + SUBLANES, GDN_SLAB), F32)],
        compiler_params=pltpu.CompilerParams(dimension_semantics=("arbitrary",), vmem_limit_bytes=VMEM_LIMIT_BYTES),
        name="gdn_in",
    )(x2d, nw, w_in, w_ba, prev8, conv_wt)


def _gdn_chunk_kernel(act_ref, ba_ref, s0_ref, alog_ref, dtb_ref, o_ref, sout_ref, s_scr, *, chunk, gb, gt, n_steps):
    step = pl.program_id(1)
    C = chunk
    C2 = 2 * C
    carried = n_steps * gt > 1

    if carried:
        @pl.when(step == 0)
        def _():
            s_scr[...] = s0_ref[...]

    tril = _tril3(C, C)
    tril2 = _tril3(C2, C)
    row, col = _iota2(C2, C2)
    causal = ((row >= C) == (col >= C)) & (row >= col)
    top = lax.broadcasted_iota(jnp.int32, (C2, 1), 0) < C
    left = lax.broadcasted_iota(jnp.int32, (1, C2), 1) < C

    pairs = []
    for r in range(gb):
        ba = ba_ref[r]
        beta_l = _sigmoid(ba)
        g_l = -jnp.exp(alog_ref[...]) * _softplus(ba + dtb_ref[...])
        for j in range(gt):
            rows = slice(j * C, (j + 1) * C)
            beta_all = beta_l[rows]
            gc_all = _cumsum_rows(tril, g_l[rows], "nn")
            gct_all = _cumsum_rows(tril2, g_l[rows], "tn")
            for qh in range(GDN_QK_HEADS):
                q = act_ref[r, rows, qh * GDN_HEAD:(qh + 1) * GDN_HEAD].astype(F32)
                k = act_ref[r, rows, GDN_KEY_DIM + qh * GDN_HEAD:GDN_KEY_DIM + (qh + 1) * GDN_HEAD].astype(F32)
                h0 = 2 * qh
                h1 = h0 + 1
                a0, a1 = GDN_V_HEADS + h0, GDN_V_HEADS + h1
                beta_s = jnp.concatenate([beta_all[:, h0:h0 + 1], beta_all[:, h1:h1 + 1]], axis=0)
                gc_s = jnp.concatenate([gc_all[:, a0:a0 + 1], gc_all[:, a1:a1 + 1]], axis=0)
                gc_row = jnp.where(left, gct_all[a0:a0 + 1, :], gct_all[a1:a1 + 1, :])
                glast0 = gc_all[C - 1:C, a0:a0 + 1]
                glast1 = gc_all[C - 1:C, a1:a1 + 1]
                glast_s = jnp.where(top, glast0, glast1)
                decay = jnp.exp(jnp.where(causal, gc_s - gc_row, -jnp.inf))
                k_s = jnp.concatenate([k, k], axis=0)
                q_s = jnp.concatenate([q, q], axis=0)
                v0 = 2 * GDN_KEY_DIM + h0 * GDN_HEAD
                v_s = jnp.concatenate([act_ref[r, rows, v0:v0 + GDN_HEAD],
                                       act_ref[r, rows, v0 + GDN_HEAD:v0 + 2 * GDN_HEAD]], axis=0).astype(F32)
                egc = jnp.exp(gc_s)
                kb_s = k_s * beta_s
                pairs.append(dict(
                    r=r, j=j, h0=h0, h1=h1,
                    low=_bdot_nt(kb_s, k_s) * decay,
                    rhs=jnp.concatenate([v_s * beta_s, kb_s * egc], axis=1).astype(BF16),
                    attn=_bdot_nt(q_s, k_s) * decay, qd_s=q_s * egc, kd_s=k_s * jnp.exp(glast_s - gc_s),
                    eg0=jnp.exp(glast0), eg1=jnp.exp(glast1)))

    tinvs = _tri_inv_many([pr["low"] for pr in pairs], C2, C, half_rows=True)
    sols = [jnp.dot(t, pr["rhs"], preferred_element_type=F32) for t, pr in zip(tinvs, pairs)]

    def state_in(r, h):
        return s_scr[r, h] if carried else s0_ref[r, h]

    def state_out(r, h, val):
        if carried:
            s_scr[r, h] = val
        else:
            sout_ref[r, h] = val

    for r in range(gb):
        for j in range(gt):
            grp = [(pr, sol) for pr, sol in zip(pairs, sols) if pr["r"] == r and pr["j"] == j]
            rows = slice(j * C, (j + 1) * C)
            sts, wqs = [], []
            for pr, sol in grp:
                w_s = sol[:, GDN_HEAD:]
                st0 = state_in(r, pr["h0"])
                st1 = state_in(r, pr["h1"])
                sts.append((st0, st1))
                wqs.append((_bdot(jnp.concatenate([w_s[:C], pr["qd_s"][:C]], axis=0), st0),
                            _bdot(jnp.concatenate([w_s[C:], pr["qd_s"][C:]], axis=0), st1)))
            for (pr, sol), (st0, st1), (wq0, wq1) in zip(grp, sts, wqs):
                h0, h1 = pr["h0"], pr["h1"]
                u_s = sol[:, :GDN_HEAD]
                vn0 = u_s[:C] - wq0[:C]
                vn1 = u_s[C:] - wq1[:C]
                o_s = _bdot(pr["attn"], jnp.concatenate([vn0, vn1], axis=0))
                o_ref[r, rows, h0 * GDN_HEAD:(h0 + 1) * GDN_HEAD] = (o_s[:C] + wq0[C:]).astype(BF16)
                o_ref[r, rows, h1 * GDN_HEAD:(h1 + 1) * GDN_HEAD] = (o_s[C:] + wq1[C:]).astype(BF16)
                state_out(r, h0, st0 * pr["eg0"] + _bdot_tn(pr["kd_s"][:C], vn0))
                state_out(r, h1, st1 * pr["eg1"] + _bdot_tn(pr["kd_s"][C:], vn1))

    if carried:
        @pl.when(step == n_steps - 1)
        def _():
            sout_ref[...] = s_scr[...]


def _gdn_chunk(act, ba, s0, alog_row, dtb_row, chunk, gb, gt):
    b, t, _ = act.shape
    span = gt * chunk
    n_steps = t // span
    assert b % gb == 0 and t % span == 0 and (gb == 1 or n_steps == 1)
    kern = functools.partial(_gdn_chunk_kernel, chunk=chunk, gb=gb, gt=gt, n_steps=n_steps)
    state = pl.BlockSpec((gb, GDN_V_HEADS, GDN_HEAD, GDN_HEAD), lambda i, c: (i, 0, 0, 0))
    return pl.pallas_call(
        kern,
        grid=(b // gb, n_steps),
        in_specs=[
            pl.BlockSpec((gb, span, GDN_CONV_DIM), lambda i, c: (i, c, 0)),
            pl.BlockSpec((gb, span, LANES), lambda i, c: (i, c, 0)),
            state,
            pl.BlockSpec((1, LANES), lambda i, c: (0, 0)),
            pl.BlockSpec((1, LANES), lambda i, c: (0, 0)),
        ],
        out_specs=[pl.BlockSpec((gb, span, GDN_VALUE_DIM), lambda i, c: (i, c, 0)), state],
        out_shape=[
            jax.ShapeDtypeStruct((b, t, GDN_VALUE_DIM), BF16),
            jax.ShapeDtypeStruct((b, GDN_V_HEADS, GDN_HEAD, GDN_HEAD), F32),
        ],
        scratch_shapes=[pltpu.VMEM((gb if n_steps * gt > 1 else 1, GDN_V_HEADS, GDN_HEAD, GDN_HEAD), F32)],
        compiler_params=pltpu.CompilerParams(
            dimension_semantics=("parallel", "arbitrary"), vmem_limit_bytes=VMEM_LIMIT_BYTES),
        name="gdn_chunk",
    )(act, ba, s0, alog_row, dtb_row)


def _gdn_out_kernel(o_ref, gate_ref, x_ref, gw_ref, w_ref, y_ref):
    gw = gw_ref[...]
    parts = []
    for h in range(GDN_V_HEADS):
        sl = slice(h * GDN_HEAD, (h + 1) * GDN_HEAD)
        parts.append((_rms(o_ref[:, sl].astype(F32), gw) * gate_ref[:, sl].astype(F32)).astype(BF16))
    gated = jnp.concatenate(parts, axis=1)
    y_ref[...] = x_ref[...] + jnp.dot(gated, w_ref[...], preferred_element_type=F32)


def _gdn_out(o2d, act2d, x2d, gw, w_out, tm):
    m = x2d.shape[0]
    z_block = GDN_CONV_DIM // GDN_VALUE_DIM
    return pl.pallas_call(
        _gdn_out_kernel,
        grid=(m // tm,),
        in_specs=[
            pl.BlockSpec((tm, GDN_VALUE_DIM), lambda i: (i, 0)),
            pl.BlockSpec((tm, GDN_VALUE_DIM), lambda i: (i, z_block)),
            pl.BlockSpec((tm, D_MODEL), lambda i: (i, 0)),
            pl.BlockSpec((1, GDN_HEAD), lambda i: (0, 0)),
            pl.BlockSpec((GDN_VALUE_DIM, D_MODEL), lambda i: (0, 0)),
        ],
        out_specs=pl.BlockSpec((tm, D_MODEL), lambda i: (i, 0)),
        out_shape=jax.ShapeDtypeStruct((m, D_MODEL), F32),
        compiler_params=pltpu.CompilerParams(
            dimension_semantics=("parallel",), vmem_limit_bytes=VMEM_LIMIT_BYTES),
        name="gdn_out",
    )(o2d, act2d, x2d, gw, w_out)


def _head_sum(x, ones_bd):
    return jnp.concatenate([_bdot(x[:, p * LANES:(p + 1) * LANES], ones_bd) for p in range(x.shape[1] // LANES)], axis=1)


def _rwkv_in_kernel(x_ref, xp_ref, sh_ref, nw_ref, mu_ref, wr_ref, wk_ref, wv_ref, wz_ref,
                    w0_ref, w1_ref, w2_ref, a0_ref, a1_ref, a2_ref, kk_ref, ka_ref, rk_ref, ones_ref,
                    r_out, k_out, v_out, ld_out, kn_out, b_out, z_out, bonus_out, last_out, *, tb, tt, n_t):
    t = pl.program_id(1)
    m = tb * tt
    nw = nw_ref[...]
    xn = _rms(x_ref[...].reshape(m, D_MODEL), nw)
    prev_last = _rms(xp_ref[:, SUBLANES - 1:SUBLANES, :], nw)
    first = jnp.where(t == 0, sh_ref[...], prev_last)
    first = jnp.broadcast_to(first, (tb, tt, D_MODEL)).reshape(m, D_MODEL)
    rolled = pltpu.roll(xn, 1, axis=0)
    rows = lax.broadcasted_iota(jnp.int32, (m, 1), 0)
    xprev = jnp.where((rows & (tt - 1)) == 0, first, rolled)
    xx = xprev - xn
    mu = mu_ref[...]

    mixed = [(xn + xx * mu[i:i + 1, :]).astype(BF16) for i in range(6)]
    w_mid = jnp.tanh(jnp.dot(mixed[4], w1_ref[...], preferred_element_type=F32)).astype(BF16)
    a_mid = jnp.dot(mixed[5], a1_ref[...], preferred_element_type=F32).astype(BF16)
    ones_bd = ones_ref[...]
    d = functools.partial(jnp.dot, preferred_element_type=F32)

    def project(cs):
        return (d(mixed[0], wr_ref[:, cs]), d(mixed[1], wk_ref[:, cs]), d(mixed[2], wv_ref[:, cs]),
                d(mixed[3], wz_ref[:, cs]), d(w_mid, w2_ref[:, cs]), d(a_mid, a2_ref[:, cs]))

    n_slabs = D_MODEL // RWKV_SLAB
    col = lambda c: slice(c * RWKV_SLAB, (c + 1) * RWKV_SLAB)
    nxt = project(col(0))
    for c in range(n_slabs):
        cs = col(c)
        r, k, v, z, w_lo, a_lo = nxt
        if c + 1 < n_slabs:
            nxt = project(col(c + 1))
        a = _sigmoid(a0_ref[:, cs] + a_lo)
        kk = k * kk_ref[:, cs]
        kn = kk * lax.rsqrt(_head_sum(kk * kk, ones_bd) + L2_EPS)
        k = k * (1.0 + (a - 1.0) * ka_ref[:, cs])
        r_out[:, cs] = r
        k_out[:, cs] = k
        v_out[:, cs] = v
        z_out[:, cs] = z
        ld_out[:, cs] = -DECAY_SCALE * _sigmoid(w0_ref[:, cs] + w_lo)
        kn_out[:, cs] = kn
        b_out[:, cs] = kn * a
        bonus_out[:, cs] = _head_sum(r * k * rk_ref[:, cs], ones_bd) * v

    @pl.when(t == n_t - 1)
    def _():
        last_out[...] = _rms(x_ref[:, tt - 1:tt, :], nw)


def _rwkv_in(x3d, shift, nw, mu, wr, wk, wv, wz, w0, w1, w2, a0, a1, a2, k_k, k_a, r_k, ones_bd, tb, tt):
    b, t, _ = x3d.shape
    n_t = t // tt
    tpb = tt // SUBLANES
    kern = functools.partial(_rwkv_in_kernel, tb=tb, tt=tt, n_t=n_t)
    row = lambda n: pl.BlockSpec((n, D_MODEL), lambda i, j: (0, 0))
    big = pl.BlockSpec((D_MODEL, D_MODEL), lambda i, j: (0, 0))
    tok = pl.BlockSpec((tb * tt, D_MODEL), lambda i, j: (i * n_t + j, 0))
    m = b * t
    outs = pl.pallas_call(
        kern,
        grid=(b // tb, n_t),
        in_specs=[
            pl.BlockSpec((tb, tt, D_MODEL), lambda i, j: (i, j, 0)),
            pl.BlockSpec((tb, SUBLANES, D_MODEL), lambda i, j: (i, jnp.maximum(j * tpb - 1, 0), 0)),
            pl.BlockSpec((tb, 1, D_MODEL), lambda i, j: (i, 0, 0)),
            row(1), row(6), big, big, big, big,
            row(1), pl.BlockSpec((D_MODEL, LORA), lambda i, j: (0, 0)), pl.BlockSpec((LORA, D_MODEL), lambda i, j: (0, 0)),
            row(1), pl.BlockSpec((D_MODEL, LORA), lambda i, j: (0, 0)), pl.BlockSpec((LORA, D_MODEL), lambda i, j: (0, 0)),
            row(1), row(1), row(1),
            pl.BlockSpec((LANES, LANES), lambda i, j: (0, 0)),
        ],
        out_specs=[tok] * 8 + [pl.BlockSpec((tb, 1, D_MODEL), lambda i, j: (i, 0, 0))],
        out_shape=[jax.ShapeDtypeStruct((m, D_MODEL), F32)] * 8 + [jax.ShapeDtypeStruct((b, 1, D_MODEL), F32)],
        compiler_params=pltpu.CompilerParams(
            dimension_semantics=("parallel", "arbitrary"), vmem_limit_bytes=VMEM_LIMIT_BYTES),
        name="rwkv_in",
    )(x3d, x3d, shift, nw, mu, wr, wk, wv, wz, w0, w1, w2, a0, a1, a2, k_k, k_a, r_k, ones_bd)
    return outs


def _rwkv_rec_kernel(r_ref, k_ref, v_ref, ld_ref, kn_ref, b_ref, s0_ref, y_ref, sout_ref, s_scr,
                     *, chunk, gb, gt, n_steps):
    step = pl.program_id(1)
    C = chunk
    C2 = 2 * C
    carried = n_steps * gt > 1

    def pair_state(rr, p):
        z = jnp.zeros((RWKV_HEAD, RWKV_HEAD), F32)
        return jnp.concatenate([jnp.concatenate([s0_ref[rr, 2 * p], z], axis=1),
                                jnp.concatenate([z, s0_ref[rr, 2 * p + 1]], axis=1)], axis=0)

    def put_state(rr, p, val):
        sout_ref[rr, 2 * p] = val[:RWKV_HEAD, :RWKV_HEAD]
        sout_ref[rr, 2 * p + 1] = val[RWKV_HEAD:, RWKV_HEAD:]

    if carried:
        @pl.when(step == 0)
        def _():
            for rr in range(gb):
                for p in range(RWKV_PAIRS):
                    s_scr[rr, p] = pair_state(rr, p)

    tril = _tril3(C, C)
    lane = lax.broadcasted_iota(jnp.int32, (1, LANES), 1)
    m_lo = lane < RWKV_HEAD
    rc, cc = _iota2(C, C2)
    tl = cc & (C - 1)
    strict_pair = rc > tl
    incl_pair = rc >= tl

    def stack(x):
        return jnp.concatenate([jnp.where(m_lo, x, 0.0), jnp.where(m_lo, 0.0, x)], axis=0)

    pairs = []
    for rr in range(gb):
        for j in range(gt):
            rows = slice(j * C, (j + 1) * C)
            for p in range(RWKV_PAIRS):
                sl = slice(p * LANES, (p + 1) * LANES)
                pairs.append(dict(rr=rr, j=j, p=p, rows=rows, sl=sl))
    for pr in pairs:
        pr["gc"] = _cumsum_rows(tril, ld_ref[pr["rr"], pr["rows"], pr["sl"]], "nn")
    for pr in pairs:
        at_ref = (pr["rr"], pr["rows"], pr["sl"])
        gc = pr["gc"]
        k = k_ref[at_ref]
        b = b_ref[at_ref]
        glast = gc[C - 1:C, :]
        e_neg = jnp.exp(-gc)
        e_end = jnp.exp(glast - gc)
        rt = r_ref[at_ref] * jnp.exp(gc)
        at = -kn_ref[at_ref] * jnp.exp(gc - ld_ref[at_ref])
        at_s = stack(at)
        pr.update(rt=rt, at_s=at_s, v_s=stack(v_ref[at_ref]).astype(BF16),
                  kh_s=stack(k * e_end), bh_s=stack(b * e_end).astype(BF16), eg=jnp.exp(glast))
        g = _bdot_nt(jnp.concatenate([at, rt], axis=0),
                     jnp.concatenate([stack(k * e_neg), stack(b * e_neg)], axis=0))
        ab = g[:C, C2:]
        pr["low"] = -jnp.concatenate([ab, ab], axis=0)
        pr["rb"] = jnp.where(incl_pair, g[C:, C2:], 0.0).astype(BF16)
        pr["rk"] = jnp.where(incl_pair, g[C:, :C2], 0.0).astype(BF16)
        pr["ak"] = jnp.where(strict_pair, g[:C, :C2], 0.0)
    for pr in pairs:
        akv = jnp.dot(pr["ak"].astype(BF16), pr["v_s"], preferred_element_type=F32)
        pr["rhs"] = jnp.concatenate([pr["at_s"], stack(akv)], axis=1).astype(BF16)

    tinvs = _tri_inv_many([pr["low"] for pr in pairs], C2, C, half_rows=False)
    sols = [jnp.dot(t, pr["rhs"], preferred_element_type=F32).astype(BF16) for t, pr in zip(tinvs, pairs)]
    for pr, sol in zip(pairs, sols):
        a2_s = sol[:, :LANES]
        u2_s = sol[:, LANES:]
        d = functools.partial(jnp.dot, preferred_element_type=F32)
        pr["r2"] = pr["rt"] + d(pr["rb"], a2_s)
        pr["y2"] = d(pr["rk"], pr["v_s"]) + d(pr["rb"], u2_s)
        pr["w2"] = _bdot_tn(a2_s, pr["bh_s"])
        pr["n2"] = _bdot_tn(jnp.concatenate([pr["v_s"], u2_s], axis=0),
                            jnp.concatenate([pr["kh_s"].astype(BF16), pr["bh_s"]], axis=0))

    for pr in pairs:
        rr, p = pr["rr"], pr["p"]
        st = s_scr[rr, p] if carried else pair_state(rr, p)
        y_ref[rr, pr["rows"], pr["sl"]] = _bdot_nt(pr["r2"], st) + pr["y2"]
        new = st * pr["eg"] + (_bdot(st, pr["w2"]) + pr["n2"])
        if carried:
            s_scr[rr, p] = new
        else:
            put_state(rr, p, new)

    if carried:
        @pl.when(step == n_steps - 1)
        def _():
            for rr in range(gb):
                for p in range(RWKV_PAIRS):
                    put_state(rr, p, s_scr[rr, p])


def _rwkv_rec(r, k, v, ld, kn, bvec, s0, chunk, gb, gt):
    b, t, _ = r.shape
    span = gt * chunk
    n_steps = t // span
    assert b % gb == 0 and t % span == 0 and (gb == 1 or n_steps == 1)
    kern = functools.partial(_rwkv_rec_kernel, chunk=chunk, gb=gb, gt=gt, n_steps=n_steps)
    tok = pl.BlockSpec((gb, span, D_MODEL), lambda i, c: (i, c, 0))
    st = pl.BlockSpec((gb, RWKV_HEADS, RWKV_HEAD, RWKV_HEAD), lambda i, c: (i, 0, 0, 0))
    return pl.pallas_call(
        kern,
        grid=(b // gb, n_steps),
        in_specs=[tok] * 6 + [st],
        out_specs=[tok, st],
        out_shape=[
            jax.ShapeDtypeStruct((b, t, D_MODEL), F32),
            jax.ShapeDtypeStruct((b, RWKV_HEADS, RWKV_HEAD, RWKV_HEAD), F32),
        ],
        scratch_shapes=[pltpu.VMEM((gb if n_steps * gt > 1 else 1, RWKV_PAIRS, LANES, LANES), F32)],
        compiler_params=pltpu.CompilerParams(
            dimension_semantics=("parallel", "arbitrary"), vmem_limit_bytes=VMEM_LIMIT_BYTES),
        name="rwkv_rec",
    )(r, k, v, ld, kn, bvec, s0)


def _rwkv_out_kernel(y_ref, z_ref, bonus_ref, x_ref, lw_ref, lb_ref, avg_ref, w_ref, fw_ref, out_ref):
    y = y_ref[...]
    avg = avg_ref[...]
    mean = _head_sum(y, avg)
    yc = y - mean
    var = _head_sum(yc * yc, avg)
    gn = yc * lax.rsqrt(var + GN_EPS) * lw_ref[...] + lb_ref[...]
    o = (gn + bonus_ref[...]) * _silu(z_ref[...])
    x2 = x_ref[...] + _bdot(o, w_ref[...])
    out_ref[...] = _rms(x2, fw_ref[...])


def _rwkv_out(y2d, z2d, bonus2d, x2d, lnx_w, lnx_b, avg_bd, w_o, fw, tm):
    m = x2d.shape[0]
    tok = pl.BlockSpec((tm, D_MODEL), lambda i: (i, 0))
    row = pl.BlockSpec((1, D_MODEL), lambda i: (0, 0))
    return pl.pallas_call(
        _rwkv_out_kernel,
        grid=(m // tm,),
        in_specs=[tok, tok, tok, tok, row, row,
                  pl.BlockSpec((LANES, LANES), lambda i: (0, 0)),
                  pl.BlockSpec((D_MODEL, D_MODEL), lambda i: (0, 0)), row],
        out_specs=tok,
        out_shape=jax.ShapeDtypeStruct((m, D_MODEL), F32),
        compiler_params=pltpu.CompilerParams(
            dimension_semantics=("parallel",), vmem_limit_bytes=VMEM_LIMIT_BYTES),
        name="rwkv_out",
    )(y2d, z2d, bonus2d, x2d, lnx_w, lnx_b, avg_bd, w_o, fw)


def _pick_tile(m, cap):
    t = min(m, cap)
    while m % t:
        t //= 2
    return t


def _step_groups(b, t, chunk, chunks_per_step):
    n_chunks = t // chunk
    if n_chunks > 1:
        return 1, _pick_tile(n_chunks, chunks_per_step)
    return _pick_tile(b, ROWS_PER_STEP), 1


def _trunk(x, gdn_s, gdn_conv, rwkv_s, rwkv_shift, p, gdn_chunk, rwkv_chunk):
    b, t, _ = x.shape
    m = b * t
    x2d = x.reshape(m, D_MODEL)
    prev8 = jnp.pad(gdn_conv, ((0, 0), (SUBLANES - (CONV_W - 1), 0), (0, 0)))
    act, ba, last8 = _gdn_in(x2d, p["nw0"], p["w_in"], p["w_ba"], prev8, p["conv_wt"], t)
    conv_new = last8.reshape(b, -1, SUBLANES, GDN_CONV_DIM)[:, -1, SUBLANES - (CONV_W - 1):, :]
    o, s_gdn = _gdn_chunk(act.reshape(b, t, GDN_QKVZ_DIM), ba.reshape(b, t, LANES), gdn_s,
                          p["alog_row"], p["dtb_row"], gdn_chunk, *_step_groups(b, t, gdn_chunk, GDN_CHUNKS_PER_STEP))
    tm2 = _pick_tile(m, OUT_TM)
    x1 = _gdn_out(o.reshape(m, GDN_VALUE_DIM), act, x2d, p["gn_w"], p["w_out"], tm2)
    if t >= RWKV_TM:
        tb, tt = 1, RWKV_TM
    elif t * b <= RWKV_TM:
        tb, tt = b, t
    else:
        tb, tt = RWKV_TM // t, t
    r, k, v, ld, kn, bvec, z, bonus, last = _rwkv_in(
        x1.reshape(b, t, D_MODEL), rwkv_shift.reshape(b, 1, D_MODEL), p["nw1"], p["mu"],
        p["wr"], p["wk"], p["wv"], p["wz"], p["w0"], p["w1"], p["w2"], p["a0"], p["a1"], p["a2"],
        p["k_k"], p["k_a"], p["r_k"], p["ones_bd"], tb, tt)
    sh3 = lambda a: a.reshape(b, t, D_MODEL)
    y, s_rwkv = _rwkv_rec(sh3(r), sh3(k), sh3(v), sh3(ld), sh3(kn), sh3(bvec), rwkv_s,
                          rwkv_chunk, *_step_groups(b, t, rwkv_chunk, RWKV_CHUNKS_PER_STEP))
    out = _rwkv_out(y.reshape(m, D_MODEL), z, bonus, x1, p["lnx_w"], p["lnx_b"], p["avg_bd"], p["w_o"], p["fw"], tm2)
    return (out.reshape(b, t, D_MODEL), s_gdn, conv_new, s_rwkv, last.reshape(b, D_MODEL))


def kernel(x_prompt, x_sample, state_gdn, state_gdn_conv, state_rwkv, state_rwkv_shift, meta_tokens, norm_w, final_norm_w, gdn_w_in, gdn_conv_w, gdn_a_log, gdn_dt_bias, gdn_norm_w, gdn_w_out, rwkv_mu, rwkv_w_rkvz, rwkv_w0, rwkv_w1, rwkv_w2, rwkv_a0, rwkv_a1, rwkv_a2, rwkv_k_k, rwkv_k_a, rwkv_r_k, rwkv_lnx_w, rwkv_lnx_b, rwkv_w_o):
    assert norm_w.shape[0] == 2 and gdn_w_in.shape[0] == 1 and rwkv_mu.shape[0] == 1
    row = lambda a: a.reshape(1, -1).astype(F32)
    w_in = gdn_w_in[0]
    w_ba = jnp.pad(w_in[:, GDN_QKVZ_DIM:], ((0, 0), (0, LANES - 2 * GDN_V_HEADS)))
    gate_row = lambda a: jnp.pad(a.astype(F32), (GDN_V_HEADS, LANES - 2 * GDN_V_HEADS)).reshape(1, LANES)
    rb, cb = jnp.arange(LANES)[:, None] // RWKV_HEAD, jnp.arange(LANES)[None, :] // RWKV_HEAD
    bd = (rb == cb)
    p = {
        "nw0": row(norm_w[0]), "nw1": row(norm_w[1]), "fw": row(final_norm_w),
        "w_in": w_in.astype(BF16), "w_ba": w_ba.astype(BF16),
        "conv_wt": gdn_conv_w[0].T.astype(F32),
        "alog_row": gate_row(gdn_a_log[0]), "dtb_row": gate_row(gdn_dt_bias[0]),
        "gn_w": row(gdn_norm_w[0]), "w_out": gdn_w_out[0].astype(BF16),
        "mu": rwkv_mu[0].astype(F32),
        "wr": rwkv_w_rkvz[0, 0].astype(BF16), "wk": rwkv_w_rkvz[0, 1].astype(BF16),
        "wv": rwkv_w_rkvz[0, 2].astype(BF16), "wz": rwkv_w_rkvz[0, 3].astype(BF16),
        "w0": row(rwkv_w0[0]), "w1": rwkv_w1[0].astype(BF16), "w2": rwkv_w2[0].astype(BF16),
        "a0": row(rwkv_a0[0]), "a1": rwkv_a1[0].astype(BF16), "a2": rwkv_a2[0].astype(BF16),
        "k_k": row(rwkv_k_k[0]), "k_a": row(rwkv_k_a[0]), "r_k": row(rwkv_r_k[0]),
        "lnx_w": row(rwkv_lnx_w[0]), "lnx_b": row(rwkv_lnx_b[0]), "w_o": rwkv_w_o[0].astype(BF16),
        "ones_bd": bd.astype(BF16), "avg_bd": (bd.astype(F32) / RWKV_HEAD).astype(BF16),
    }
    bp = x_prompt.shape[0]
    zeros = lambda *s: jnp.zeros(s, F32)
    _, m_gdn, m_conv, m_rwkv, m_shift = _trunk(
        meta_tokens.astype(F32)[None], zeros(1, GDN_V_HEADS, GDN_HEAD, GDN_HEAD), zeros(1, CONV_W - 1, GDN_CONV_DIM),
        zeros(1, RWKV_HEADS, RWKV_HEAD, RWKV_HEAD), zeros(1, D_MODEL), p, N_META, N_META)
    rep = lambda a: jnp.broadcast_to(a, (bp,) + a.shape[1:])
    y_p, p_gdn, p_conv, p_rwkv, p_shift = _trunk(
        x_prompt, rep(m_gdn), rep(m_conv), rep(m_rwkv), rep(m_shift), p, GDN_CHUNK, RWKV_CHUNK)
    ts = x_sample.shape[1]
    y_s, s_gdn, s_conv, s_rwkv, s_shift = _trunk(
        x_sample, state_gdn[0], state_gdn_conv[0], state_rwkv[0], state_rwkv_shift[0], p, ts, ts)
    return (y_p, y_s, p_gdn[None], p_conv[None], p_rwkv[None], p_shift[None],
            s_gdn[None], s_conv[None], s_rwkv[None], s_shift[None])
```

```python
import functools

import jax
import jax.numpy as jnp
from jax import lax
from jax.experimental import pallas as pl
from jax.experimental.pallas import tpu as pltpu

F32 = jnp.float32
BF16 = jnp.bfloat16

D_MODEL = 1024
N_META = 16
GDN_QK_HEADS = 8
GDN_V_HEADS = 16
GDN_HEAD = 128
GDN_KEY_DIM = GDN_QK_HEADS * GDN_HEAD
GDN_VALUE_DIM = GDN_V_HEADS * GDN_HEAD
GDN_CONV_DIM = 2 * GDN_KEY_DIM + GDN_VALUE_DIM
GDN_QKVZ_DIM = GDN_CONV_DIM + GDN_VALUE_DIM
CONV_W = 4
GDN_SLAB = 256
GDN_ROWS = 128
GDN_TM = 512
GDN_TM_ROWS8 = 256
RWKV_SLAB = 256
RWKV_TM = 256
OUT_TM = 512
OUT_SLAB = 256
GDN_CHUNK = 64
RWKV_HEAD = 64
RWKV_HEADS = D_MODEL // RWKV_HEAD
RWKV_PAIRS = RWKV_HEADS // 2
RWKV_CHUNK = 64
LORA = 64
RMS_EPS = 1e-6
L2_EPS = 1e-6
GN_EPS = 64e-5
DECAY_SCALE = 0.6065306597126334
LOG2_E = 1.4426950408889634
LANES = 128
SUBLANES = 8
BF16_ROWS = 16
VMEM_LIMIT_BYTES = 56 * 1024 * 1024
GDN_CHUNKS_PER_STEP = 4
RWKV_CHUNKS_PER_STEP = 4
ROWS_PER_STEP = 8


def _bdot(a, b):
    return jnp.dot(a.astype(BF16), b.astype(BF16), preferred_element_type=F32)


def _bdot_nt(a, b):
    return lax.dot_general(a.astype(BF16), b.astype(BF16), (((1,), (1,)), ((), ())), preferred_element_type=F32)


def _bdot_tn(a, b):
    return lax.dot_general(a.astype(BF16), b.astype(BF16), (((0,), (0,)), ((), ())), preferred_element_type=F32)


def _split3(a):
    hi = a.astype(BF16).astype(F32)
    r1 = a - hi
    mid = r1.astype(BF16).astype(F32)
    return jnp.concatenate([hi, mid, r1 - mid], axis=0).astype(BF16)


def _tril3(n_out, c):
    r, k = _iota2(n_out, 3 * c)
    k = jnp.where(k >= 2 * c, k - 2 * c, jnp.where(k >= c, k - c, k))
    return jnp.where((r & (c - 1)) >= k, 1.0, 0.0).astype(BF16)


def _cumsum_rows(m3, x, dims):
    x3 = _split3(x)
    if dims == "nn":
        return jnp.dot(m3, x3, preferred_element_type=F32)
    return lax.dot_general(x3, m3, (((0,), (1,)), ((), ())), preferred_element_type=F32)


def _rms(x, w):
    return x * lax.rsqrt(jnp.mean(x * x, axis=-1, keepdims=True) + RMS_EPS) * w


def _sigmoid(x):
    return 1.0 / (1.0 + jnp.exp2(x * -LOG2_E))


def _silu(x):
    return x * _sigmoid(x)


def _softplus(x):
    return jnp.maximum(x, 0.0) + jnp.log1p(jnp.exp(-jnp.abs(x)))


def _iota2(n, m):
    return lax.broadcasted_iota(jnp.int32, (n, m), 0), lax.broadcasted_iota(jnp.int32, (n, m), 1)


def _log2(n):
    l = n.bit_length() - 1
    assert (1 << l) == n, n
    return l


def _tri_inv_many(lows, n, blk, half_rows):
    row, col = _iota2(n, n)
    eye = jnp.where(row == col, 1.0, 0.0).astype(BF16)
    negs = [(-low).astype(BF16) for low in lows]
    d = lambda a, b: jnp.dot(a, b, preferred_element_type=F32).astype(BF16)
    s = 1
    ts = None
    while s < blk:
        sh = _log2(s)
        if half_rows and s >= BF16_ROWS:
            hr, hc = _iota2(n // 2, n)
            row_blk = ((hr >> sh) << 1) + 1
            m = (row_blk >> 1 == (hc >> sh) >> 1) & (((hc >> sh) & 1) == 0)
            lower = lambda t: jnp.concatenate([t[(2 * j + 1) * s:(2 * j + 2) * s] for j in range(n // (2 * s))], axis=0)
            tl = [lower(t) for t in ts]
            xs = [d(t, ng) for t, ng in zip(tl, negs)]
            zs = [d(x, t) for x, t in zip(xs, ts)]
            new = [jnp.where(m, z, t) for z, t in zip(zs, tl)]
            ts = [jnp.concatenate([piece for j in range(n // (2 * s))
                                   for piece in (t[2 * j * s:(2 * j + 1) * s], nw[j * s:(j + 1) * s])], axis=0)
                  for t, nw in zip(ts, new)]
        else:
            sub_r = row >> sh
            sub_c = col >> sh
            m = ((sub_r >> 1) == (sub_c >> 1)) & ((sub_r & 1) == 1) & ((sub_c & 1) == 0)
            if ts is None:
                ts = [jnp.where(m, ng, eye) for ng in negs]
            else:
                xs = [d(t, ng) for t, ng in zip(ts, negs)]
                zs = [d(x, t) for x, t in zip(xs, ts)]
                ts = [jnp.where(m, z, t) for z, t in zip(zs, ts)]
        s *= 2
    return [eye for _ in lows] if ts is None else ts


def _gdn_in_kernel(x_ref, nw_ref, w_ref, wba_ref, prev_ref, cw_ref, act_ref, ba_ref, last_ref, xn_scr, tail_scr, u_scr,
                   *, tm, groups, tiles_per_row):
    i = pl.program_id(0)
    xb = _rms(x_ref[...], nw_ref[...]).astype(BF16)
    xn_scr[...] = xb
    ba_ref[...] = jnp.dot(xb, wba_ref[...], preferred_element_type=F32)

    if groups == 1:
        @pl.when(i == 0)
        def _():
            tail_scr[...] = jnp.zeros(tail_scr.shape, F32)

    col = lambda c: slice(c * GDN_SLAB, (c + 1) * GDN_SLAB)
    n_conv = GDN_CONV_DIM // GDN_SLAB
    n_key = GDN_KEY_DIM // GDN_SLAB
    n_all = GDN_QKVZ_DIM // GDN_SLAB
    in_scratch = lambda c: groups == 1 and c < n_conv

    def l2n(a, scale):
        parts = []
        for h in range(GDN_SLAB // GDN_HEAD):
            ah = a[:, h * GDN_HEAD:(h + 1) * GDN_HEAD]
            parts.append(ah * (lax.rsqrt(jnp.sum(ah * ah, axis=-1, keepdims=True) + L2_EPS) * scale))
        return jnp.concatenate(parts, axis=1)

    def finish(a, c):
        if c < n_key:
            return l2n(a, GDN_HEAD ** -0.5)
        return l2n(a, 1.0) if c < 2 * n_key else a

    def issue(c):
        u = jnp.dot(xn_scr[...], w_ref[:, col(c)], preferred_element_type=F32)
        if in_scratch(c):
            u_scr[c % 2, SUBLANES:SUBLANES + tm, :] = u
            return None
        return u

    def complete(c, u):
        cs = col(c)
        if c >= n_conv:
            act_ref[:, cs] = _silu(u).astype(BF16)
        elif groups == 1:
            buf = c % 2
            u_scr[buf, 0:SUBLANES, :] = jnp.where((i % tiles_per_row) == 0, prev_ref[0, :, cs], tail_scr[:, cs])
            end = u_scr[buf, tm:tm + SUBLANES, :]
            tail_scr[:, cs] = end
            last_ref[0, :, cs] = end
            nr = min(GDN_ROWS, tm)
            for rb in range(tm // nr):
                r0 = SUBLANES + rb * nr
                y = u_scr[buf, r0:r0 + nr, :] * cw_ref[CONV_W - 1:CONV_W, cs]
                for s in range(1, CONV_W):
                    y = y + u_scr[buf, r0 - s:r0 - s + nr, :] * cw_ref[CONV_W - 1 - s:CONV_W - s, cs]
                act_ref[rb * nr:(rb + 1) * nr, cs] = finish(_silu(y), c).astype(BF16)
        else:
            rows = lax.broadcasted_iota(jnp.int32, (tm, 1), 0)
            t_in = rows & (SUBLANES - 1)
            prev = prev_ref[:, :, cs].reshape(tm, GDN_SLAB)
            y = u * cw_ref[CONV_W - 1:CONV_W, cs]
            for s in range(1, CONV_W):
                shifted = jnp.where(t_in < s, pltpu.roll(prev, tm - SUBLANES + s, axis=0), pltpu.roll(u, s, axis=0))
                y = y + shifted * cw_ref[CONV_W - 1 - s:CONV_W - s, cs]
            last_ref[:, :, cs] = u.reshape(groups, SUBLANES, GDN_SLAB)
            act_ref[:, cs] = finish(_silu(y), c).astype(BF16)

    conv_slabs, z_slabs = list(range(n_conv)), list(range(n_conv, n_all))
    order = []
    while conv_slabs or z_slabs:
        order += conv_slabs[:2]
        conv_slabs = conv_slabs[2:]
        order += z_slabs[:1]
        z_slabs = z_slabs[1:]
    pending = issue(order[0])
    for pos, c in enumerate(order):
        u = pending
        if pos + 1 < len(order):
            pending = issue(order[pos + 1])
        complete(c, u)


def _gdn_in(x2d, nw, w_in, w_ba, prev8, conv_wt, t):
    m = x2d.shape[0]
    if t == SUBLANES:
        tm = _pick_tile(m, GDN_TM_ROWS8)
        groups, tiles_per_row = tm // SUBLANES, 1
        before = pl.BlockSpec((groups, SUBLANES, GDN_CONV_DIM), lambda i: (i, 0, 0))
    else:
        tm = _pick_tile(t, GDN_TM)
        groups, tiles_per_row = 1, t // tm
        assert tm % SUBLANES == 0 and tm > SUBLANES and tm % min(GDN_ROWS, tm) == 0
        before = pl.BlockSpec((1, SUBLANES, GDN_CONV_DIM), lambda i: (i // tiles_per_row, 0, 0))
    kern = functools.partial(_gdn_in_kernel, tm=tm, groups=groups, tiles_per_row=tiles_per_row)
    return pl.pallas_call(
        kern,
        grid=(m // tm,),
        in_specs=[
            pl.BlockSpec((tm, D_MODEL), lambda i: (i, 0)),
            pl.BlockSpec((1, D_MODEL), lambda i: (0, 0)),
            pl.BlockSpec((D_MODEL, GDN_QKVZ_DIM), lambda i: (0, 0)),
            pl.BlockSpec((D_MODEL, LANES), lambda i: (0, 0)),
            before,
            pl.BlockSpec((CONV_W, GDN_CONV_DIM), lambda i: (0, 0)),
        ],
        out_specs=[
            pl.BlockSpec((tm, GDN_QKVZ_DIM), lambda i: (i, 0)),
            pl.BlockSpec((tm, LANES), lambda i: (i, 0)),
            pl.BlockSpec((groups, SUBLANES, GDN_CONV_DIM), lambda i: (i, 0, 0)),
        ],
        out_shape=[
            jax.ShapeDtypeStruct((m, GDN_QKVZ_DIM), BF16),
            jax.ShapeDtypeStruct((m, LANES), F32),
            jax.ShapeDtypeStruct((m // tm * groups, SUBLANES, GDN_CONV_DIM), F32),
        ],
        scratch_shapes=[pltpu.VMEM((tm, D_MODEL), BF16), pltpu.VMEM((SUBLANES, GDN_CONV_DIM), F32),
                        pltpu.VMEM((2, tm + SUBLANES, GDN_SLAB), F32)],
        compiler_params=pltpu.CompilerParams(dimension_semantics=("arbitrary",), vmem_limit_bytes=VMEM_LIMIT_BYTES),
        name="gdn_in",
    )(x2d, nw, w_in, w_ba, prev8, conv_wt)


def _gdn_chunk_kernel(act_ref, ba_ref, s0_ref, alog_ref, dtb_ref, o_ref, sout_ref, s_scr, *, chunk, gb, gt, n_steps):
    step = pl.program_id(1)
    C = chunk
    C2 = 2 * C
    carried = n_steps * gt > 1

    if carried:
        @pl.when(step == 0)
        def _():
            s_scr[...] = s0_ref[...]

    tril = _tril3(C, C)
    tril2 = _tril3(C2, C)
    row, col = _iota2(C2, C2)
    causal = ((row >= C) == (col >= C)) & (row >= col)
    top = lax.broadcasted_iota(jnp.int32, (C2, 1), 0) < C
    left = lax.broadcasted_iota(jnp.int32, (1, C2), 1) < C

    pairs = []
    for r in range(gb):
        ba = ba_ref[r]
        beta_l = _sigmoid(ba)
        g_l = -jnp.exp(alog_ref[...]) * _softplus(ba + dtb_ref[...])
        for j in range(gt):
            rows = slice(j * C, (j + 1) * C)
            beta_all = beta_l[rows]
            gc_all = _cumsum_rows(tril, g_l[rows], "nn")
            gct_all = _cumsum_rows(tril2, g_l[rows], "tn")
            for qh in range(GDN_QK_HEADS):
                q = act_ref[r, rows, qh * GDN_HEAD:(qh + 1) * GDN_HEAD].astype(F32)
                k = act_ref[r, rows, GDN_KEY_DIM + qh * GDN_HEAD:GDN_KEY_DIM + (qh + 1) * GDN_HEAD].astype(F32)
                h0 = 2 * qh
                h1 = h0 + 1
                a0, a1 = GDN_V_HEADS + h0, GDN_V_HEADS + h1
                beta_s = jnp.concatenate([beta_all[:, h0:h0 + 1], beta_all[:, h1:h1 + 1]], axis=0)
                gc_s = jnp.concatenate([gc_all[:, a0:a0 + 1], gc_all[:, a1:a1 + 1]], axis=0)
                gc_row = jnp.where(left, gct_all[a0:a0 + 1, :], gct_all[a1:a1 + 1, :])
                glast0 = gc_all[C - 1:C, a0:a0 + 1]
                glast1 = gc_all[C - 1:C, a1:a1 + 1]
                glast_s = jnp.where(top, glast0, glast1)
                decay = jnp.exp(jnp.where(causal, gc_s - gc_row, -jnp.inf))
                k_s = jnp.concatenate([k, k], axis=0)
                q_s = jnp.concatenate([q, q], axis=0)
                v0 = 2 * GDN_KEY_DIM + h0 * GDN_HEAD
                v_s = jnp.concatenate([act_ref[r, rows, v0:v0 + GDN_HEAD],
                                       act_ref[r, rows, v0 + GDN_HEAD:v0 + 2 * GDN_HEAD]], axis=0).astype(F32)
                egc = jnp.exp(gc_s)
                kb_s = k_s * beta_s
                pairs.append(dict(
                    r=r, j=j, h0=h0, h1=h1,
                    low=_bdot_nt(kb_s, k_s) * decay,
                    rhs=jnp.concatenate([v_s * beta_s, kb_s * egc], axis=1).astype(BF16),
                    attn=_bdot_nt(q_s, k_s) * decay, qd_s=q_s * egc, kd_s=k_s * jnp.exp(glast_s - gc_s),
                    eg0=jnp.exp(glast0), eg1=jnp.exp(glast1)))

    tinvs = _tri_inv_many([pr["low"] for pr in pairs], C2, C, half_rows=True)
    sols = [jnp.dot(t, pr["rhs"], preferred_element_type=F32) for t, pr in zip(tinvs, pairs)]

    def state_in(r, h):
        return s_scr[r, h] if carried else s0_ref[r, h]

    def state_out(r, h, val):
        if carried:
            s_scr[r, h] = val
        else:
            sout_ref[r, h] = val

    for r in range(gb):
        for j in range(gt):
            grp = [(pr, sol) for pr, sol in zip(pairs, sols) if pr["r"] == r and pr["j"] == j]
            rows = slice(j * C, (j + 1) * C)
            sts, wqs = [], []
            for pr, sol in grp:
                w_s = sol[:, GDN_HEAD:]
                st0 = state_in(r, pr["h0"])
                st1 = state_in(r, pr["h1"])
                sts.append((st0, st1))
                wqs.append((_bdot(jnp.concatenate([w_s[:C], pr["qd_s"][:C]], axis=0), st0),
                            _bdot(jnp.concatenate([w_s[C:], pr["qd_s"][C:]], axis=0), st1)))
            for (pr, sol), (st0, st1), (wq0, wq1) in zip(grp, sts, wqs):
                h0, h1 = pr["h0"], pr["h1"]
                u_s = sol[:, :GDN_HEAD]
                vn0 = u_s[:C] - wq0[:C]
                vn1 = u_s[C:] - wq1[:C]
                o_s = _bdot(pr["attn"], jnp.concatenate([vn0, vn1], axis=0))
                o_ref[r, rows, h0 * GDN_HEAD:(h0 + 1) * GDN_HEAD] = (o_s[:C] + wq0[C:]).astype(BF16)
                o_ref[r, rows, h1 * GDN_HEAD:(h1 + 1) * GDN_HEAD] = (o_s[C:] + wq1[C:]).astype(BF16)
                state_out(r, h0, st0 * pr["eg0"] + _bdot_tn(pr["kd_s"][:C], vn0))
                state_out(r, h1, st1 * pr["eg1"] + _bdot_tn(pr["kd_s"][C:], vn1))

    if carried:
        @pl.when(step == n_steps - 1)
        def _():
            sout_ref[...] = s_scr[...]


def _gdn_chunk(act, ba, s0, alog_row, dtb_row, chunk, gb, gt):
    b, t, _ = act.shape
    span = gt * chunk
    n_steps = t // span
    assert b % gb == 0 and t % span == 0 and (gb == 1 or n_steps == 1)
    kern = functools.partial(_gdn_chunk_kernel, chunk=chunk, gb=gb, gt=gt, n_steps=n_steps)
    state = pl.BlockSpec((gb, GDN_V_HEADS, GDN_HEAD, GDN_HEAD), lambda i, c: (i, 0, 0, 0))
    return pl.pallas_call(
        kern,
        grid=(b // gb, n_steps),
        in_specs=[
            pl.BlockSpec((gb, span, GDN_CONV_DIM), lambda i, c: (i, c, 0)),
            pl.BlockSpec((gb, span, LANES), lambda i, c: (i, c, 0)),
            state,
            pl.BlockSpec((1, LANES), lambda i, c: (0, 0)),
            pl.BlockSpec((1, LANES), lambda i, c: (0, 0)),
        ],
        out_specs=[pl.BlockSpec((gb, span, GDN_VALUE_DIM), lambda i, c: (i, c, 0)), state],
        out_shape=[
            jax.ShapeDtypeStruct((b, t, GDN_VALUE_DIM), BF16),
            jax.ShapeDtypeStruct((b, GDN_V_HEADS, GDN_HEAD, GDN_HEAD), F32),
        ],
        scratch_shapes=[pltpu.VMEM((gb if n_steps * gt > 1 else 1, GDN_V_HEADS, GDN_HEAD, GDN_HEAD), F32)],
        compiler_params=pltpu.CompilerParams(
            dimension_semantics=("parallel", "arbitrary"), vmem_limit_bytes=VMEM_LIMIT_BYTES),
        name="gdn_chunk",
    )(act, ba, s0, alog_row, dtb_row)


def _gdn_out_kernel(o_ref, gate_ref, x_ref, gw_ref, w_ref, y_ref):
    gw = gw_ref[...]
    acc = x_ref[...]
    for c in range(GDN_VALUE_DIM // OUT_SLAB):
        parts = []
        for h in range(c * OUT_SLAB // GDN_HEAD, (c + 1) * OUT_SLAB // GDN_HEAD):
            sl = slice(h * GDN_HEAD, (h + 1) * GDN_HEAD)
            parts.append((_rms(o_ref[:, sl].astype(F32), gw) * gate_ref[:, sl].astype(F32)).astype(BF16))
        acc = acc + jnp.dot(jnp.concatenate(parts, axis=1), w_ref[c * OUT_SLAB:(c + 1) * OUT_SLAB, :],
                            preferred_element_type=F32)
    y_ref[...] = acc


def _gdn_out(o2d, act2d, x2d, gw, w_out, tm):
    m = x2d.shape[0]
    z_block = GDN_CONV_DIM // GDN_VALUE_DIM
    return pl.pallas_call(
        _gdn_out_kernel,
        grid=(m // tm,),
        in_specs=[
            pl.BlockSpec((tm, GDN_VALUE_DIM), lambda i: (i, 0)),
            pl.BlockSpec((tm, GDN_VALUE_DIM), lambda i: (i, z_block)),
            pl.BlockSpec((tm, D_MODEL), lambda i: (i, 0)),
            pl.BlockSpec((1, GDN_HEAD), lambda i: (0, 0)),
            pl.BlockSpec((GDN_VALUE_DIM, D_MODEL), lambda i: (0, 0)),
        ],
        out_specs=pl.BlockSpec((tm, D_MODEL), lambda i: (i, 0)),
        out_shape=jax.ShapeDtypeStruct((m, D_MODEL), F32),
        compiler_params=pltpu.CompilerParams(
            dimension_semantics=("parallel",), vmem_limit_bytes=VMEM_LIMIT_BYTES),
        name="gdn_out",
    )(o2d, act2d, x2d, gw, w_out)


def _head_sum(x, ones_bd):
    return jnp.concatenate([_bdot(x[:, p * LANES:(p + 1) * LANES], ones_bd) for p in range(x.shape[1] // LANES)], axis=1)


def _rwkv_in_kernel(x_ref, xp_ref, sh_ref, nw_ref, mu_ref, wr_ref, wk_ref, wv_ref, wz_ref,
                    w0_ref, w1_ref, w2_ref, a0_ref, a1_ref, a2_ref, kk_ref, ka_ref, rk_ref, ones_ref,
                    r_out, k_out, v_out, ld_out, kn_out, b_out, z_out, bonus_out, last_out, *, tb, tt, n_t):
    t = pl.program_id(1)
    m = tb * tt
    nw = nw_ref[...]
    xn = _rms(x_ref[...].reshape(m, D_MODEL), nw)
    prev_last = _rms(xp_ref[:, SUBLANES - 1:SUBLANES, :], nw)
    first = jnp.where(t == 0, sh_ref[...], prev_last)
    first = jnp.broadcast_to(first, (tb, tt, D_MODEL)).reshape(m, D_MODEL)
    rolled = pltpu.roll(xn, 1, axis=0)
    rows = lax.broadcasted_iota(jnp.int32, (m, 1), 0)
    xprev = jnp.where((rows & (tt - 1)) == 0, first, rolled)
    xx = xprev - xn
    mu = mu_ref[...]

    mixed = [(xn + xx * mu[i:i + 1, :]).astype(BF16) for i in range(6)]
    w_mid = jnp.tanh(jnp.dot(mixed[4], w1_ref[...], preferred_element_type=F32)).astype(BF16)
    a_mid = jnp.dot(mixed[5], a1_ref[...], preferred_element_type=F32).astype(BF16)
    ones_bd = ones_ref[...]
    d = functools.partial(jnp.dot, preferred_element_type=F32)

    def project(cs):
        return (d(mixed[0], wr_ref[:, cs]), d(mixed[1], wk_ref[:, cs]), d(mixed[2], wv_ref[:, cs]),
                d(mixed[3], wz_ref[:, cs]), d(w_mid, w2_ref[:, cs]), d(a_mid, a2_ref[:, cs]))

    n_slabs = D_MODEL // RWKV_SLAB
    col = lambda c: slice(c * RWKV_SLAB, (c + 1) * RWKV_SLAB)
    nxt = project(col(0))
    for c in range(n_slabs):
        cs = col(c)
        r, k, v, z, w_lo, a_lo = nxt
        if c + 1 < n_slabs:
            nxt = project(col(c + 1))
        a = _sigmoid(a0_ref[:, cs] + a_lo)
        kk = k * kk_ref[:, cs]
        kn = kk * lax.rsqrt(_head_sum(kk * kk, ones_bd) + L2_EPS)
        k = k * (1.0 + (a - 1.0) * ka_ref[:, cs])
        r_out[:, cs] = r
        k_out[:, cs] = k
        v_out[:, cs] = v
        z_out[:, cs] = z
        ld_out[:, cs] = -DECAY_SCALE * _sigmoid(w0_ref[:, cs] + w_lo)
        kn_out[:, cs] = kn
        b_out[:, cs] = kn * a
        bonus_out[:, cs] = _head_sum(r * k * rk_ref[:, cs], ones_bd) * v

    @pl.when(t == n_t - 1)
    def _():
        last_out[...] = _rms(x_ref[:, tt - 1:tt, :], nw)


def _rwkv_in(x3d, shift, nw, mu, wr, wk, wv, wz, w0, w1, w2, a0, a1, a2, k_k, k_a, r_k, ones_bd, tb, tt):
    b, t, _ = x3d.shape
    n_t = t // tt
    tpb = tt // SUBLANES
    kern = functools.partial(_rwkv_in_kernel, tb=tb, tt=tt, n_t=n_t)
    row = lambda n: pl.BlockSpec((n, D_MODEL), lambda i, j: (0, 0))
    big = pl.BlockSpec((D_MODEL, D_MODEL), lambda i, j: (0, 0))
    tok = pl.BlockSpec((tb * tt, D_MODEL), lambda i, j: (i * n_t + j, 0))
    m = b * t
    outs = pl.pallas_call(
        kern,
        grid=(b // tb, n_t),
        in_specs=[
            pl.BlockSpec((tb, tt, D_MODEL), lambda i, j: (i, j, 0)),
            pl.BlockSpec((tb, SUBLANES, D_MODEL), lambda i, j: (i, jnp.maximum(j * tpb - 1, 0), 0)),
            pl.BlockSpec((tb, 1, D_MODEL), lambda i, j: (i, 0, 0)),
            row(1), row(6), big, big, big, big,
            row(1), pl.BlockSpec((D_MODEL, LORA), lambda i, j: (0, 0)), pl.BlockSpec((LORA, D_MODEL), lambda i, j: (0, 0)),
            row(1), pl.BlockSpec((D_MODEL, LORA), lambda i, j: (0, 0)), pl.BlockSpec((LORA, D_MODEL), lambda i, j: (0, 0)),
            row(1), row(1), row(1),
            pl.BlockSpec((LANES, LANES), lambda i, j: (0, 0)),
        ],
        out_specs=[tok] * 8 + [pl.BlockSpec((tb, 1, D_MODEL), lambda i, j: (i, 0, 0))],
        out_shape=[jax.ShapeDtypeStruct((m, D_MODEL), F32)] * 8 + [jax.ShapeDtypeStruct((b, 1, D_MODEL), F32)],
        compiler_params=pltpu.CompilerParams(
            dimension_semantics=("parallel", "arbitrary"), vmem_limit_bytes=VMEM_LIMIT_BYTES),
        name="rwkv_in",
    )(x3d, x3d, shift, nw, mu, wr, wk, wv, wz, w0, w1, w2, a0, a1, a2, k_k, k_a, r_k, ones_bd)
    return outs


def _rwkv_rec_kernel(r_ref, k_ref, v_ref, ld_ref, kn_ref, b_ref, s0_ref, y_ref, sout_ref, s_scr,
                     *, chunk, gb, gt, n_steps):
    step = pl.program_id(1)
    C = chunk
    C2 = 2 * C
    carried = n_steps * gt > 1

    def pair_state(rr, p):
        z = jnp.zeros((RWKV_HEAD, RWKV_HEAD), F32)
        return jnp.concatenate([jnp.concatenate([s0_ref[rr, 2 * p], z], axis=1),
                                jnp.concatenate([z, s0_ref[rr, 2 * p + 1]], axis=1)], axis=0)

    def put_state(rr, p, val):
        sout_ref[rr, 2 * p] = val[:RWKV_HEAD, :RWKV_HEAD]
        sout_ref[rr, 2 * p + 1] = val[RWKV_HEAD:, RWKV_HEAD:]

    if carried:
        @pl.when(step == 0)
        def _():
            for rr in range(gb):
                for p in range(RWKV_PAIRS):
                    s_scr[rr, p] = pair_state(rr, p)

    tril = _tril3(C, C)
    lane = lax.broadcasted_iota(jnp.int32, (1, LANES), 1)
    m_lo = lane < RWKV_HEAD
    rc, cc = _iota2(C, C2)
    tl = cc & (C - 1)
    strict_pair = rc > tl
    incl_pair = rc >= tl

    def stack(x):
        return jnp.concatenate([jnp.where(m_lo, x, 0.0), jnp.where(m_lo, 0.0, x)], axis=0)

    pairs = []
    for rr in range(gb):
        for j in range(gt):
            rows = slice(j * C, (j + 1) * C)
            for p in range(RWKV_PAIRS):
                sl = slice(p * LANES, (p + 1) * LANES)
                pairs.append(dict(rr=rr, j=j, p=p, rows=rows, sl=sl))
    for pr in pairs:
        pr["gc"] = _cumsum_rows(tril, ld_ref[pr["rr"], pr["rows"], pr["sl"]], "nn")
    for pr in pairs:
        at_ref = (pr["rr"], pr["rows"], pr["sl"])
        gc = pr["gc"]
        k = k_ref[at_ref]
        b = b_ref[at_ref]
        glast = gc[C - 1:C, :]
        e_neg = jnp.exp(-gc)
        e_end = jnp.exp(glast - gc)
        rt = r_ref[at_ref] * jnp.exp(gc)
        at = -kn_ref[at_ref] * jnp.exp(gc - ld_ref[at_ref])
        at_s = stack(at)
        pr.update(rt=rt, at_s=at_s, v_s=stack(v_ref[at_ref]).astype(BF16),
                  kh_s=stack(k * e_end), bh_s=stack(b * e_end).astype(BF16), eg=jnp.exp(glast))
        g = _bdot_nt(jnp.concatenate([at, rt], axis=0),
                     jnp.concatenate([stack(k * e_neg), stack(b * e_neg)], axis=0))
        ab = g[:C, C2:]
        pr["low"] = -jnp.concatenate([ab, ab], axis=0)
        pr["rb"] = jnp.where(incl_pair, g[C:, C2:], 0.0).astype(BF16)
        pr["rk"] = jnp.where(incl_pair, g[C:, :C2], 0.0).astype(BF16)
        pr["ak"] = jnp.where(strict_pair, g[:C, :C2], 0.0)
    for pr in pairs:
        kv = jnp.dot(jnp.concatenate([pr["ak"].astype(BF16), pr["rk"]], axis=0), pr["v_s"], preferred_element_type=F32)
        pr["rkv"] = kv[C:]
        pr["rhs"] = jnp.concatenate([pr["at_s"], stack(kv[:C])], axis=1).astype(BF16)

    tinvs = _tri_inv_many([pr["low"] for pr in pairs], C2, C, half_rows=False)
    sols = [jnp.dot(t, pr["rhs"], preferred_element_type=F32).astype(BF16) for t, pr in zip(tinvs, pairs)]
    for pr, sol in zip(pairs, sols):
        a2_s = sol[:, :LANES]
        u2_s = sol[:, LANES:]
        rb_sol = jnp.dot(pr["rb"], sol, preferred_element_type=F32)
        pr["r2"] = pr["rt"] + rb_sol[:, :LANES]
        pr["y2"] = pr["rkv"] + rb_sol[:, LANES:]
        pr["w2"] = _bdot_tn(a2_s, pr["bh_s"])
        pr["n2"] = _bdot_tn(jnp.concatenate([pr["v_s"], u2_s], axis=0),
                            jnp.concatenate([pr["kh_s"].astype(BF16), pr["bh_s"]], axis=0))

    for pr in pairs:
        rr, p = pr["rr"], pr["p"]
        st = s_scr[rr, p] if carried else pair_state(rr, p)
        y_ref[rr, pr["rows"], pr["sl"]] = _bdot_nt(pr["r2"], st) + pr["y2"]
        new = st * pr["eg"] + (_bdot(st, pr["w2"]) + pr["n2"])
        if carried:
            s_scr[rr, p] = new
        else:
            put_state(rr, p, new)

    if carried:
        @pl.when(step == n_steps - 1)
        def _():
            for rr in range(gb):
                for p in range(RWKV_PAIRS):
                    put_state(rr, p, s_scr[rr, p])


def _rwkv_rec(r, k, v, ld, kn, bvec, s0, chunk, gb, gt):
    b, t, _ = r.shape
    span = gt * chunk
    n_steps = t // span
    assert b % gb == 0 and t % span == 0 and (gb == 1 or n_steps == 1)
    kern = functools.partial(_rwkv_rec_kernel, chunk=chunk, gb=gb, gt=gt, n_steps=n_steps)
    tok = pl.BlockSpec((gb, span, D_MODEL), lambda i, c: (i, c, 0))
    st = pl.BlockSpec((gb, RWKV_HEADS, RWKV_HEAD, RWKV_HEAD), lambda i, c: (i, 0, 0, 0))
    return pl.pallas_call(
        kern,
        grid=(b // gb, n_steps),
        in_specs=[tok] * 6 + [st],
        out_specs=[tok, st],
        out_shape=[
            jax.ShapeDtypeStruct((b, t, D_MODEL), F32),
            jax.ShapeDtypeStruct((b, RWKV_HEADS, RWKV_HEAD, RWKV_HEAD), F32),
        ],
        scratch_shapes=[pltpu.VMEM((gb if n_steps * gt > 1 else 1, RWKV_PAIRS, LANES, LANES), F32)],
        compiler_params=pltpu.CompilerParams(
            dimension_semantics=("parallel", "arbitrary"), vmem_limit_bytes=VMEM_LIMIT_BYTES),
        name="rwkv_rec",
    )(r, k, v, ld, kn, bvec, s0)


def _rwkv_out_kernel(y_ref, z_ref, bonus_ref, x_ref, lw_ref, lb_ref, avg_ref, w_ref, fw_ref, out_ref):
    y = y_ref[...]
    avg = avg_ref[...]
    mean = _head_sum(y, avg)
    yc = y - mean
    var = _head_sum(yc * yc, avg)
    gn = yc * lax.rsqrt(var + GN_EPS) * lw_ref[...] + lb_ref[...]
    o = (gn + bonus_ref[...]) * _silu(z_ref[...])
    x2 = x_ref[...] + _bdot(o, w_ref[...])
    out_ref[...] = _rms(x2, fw_ref[...])


def _rwkv_out(y2d, z2d, bonus2d, x2d, lnx_w, lnx_b, avg_bd, w_o, fw, tm):
    m = x2d.shape[0]
    tok = pl.BlockSpec((tm, D_MODEL), lambda i: (i, 0))
    row = pl.BlockSpec((1, D_MODEL), lambda i: (0, 0))
    return pl.pallas_call(
        _rwkv_out_kernel,
        grid=(m // tm,),
        in_specs=[tok, tok, tok, tok, row, row,
                  pl.BlockSpec((LANES, LANES), lambda i: (0, 0)),
                  pl.BlockSpec((D_MODEL, D_MODEL), lambda i: (0, 0)), row],
        out_specs=tok,
        out_shape=jax.ShapeDtypeStruct((m, D_MODEL), F32),
        compiler_params=pltpu.CompilerParams(
            dimension_semantics=("parallel",), vmem_limit_bytes=VMEM_LIMIT_BYTES),
        name="rwkv_out",
    )(y2d, z2d, bonus2d, x2d, lnx_w, lnx_b, avg_bd, w_o, fw)


def _pick_tile(m, cap):
    t = min(m, cap)
    while m % t:
        t //= 2
    return t


def _step_groups(b, t, chunk, chunks_per_step):
    n_chunks = t // chunk
    if n_chunks > 1:
        return 1, _pick_tile(n_chunks, chunks_per_step)
    return _pick_tile(b, ROWS_PER_STEP), 1


def _trunk(x, gdn_s, gdn_conv, rwkv_s, rwkv_shift, p, gdn_chunk, rwkv_chunk):
    b, t, _ = x.shape
    m = b * t
    x2d = x.reshape(m, D_MODEL)
    prev8 = jnp.pad(gdn_conv, ((0, 0), (SUBLANES - (CONV_W - 1), 0), (0, 0)))
    act, ba, last8 = _gdn_in(x2d, p["nw0"], p["w_in"], p["w_ba"], prev8, p["conv_wt"], t)
    conv_new = last8.reshape(b, -1, SUBLANES, GDN_CONV_DIM)[:, -1, SUBLANES - (CONV_W - 1):, :]
    o, s_gdn = _gdn_chunk(act.reshape(b, t, GDN_QKVZ_DIM), ba.reshape(b, t, LANES), gdn_s,
                          p["alog_row"], p["dtb_row"], gdn_chunk, *_step_groups(b, t, gdn_chunk, GDN_CHUNKS_PER_STEP))
    tm2 = _pick_tile(m, OUT_TM)
    x1 = _gdn_out(o.reshape(m, GDN_VALUE_DIM), act, x2d, p["gn_w"], p["w_out"], tm2)
    if t >= RWKV_TM:
        tb, tt = 1, RWKV_TM
    elif t * b <= RWKV_TM:
        tb, tt = b, t
    else:
        tb, tt = RWKV_TM // t, t
    r, k, v, ld, kn, bvec, z, bonus, last = _rwkv_in(
        x1.reshape(b, t, D_MODEL), rwkv_shift.reshape(b, 1, D_MODEL), p["nw1"], p["mu"],
        p["wr"], p["wk"], p["wv"], p["wz"], p["w0"], p["w1"], p["w2"], p["a0"], p["a1"], p["a2"],
        p["k_k"], p["k_a"], p["r_k"], p["ones_bd"], tb, tt)
    sh3 = lambda a: a.reshape(b, t, D_MODEL)
    y, s_rwkv = _rwkv_rec(sh3(r), sh3(k), sh3(v), sh3(ld), sh3(kn), sh3(bvec), rwkv_s,
                          rwkv_chunk, *_step_groups(b, t, rwkv_chunk, RWKV_CHUNKS_PER_STEP))
    out = _rwkv_out(y.reshape(m, D_MODEL), z, bonus, x1, p["lnx_w"], p["lnx_b"], p["avg_bd"], p["w_o"], p["fw"], tm2)
    return (out.reshape(b, t, D_MODEL), s_gdn, conv_new, s_rwkv, last.reshape(b, D_MODEL))


def kernel(x_prompt, x_sample, state_gdn, state_gdn_conv, state_rwkv, state_rwkv_shift, meta_tokens, norm_w, final_norm_w, gdn_w_in, gdn_conv_w, gdn_a_log, gdn_dt_bias, gdn_norm_w, gdn_w_out, rwkv_mu, rwkv_w_rkvz, rwkv_w0, rwkv_w1, rwkv_w2, rwkv_a0, rwkv_a1, rwkv_a2, rwkv_k_k, rwkv_k_a, rwkv_r_k, rwkv_lnx_w, rwkv_lnx_b, rwkv_w_o):
    assert norm_w.shape[0] == 2 and gdn_w_in.shape[0] == 1 and rwkv_mu.shape[0] == 1
    row = lambda a: a.reshape(1, -1).astype(F32)
    w_in = gdn_w_in[0]
    w_ba = jnp.pad(w_in[:, GDN_QKVZ_DIM:], ((0, 0), (0, LANES - 2 * GDN_V_HEADS)))
    gate_row = lambda a: jnp.pad(a.astype(F32), (GDN_V_HEADS, LANES - 2 * GDN_V_HEADS)).reshape(1, LANES)
    rb, cb = jnp.arange(LANES)[:, None] // RWKV_HEAD, jnp.arange(LANES)[None, :] // RWKV_HEAD
    bd = (rb == cb)
    p = {
        "nw0": row(norm_w[0]), "nw1": row(norm_w[1]), "fw": row(final_norm_w),
        "w_in": w_in.astype(BF16), "w_ba": w_ba.astype(BF16),
        "conv_wt": gdn_conv_w[0].T.astype(F32),
        "alog_row": gate_row(gdn_a_log[0]), "dtb_row": gate_row(gdn_dt_bias[0]),
        "gn_w": row(gdn_norm_w[0]), "w_out": gdn_w_out[0].astype(BF16),
        "mu": rwkv_mu[0].astype(F32),
        "wr": rwkv_w_rkvz[0, 0].astype(BF16), "wk": rwkv_w_rkvz[0, 1].astype(BF16),
        "wv": rwkv_w_rkvz[0, 2].astype(BF16), "wz": rwkv_w_rkvz[0, 3].astype(BF16),
        "w0": row(rwkv_w0[0]), "w1": rwkv_w1[0].astype(BF16), "w2": rwkv_w2[0].astype(BF16),
        "a0": row(rwkv_a0[0]), "a1": rwkv_a1[0].astype(BF16), "a2": rwkv_a2[0].astype(BF16),
        "k_k": row(rwkv_k_k[0]), "k_a": row(rwkv_k_a[0]), "r_k": row(rwkv_r_k[0]),
        "lnx_w": row(rwkv_lnx_w[0]), "lnx_b": row(rwkv_lnx_b[0]), "w_o": rwkv_w_o[0].astype(BF16),
        "ones_bd": bd.astype(BF16), "avg_bd": (bd.astype(F32) / RWKV_HEAD).astype(BF16),
    }
    bp = x_prompt.shape[0]
    zeros = lambda *s: jnp.zeros(s, F32)
    _, m_gdn, m_conv, m_rwkv, m_shift = _trunk(
        meta_tokens.astype(F32)[None], zeros(1, GDN_V_HEADS, GDN_HEAD, GDN_HEAD), zeros(1, CONV_W - 1, GDN_CONV_DIM),
        zeros(1, RWKV_HEADS, RWKV_HEAD, RWKV_HEAD), zeros(1, D_MODEL), p, N_META, N_META)
    rep = lambda a: jnp.broadcast_to(a, (bp,) + a.shape[1:])
    y_p, p_gdn, p_conv, p_rwkv, p_shift = _trunk(
        x_prompt, rep(m_gdn), rep(m_conv), rep(m_rwkv), rep(m_shift), p, GDN_CHUNK, RWKV_CHUNK)
    ts = x_sample.shape[1]
    y_s, s_gdn, s_conv, s_rwkv, s_shift = _trunk(
        x_sample, state_gdn[0], state_gdn_conv[0], state_rwkv[0], state_rwkv_shift[0], p, ts, ts)
    return (y_p, y_s, p_gdn[None], p_conv[None], p_rwkv[None], p_shift[None],
            s_gdn[None], s_conv[None], s_rwkv[None], s_shift[None])
```

```python
import functools

import jax
import jax.numpy as jnp
from jax import lax
from jax.experimental import pallas as pl
from jax.experimental.pallas import tpu as pltpu

F32 = jnp.float32
BF16 = jnp.bfloat16

D_MODEL = 1024
N_META = 16
GDN_QK_HEADS = 8
GDN_V_HEADS = 16
GDN_HEAD = 128
GDN_KEY_DIM = GDN_QK_HEADS * GDN_HEAD
GDN_VALUE_DIM = GDN_V_HEADS * GDN_HEAD
GDN_CONV_DIM = 2 * GDN_KEY_DIM + GDN_VALUE_DIM
GDN_QKVZ_DIM = GDN_CONV_DIM + GDN_VALUE_DIM
CONV_W = 4
GDN_SLAB = 256
GDN_ROWS = 128
GDN_TM = 512
GDN_TM_ROWS8 = 256
RWKV_SLAB = 256
RWKV_TM = 512
RWKV_TM_SHORT = 256
OUT_TM = 512
OUT_SLAB = 256
GDN_CHUNK = 64
RWKV_HEAD = 64
RWKV_HEADS = D_MODEL // RWKV_HEAD
RWKV_PAIRS = RWKV_HEADS // 2
RWKV_CHUNK = 64
LORA = 64
RMS_EPS = 1e-6
L2_EPS = 1e-6
GN_EPS = 64e-5
DECAY_SCALE = 0.6065306597126334
LOG2_E = 1.4426950408889634
LANES = 128
SUBLANES = 8
BF16_ROWS = 16
VMEM_LIMIT_BYTES = 56 * 1024 * 1024
GDN_CHUNKS_PER_STEP = 4
RWKV_CHUNKS_PER_STEP = 4
ROWS_PER_STEP = 8


def _bdot(a, b):
    return jnp.dot(a.astype(BF16), b.astype(BF16), preferred_element_type=F32)


def _bdot_nt(a, b):
    return lax.dot_general(a.astype(BF16), b.astype(BF16), (((1,), (1,)), ((), ())), preferred_element_type=F32)


def _bdot_tn(a, b):
    return lax.dot_general(a.astype(BF16), b.astype(BF16), (((0,), (0,)), ((), ())), preferred_element_type=F32)


def _split3(a):
    hi = a.astype(BF16).astype(F32)
    r1 = a - hi
    mid = r1.astype(BF16).astype(F32)
    return jnp.concatenate([hi, mid, r1 - mid], axis=0).astype(BF16)


def _tril3(n_out, c):
    r, k = _iota2(n_out, 3 * c)
    k = jnp.where(k >= 2 * c, k - 2 * c, jnp.where(k >= c, k - c, k))
    return jnp.where((r & (c - 1)) >= k, 1.0, 0.0).astype(BF16)


def _cumsum_rows(m3, x, dims):
    x3 = _split3(x)
    if dims == "nn":
        return jnp.dot(m3, x3, preferred_element_type=F32)
    return lax.dot_general(x3, m3, (((0,), (1,)), ((), ())), preferred_element_type=F32)


def _rms(x, w):
    return x * lax.rsqrt(jnp.mean(x * x, axis=-1, keepdims=True) + RMS_EPS) * w


def _sigmoid(x):
    return 1.0 / (1.0 + jnp.exp2(x * -LOG2_E))


def _silu(x):
    return x * _sigmoid(x)


def _softplus(x):
    return jnp.maximum(x, 0.0) + jnp.log1p(jnp.exp(-jnp.abs(x)))


def _iota2(n, m):
    return lax.broadcasted_iota(jnp.int32, (n, m), 0), lax.broadcasted_iota(jnp.int32, (n, m), 1)


def _log2(n):
    l = n.bit_length() - 1
    assert (1 << l) == n, n
    return l


def _tri_inv_many(lows, n, blk, half_rows):
    row, col = _iota2(n, n)
    eye = jnp.where(row == col, 1.0, 0.0).astype(BF16)
    negs = [(-low).astype(BF16) for low in lows]
    d = lambda a, b: jnp.dot(a, b, preferred_element_type=F32).astype(BF16)
    s = 1
    ts = None
    while s < blk:
        sh = _log2(s)
        if half_rows and s >= BF16_ROWS:
            hr, hc = _iota2(n // 2, n)
            row_blk = ((hr >> sh) << 1) + 1
            m = (row_blk >> 1 == (hc >> sh) >> 1) & (((hc >> sh) & 1) == 0)
            lower = lambda t: jnp.concatenate([t[(2 * j + 1) * s:(2 * j + 2) * s] for j in range(n // (2 * s))], axis=0)
            tl = [lower(t) for t in ts]
            xs = [d(t, ng) for t, ng in zip(tl, negs)]
            zs = [d(x, t) for x, t in zip(xs, ts)]
            new = [jnp.where(m, z, t) for z, t in zip(zs, tl)]
            ts = [jnp.concatenate([piece for j in range(n // (2 * s))
                                   for piece in (t[2 * j * s:(2 * j + 1) * s], nw[j * s:(j + 1) * s])], axis=0)
                  for t, nw in zip(ts, new)]
        else:
            sub_r = row >> sh
            sub_c = col >> sh
            m = ((sub_r >> 1) == (sub_c >> 1)) & ((sub_r & 1) == 1) & ((sub_c & 1) == 0)
            if ts is None:
                ts = [jnp.where(m, ng, eye) for ng in negs]
            else:
                xs = [d(t, ng) for t, ng in zip(ts, negs)]
                zs = [d(x, t) for x, t in zip(xs, ts)]
                ts = [jnp.where(m, z, t) for z, t in zip(zs, ts)]
        s *= 2
    return [eye for _ in lows] if ts is None else ts


def _gdn_in_kernel(x_ref, nw_ref, w_ref, wba_ref, prev_ref, cw_ref, act_ref, ba_ref, last_ref, xn_scr, tail_scr, u_scr,
                   *, tm, groups, tiles_per_row):
    i = pl.program_id(0)
    xb = _rms(x_ref[...], nw_ref[...]).astype(BF16)
    xn_scr[...] = xb
    ba_ref[...] = jnp.dot(xb, wba_ref[...], preferred_element_type=F32)

    if groups == 1:
        @pl.when(i == 0)
        def _():
            tail_scr[...] = jnp.zeros(tail_scr.shape, F32)

    col = lambda c: slice(c * GDN_SLAB, (c + 1) * GDN_SLAB)
    n_conv = GDN_CONV_DIM // GDN_SLAB
    n_key = GDN_KEY_DIM // GDN_SLAB
    n_all = GDN_QKVZ_DIM // GDN_SLAB
    in_scratch = lambda c: groups == 1 and c < n_conv

    def l2n(a, scale):
        parts = []
        for h in range(GDN_SLAB // GDN_HEAD):
            ah = a[:, h * GDN_HEAD:(h + 1) * GDN_HEAD]
            parts.append(ah * (lax.rsqrt(jnp.sum(ah * ah, axis=-1, keepdims=True) + L2_EPS) * scale))
        return jnp.concatenate(parts, axis=1)

    def finish(a, c):
        if c < n_key:
            return l2n(a, GDN_HEAD ** -0.5)
        return l2n(a, 1.0) if c < 2 * n_key else a

    def issue(c):
        u = jnp.dot(xn_scr[...], w_ref[:, col(c)], preferred_element_type=F32)
        if in_scratch(c):
            u_scr[c % 2, SUBLANES:SUBLANES + tm, :] = u
            return None
        return u

    def complete(c, u):
        cs = col(c)
        if c >= n_conv:
            act_ref[:, cs] = _silu(u).astype(BF16)
        elif groups == 1:
            buf = c % 2
            u_scr[buf, 0:SUBLANES, :] = jnp.where((i % tiles_per_row) == 0, prev_ref[0, :, cs], tail_scr[:, cs])
            end = u_scr[buf, tm:tm + SUBLANES, :]
            tail_scr[:, cs] = end
            last_ref[0, :, cs] = end
            nr = min(GDN_ROWS, tm)
            for rb in range(tm // nr):
                r0 = SUBLANES + rb * nr
                y = u_scr[buf, r0:r0 + nr, :] * cw_ref[CONV_W - 1:CONV_W, cs]
                for s in range(1, CONV_W):
                    y = y + u_scr[buf, r0 - s:r0 - s + nr, :] * cw_ref[CONV_W - 1 - s:CONV_W - s, cs]
                act_ref[rb * nr:(rb + 1) * nr, cs] = finish(_silu(y), c).astype(BF16)
        else:
            rows = lax.broadcasted_iota(jnp.int32, (tm, 1), 0)
            t_in = rows & (SUBLANES - 1)
            prev = prev_ref[:, :, cs].reshape(tm, GDN_SLAB)
            y = u * cw_ref[CONV_W - 1:CONV_W, cs]
            for s in range(1, CONV_W):
                shifted = jnp.where(t_in < s, pltpu.roll(prev, tm - SUBLANES + s, axis=0), pltpu.roll(u, s, axis=0))
                y = y + shifted * cw_ref[CONV_W - 1 - s:CONV_W - s, cs]
            last_ref[:, :, cs] = u.reshape(groups, SUBLANES, GDN_SLAB)
            act_ref[:, cs] = finish(_silu(y), c).astype(BF16)

    conv_slabs, z_slabs = list(range(n_conv)), list(range(n_conv, n_all))
    order = []
    while conv_slabs or z_slabs:
        order += conv_slabs[:2]
        conv_slabs = conv_slabs[2:]
        order += z_slabs[:1]
        z_slabs = z_slabs[1:]
    pending = issue(order[0])
    for pos, c in enumerate(order):
        u = pending
        if pos + 1 < len(order):
            pending = issue(order[pos + 1])
        complete(c, u)


def _gdn_in(x2d, nw, w_in, w_ba, prev8, conv_wt, t):
    m = x2d.shape[0]
    if t == SUBLANES:
        tm = _pick_tile(m, GDN_TM_ROWS8)
        groups, tiles_per_row = tm // SUBLANES, 1
        before = pl.BlockSpec((groups, SUBLANES, GDN_CONV_DIM), lambda i: (i, 0, 0))
    else:
        tm = _pick_tile(t, GDN_TM)
        groups, tiles_per_row = 1, t // tm
        assert tm % SUBLANES == 0 and tm > SUBLANES and tm % min(GDN_ROWS, tm) == 0
        before = pl.BlockSpec((1, SUBLANES, GDN_CONV_DIM), lambda i: (i // tiles_per_row, 0, 0))
    kern = functools.partial(_gdn_in_kernel, tm=tm, groups=groups, tiles_per_row=tiles_per_row)
    return pl.pallas_call(
        kern,
        grid=(m // tm,),
        in_specs=[
            pl.BlockSpec((tm, D_MODEL), lambda i: (i, 0)),
            pl.BlockSpec((1, D_MODEL), lambda i: (0, 0)),
            pl.BlockSpec((D_MODEL, GDN_QKVZ_DIM), lambda i: (0, 0)),
            pl.BlockSpec((D_MODEL, LANES), lambda i: (0, 0)),
            before,
            pl.BlockSpec((CONV_W, GDN_CONV_DIM), lambda i: (0, 0)),
        ],
        out_specs=[
            pl.BlockSpec((tm, GDN_QKVZ_DIM), lambda i: (i, 0)),
            pl.BlockSpec((tm, LANES), lambda i: (i, 0)),
            pl.BlockSpec((groups, SUBLANES, GDN_CONV_DIM), lambda i: (i, 0, 0)),
        ],
        out_shape=[
            jax.ShapeDtypeStruct((m, GDN_QKVZ_DIM), BF16),
            jax.ShapeDtypeStruct((m, LANES), F32),
            jax.ShapeDtypeStruct((m // tm * groups, SUBLANES, GDN_CONV_DIM), F32),
        ],
        scratch_shapes=[pltpu.VMEM((tm, D_MODEL), BF16), pltpu.VMEM((SUBLANES, GDN_CONV_DIM), F32),
                        pltpu.VMEM((2, tm + SUBLANES, GDN_SLAB), F32)],
        compiler_params=pltpu.CompilerParams(dimension_semantics=("arbitrary",), vmem_limit_bytes=VMEM_LIMIT_BYTES),
        name="gdn_in",
    )(x2d, nw, w_in, w_ba, prev8, conv_wt)


def _gdn_chunk_kernel(act_ref, ba_ref, s0_ref, alog_ref, dtb_ref, o_ref, sout_ref, s_scr, *, chunk, gb, gt, n_steps):
    step = pl.program_id(1)
    C = chunk
    C2 = 2 * C
    carried = n_steps * gt > 1

    if carried:
        @pl.when(step == 0)
        def _():
            s_scr[...] = s0_ref[...]

    tril = _tril3(C, C)
    tril2 = _tril3(C2, C)
    row, col = _iota2(C2, C2)
    causal = ((row >= C) == (col >= C)) & (row >= col)
    top = lax.broadcasted_iota(jnp.int32, (C2, 1), 0) < C
    left = lax.broadcasted_iota(jnp.int32, (1, C2), 1) < C

    pairs = []
    for r in range(gb):
        ba = ba_ref[r]
        beta_l = _sigmoid(ba)
        g_l = -jnp.exp(alog_ref[...]) * _softplus(ba + dtb_ref[...])
        for j in range(gt):
            rows = slice(j * C, (j + 1) * C)
            beta_all = beta_l[rows]
            gc_all = _cumsum_rows(tril, g_l[rows], "nn")
            gct_all = _cumsum_rows(tril2, g_l[rows], "tn")
            for qh in range(GDN_QK_HEADS):
                q = act_ref[r, rows, qh * GDN_HEAD:(qh + 1) * GDN_HEAD].astype(F32)
                k = act_ref[r, rows, GDN_KEY_DIM + qh * GDN_HEAD:GDN_KEY_DIM + (qh + 1) * GDN_HEAD].astype(F32)
                h0 = 2 * qh
                h1 = h0 + 1
                a0, a1 = GDN_V_HEADS + h0, GDN_V_HEADS + h1
                beta_s = jnp.concatenate([beta_all[:, h0:h0 + 1], beta_all[:, h1:h1 + 1]], axis=0)
                gc_s = jnp.concatenate([gc_all[:, a0:a0 + 1], gc_all[:, a1:a1 + 1]], axis=0)
                gc_row = jnp.where(left, gct_all[a0:a0 + 1, :], gct_all[a1:a1 + 1, :])
                glast0 = gc_all[C - 1:C, a0:a0 + 1]
                glast1 = gc_all[C - 1:C, a1:a1 + 1]
                glast_s = jnp.where(top, glast0, glast1)
                decay = jnp.exp(jnp.where(causal, gc_s - gc_row, -jnp.inf))
                k_s = jnp.concatenate([k, k], axis=0)
                q_s = jnp.concatenate([q, q], axis=0)
                v0 = 2 * GDN_KEY_DIM + h0 * GDN_HEAD
                v_s = jnp.concatenate([act_ref[r, rows, v0:v0 + GDN_HEAD],
                                       act_ref[r, rows, v0 + GDN_HEAD:v0 + 2 * GDN_HEAD]], axis=0).astype(F32)
                egc = jnp.exp(gc_s)
                kb_s = k_s * beta_s
                pairs.append(dict(
                    r=r, j=j, h0=h0, h1=h1,
                    low=_bdot_nt(kb_s, k_s) * decay,
                    rhs=jnp.concatenate([v_s * beta_s, kb_s * egc], axis=1).astype(BF16),
                    attn=_bdot_nt(q_s, k_s) * decay, qd_s=q_s * egc, kd_s=k_s * jnp.exp(glast_s - gc_s),
                    eg0=jnp.exp(glast0), eg1=jnp.exp(glast1)))

    tinvs = _tri_inv_many([pr["low"] for pr in pairs], C2, C, half_rows=True)
    sols = [jnp.dot(t, pr["rhs"], preferred_element_type=F32) for t, pr in zip(tinvs, pairs)]

    def state_in(r, h):
        return s_scr[r, h] if carried else s0_ref[r, h]

    def state_out(r, h, val):
        if carried:
            s_scr[r, h] = val
        else:
            sout_ref[r, h] = val

    for r in range(gb):
        for j in range(gt):
            grp = [(pr, sol) for pr, sol in zip(pairs, sols) if pr["r"] == r and pr["j"] == j]
            rows = slice(j * C, (j + 1) * C)
            sts, wqs = [], []
            for pr, sol in grp:
                w_s = sol[:, GDN_HEAD:]
                st0 = state_in(r, pr["h0"])
                st1 = state_in(r, pr["h1"])
                sts.append((st0, st1))
                wqs.append((_bdot(jnp.concatenate([w_s[:C], pr["qd_s"][:C]], axis=0), st0),
                            _bdot(jnp.concatenate([w_s[C:], pr["qd_s"][C:]], axis=0), st1)))
            for (pr, sol), (st0, st1), (wq0, wq1) in zip(grp, sts, wqs):
                h0, h1 = pr["h0"], pr["h1"]
                u_s = sol[:, :GDN_HEAD]
                vn0 = u_s[:C] - wq0[:C]
                vn1 = u_s[C:] - wq1[:C]
                o_s = _bdot(pr["attn"], jnp.concatenate([vn0, vn1], axis=0))
                o_ref[r, rows, h0 * GDN_HEAD:(h0 + 1) * GDN_HEAD] = (o_s[:C] + wq0[C:]).astype(BF16)
                o_ref[r, rows, h1 * GDN_HEAD:(h1 + 1) * GDN_HEAD] = (o_s[C:] + wq1[C:]).astype(BF16)
                state_out(r, h0, st0 * pr["eg0"] + _bdot_tn(pr["kd_s"][:C], vn0))
                state_out(r, h1, st1 * pr["eg1"] + _bdot_tn(pr["kd_s"][C:], vn1))

    if carried:
        @pl.when(step == n_steps - 1)
        def _():
            sout_ref[...] = s_scr[...]


def _gdn_chunk(act, ba, s0, alog_row, dtb_row, chunk, gb, gt):
    b, t, _ = act.shape
    span = gt * chunk
    n_steps = t // span
    assert b % gb == 0 and t % span == 0 and (gb == 1 or n_steps == 1)
    kern = functools.partial(_gdn_chunk_kernel, chunk=chunk, gb=gb, gt=gt, n_steps=n_steps)
    state = pl.BlockSpec((gb, GDN_V_HEADS, GDN_HEAD, GDN_HEAD), lambda i, c: (i, 0, 0, 0))
    return pl.pallas_call(
        kern,
        grid=(b // gb, n_steps),
        in_specs=[
            pl.BlockSpec((gb, span, GDN_CONV_DIM), lambda i, c: (i, c, 0)),
            pl.BlockSpec((gb, span, LANES), lambda i, c: (i, c, 0)),
            state,
            pl.BlockSpec((1, LANES), lambda i, c: (0, 0)),
            pl.BlockSpec((1, LANES), lambda i, c: (0, 0)),
        ],
        out_specs=[pl.BlockSpec((gb, span, GDN_VALUE_DIM), lambda i, c: (i, c, 0)), state],
        out_shape=[
            jax.ShapeDtypeStruct((b, t, GDN_VALUE_DIM), BF16),
            jax.ShapeDtypeStruct((b, GDN_V_HEADS, GDN_HEAD, GDN_HEAD), F32),
        ],
        scratch_shapes=[pltpu.VMEM((gb if n_steps * gt > 1 else 1, GDN_V_HEADS, GDN_HEAD, GDN_HEAD), F32)],
        compiler_params=pltpu.CompilerParams(
            dimension_semantics=("parallel", "arbitrary"), vmem_limit_bytes=VMEM_LIMIT_BYTES),
        name="gdn_chunk",
    )(act, ba, s0, alog_row, dtb_row)


def _gdn_out_kernel(o_ref, gate_ref, x_ref, gw_ref, w_ref, y_ref):
    gw = gw_ref[...]
    acc = x_ref[...]
    for c in range(GDN_VALUE_DIM // OUT_SLAB):
        parts = []
        for h in range(c * OUT_SLAB // GDN_HEAD, (c + 1) * OUT_SLAB // GDN_HEAD):
            sl = slice(h * GDN_HEAD, (h + 1) * GDN_HEAD)
            parts.append((_rms(o_ref[:, sl].astype(F32), gw) * gate_ref[:, sl].astype(F32)).astype(BF16))
        acc = acc + jnp.dot(jnp.concatenate(parts, axis=1), w_ref[c * OUT_SLAB:(c + 1) * OUT_SLAB, :],
                            preferred_element_type=F32)
    y_ref[...] = acc


def _gdn_out(o2d, act2d, x2d, gw, w_out, tm):
    m = x2d.shape[0]
    z_block = GDN_CONV_DIM // GDN_VALUE_DIM
    return pl.pallas_call(
        _gdn_out_kernel,
        grid=(m // tm,),
        in_specs=[
            pl.BlockSpec((tm, GDN_VALUE_DIM), lambda i: (i, 0)),
            pl.BlockSpec((tm, GDN_VALUE_DIM), lambda i: (i, z_block)),
            pl.BlockSpec((tm, D_MODEL), lambda i: (i, 0)),
            pl.BlockSpec((1, GDN_HEAD), lambda i: (0, 0)),
            pl.BlockSpec((GDN_VALUE_DIM, D_MODEL), lambda i: (0, 0)),
        ],
        out_specs=pl.BlockSpec((tm, D_MODEL), lambda i: (i, 0)),
        out_shape=jax.ShapeDtypeStruct((m, D_MODEL), F32),
        compiler_params=pltpu.CompilerParams(
            dimension_semantics=("parallel",), vmem_limit_bytes=VMEM_LIMIT_BYTES),
        name="gdn_out",
    )(o2d, act2d, x2d, gw, w_out)


def _head_sum(x, ones_bd):
    return jnp.concatenate([_bdot(x[:, p * LANES:(p + 1) * LANES], ones_bd) for p in range(x.shape[1] // LANES)], axis=1)


def _rwkv_in_kernel(x_ref, xp_ref, sh_ref, nw_ref, mu_ref, wr_ref, wk_ref, wv_ref, wz_ref,
                    w0_ref, w1_ref, w2_ref, a0_ref, a1_ref, a2_ref, kk_ref, ka_ref, rk_ref, ones_ref,
                    r_out, k_out, v_out, ld_out, kn_out, b_out, z_out, bonus_out, last_out, *, tb, tt, n_t):
    t = pl.program_id(1)
    m = tb * tt
    nw = nw_ref[...]
    xn = _rms(x_ref[...].reshape(m, D_MODEL), nw)
    prev_last = _rms(xp_ref[:, SUBLANES - 1:SUBLANES, :], nw)
    first = jnp.where(t == 0, sh_ref[...], prev_last)
    first = jnp.broadcast_to(first, (tb, tt, D_MODEL)).reshape(m, D_MODEL)
    rolled = pltpu.roll(xn, 1, axis=0)
    rows = lax.broadcasted_iota(jnp.int32, (m, 1), 0)
    xprev = jnp.where((rows & (tt - 1)) == 0, first, rolled)
    xx = xprev - xn
    mu = mu_ref[...]

    mixed = [(xn + xx * mu[i:i + 1, :]).astype(BF16) for i in range(6)]
    w_mid = jnp.tanh(jnp.dot(mixed[4], w1_ref[...], preferred_element_type=F32)).astype(BF16)
    a_mid = jnp.dot(mixed[5], a1_ref[...], preferred_element_type=F32).astype(BF16)
    ones_bd = ones_ref[...]
    d = functools.partial(jnp.dot, preferred_element_type=F32)

    def project(cs):
        return (d(mixed[0], wr_ref[:, cs]), d(mixed[1], wk_ref[:, cs]), d(mixed[2], wv_ref[:, cs]),
                d(mixed[3], wz_ref[:, cs]), d(w_mid, w2_ref[:, cs]), d(a_mid, a2_ref[:, cs]))

    n_slabs = D_MODEL // RWKV_SLAB
    col = lambda c: slice(c * RWKV_SLAB, (c + 1) * RWKV_SLAB)
    nxt = project(col(0))
    for c in range(n_slabs):
        cs = col(c)
        r, k, v, z, w_lo, a_lo = nxt
        if c + 1 < n_slabs:
            nxt = project(col(c + 1))
        a = _sigmoid(a0_ref[:, cs] + a_lo)
        kk = k * kk_ref[:, cs]
        kn = kk * lax.rsqrt(_head_sum(kk * kk, ones_bd) + L2_EPS)
        k = k * (1.0 + (a - 1.0) * ka_ref[:, cs])
        r_out[:, cs] = r
        k_out[:, cs] = k
        v_out[:, cs] = v
        z_out[:, cs] = z
        ld_out[:, cs] = -DECAY_SCALE * _sigmoid(w0_ref[:, cs] + w_lo)
        kn_out[:, cs] = kn
        b_out[:, cs] = kn * a
        bonus_out[:, cs] = _head_sum(r * k * rk_ref[:, cs], ones_bd) * v

    @pl.when(t == n_t - 1)
    def _():
        last_out[...] = _rms(x_ref[:, tt - 1:tt, :], nw)


def _rwkv_in(x3d, shift, nw, mu, wr, wk, wv, wz, w0, w1, w2, a0, a1, a2, k_k, k_a, r_k, ones_bd, tb, tt):
    b, t, _ = x3d.shape
    n_t = t // tt
    tpb = tt // SUBLANES
    kern = functools.partial(_rwkv_in_kernel, tb=tb, tt=tt, n_t=n_t)
    row = lambda n: pl.BlockSpec((n, D_MODEL), lambda i, j: (0, 0))
    big = pl.BlockSpec((D_MODEL, D_MODEL), lambda i, j: (0, 0))
    tok = pl.BlockSpec((tb * tt, D_MODEL), lambda i, j: (i * n_t + j, 0))
    m = b * t
    outs = pl.pallas_call(
        kern,
        grid=(b // tb, n_t),
        in_specs=[
            pl.BlockSpec((tb, tt, D_MODEL), lambda i, j: (i, j, 0)),
            pl.BlockSpec((tb, SUBLANES, D_MODEL), lambda i, j: (i, jnp.maximum(j * tpb - 1, 0), 0)),
            pl.BlockSpec((tb, 1, D_MODEL), lambda i, j: (i, 0, 0)),
            row(1), row(6), big, big, big, big,
            row(1), pl.BlockSpec((D_MODEL, LORA), lambda i, j: (0, 0)), pl.BlockSpec((LORA, D_MODEL), lambda i, j: (0, 0)),
            row(1), pl.BlockSpec((D_MODEL, LORA), lambda i, j: (0, 0)), pl.BlockSpec((LORA, D_MODEL), lambda i, j: (0, 0)),
            row(1), row(1), row(1),
            pl.BlockSpec((LANES, LANES), lambda i, j: (0, 0)),
        ],
        out_specs=[tok] * 8 + [pl.BlockSpec((tb, 1, D_MODEL), lambda i, j: (i, 0, 0))],
        out_shape=[jax.ShapeDtypeStruct((m, D_MODEL), F32)] * 8 + [jax.ShapeDtypeStruct((b, 1, D_MODEL), F32)],
        compiler_params=pltpu.CompilerParams(
            dimension_semantics=("parallel", "arbitrary"), vmem_limit_bytes=VMEM_LIMIT_BYTES),
        name="rwkv_in",
    )(x3d, x3d, shift, nw, mu, wr, wk, wv, wz, w0, w1, w2, a0, a1, a2, k_k, k_a, r_k, ones_bd)
    return outs


def _rwkv_rec_kernel(r_ref, k_ref, v_ref, ld_ref, kn_ref, b_ref, s0_ref, y_ref, sout_ref, s_scr,
                     *, chunk, gb, gt, n_steps):
    step = pl.program_id(1)
    C = chunk
    C2 = 2 * C
    carried = n_steps * gt > 1

    def pair_state(rr, p):
        z = jnp.zeros((RWKV_HEAD, RWKV_HEAD), F32)
        return jnp.concatenate([jnp.concatenate([s0_ref[rr, 2 * p], z], axis=1),
                                jnp.concatenate([z, s0_ref[rr, 2 * p + 1]], axis=1)], axis=0)

    def put_state(rr, p, val):
        sout_ref[rr, 2 * p] = val[:RWKV_HEAD, :RWKV_HEAD]
        sout_ref[rr, 2 * p + 1] = val[RWKV_HEAD:, RWKV_HEAD:]

    if carried:
        @pl.when(step == 0)
        def _():
            for rr in range(gb):
                for p in range(RWKV_PAIRS):
                    s_scr[rr, p] = pair_state(rr, p)

    tril = _tril3(C, C)
    lane = lax.broadcasted_iota(jnp.int32, (1, LANES), 1)
    m_lo = lane < RWKV_HEAD
    rc, cc = _iota2(C, C2)
    tl = cc & (C - 1)
    strict_pair = rc > tl
    incl_pair = rc >= tl

    def stack(x):
        return jnp.concatenate([jnp.where(m_lo, x, 0.0), jnp.where(m_lo, 0.0, x)], axis=0)

    pairs = []
    for rr in range(gb):
        for j in range(gt):
            rows = slice(j * C, (j + 1) * C)
            for p in range(RWKV_PAIRS):
                sl = slice(p * LANES, (p + 1) * LANES)
                pairs.append(dict(rr=rr, j=j, p=p, rows=rows, sl=sl))
    for pr in pairs:
        pr["gc"] = _cumsum_rows(tril, ld_ref[pr["rr"], pr["rows"], pr["sl"]], "nn")
    for pr in pairs:
        at_ref = (pr["rr"], pr["rows"], pr["sl"])
        gc = pr["gc"]
        k = k_ref[at_ref]
        b = b_ref[at_ref]
        glast = gc[C - 1:C, :]
        e_neg = jnp.exp(-gc)
        e_end = jnp.exp(glast - gc)
        rt = r_ref[at_ref] * jnp.exp(gc)
        at = -kn_ref[at_ref] * jnp.exp(gc - ld_ref[at_ref])
        at_s = stack(at)
        pr.update(rt=rt, at_s=at_s, v_s=stack(v_ref[at_ref]).astype(BF16),
                  kh_s=stack(k * e_end), bh_s=stack(b * e_end).astype(BF16), eg=jnp.exp(glast))
        g = _bdot_nt(jnp.concatenate([at, rt], axis=0),
                     jnp.concatenate([stack(k * e_neg), stack(b * e_neg)], axis=0))
        ab = g[:C, C2:]
        pr["low"] = -jnp.concatenate([ab, ab], axis=0)
        pr["rb"] = jnp.where(incl_pair, g[C:, C2:], 0.0).astype(BF16)
        pr["rk"] = jnp.where(incl_pair, g[C:, :C2], 0.0).astype(BF16)
        pr["ak"] = jnp.where(strict_pair, g[:C, :C2], 0.0)
    for pr in pairs:
        kv = jnp.dot(jnp.concatenate([pr["ak"].astype(BF16), pr["rk"]], axis=0), pr["v_s"], preferred_element_type=F32)
        pr["rkv"] = kv[C:]
        pr["rhs"] = jnp.concatenate([pr["at_s"], stack(kv[:C])], axis=1).astype(BF16)

    tinvs = _tri_inv_many([pr["low"] for pr in pairs], C2, C, half_rows=False)
    sols = [jnp.dot(t, pr["rhs"], preferred_element_type=F32).astype(BF16) for t, pr in zip(tinvs, pairs)]
    for pr, sol in zip(pairs, sols):
        a2_s = sol[:, :LANES]
        u2_s = sol[:, LANES:]
        rb_sol = jnp.dot(pr["rb"], sol, preferred_element_type=F32)
        pr["r2"] = pr["rt"] + rb_sol[:, :LANES]
        pr["y2"] = pr["rkv"] + rb_sol[:, LANES:]
        pr["w2"] = _bdot_tn(a2_s, pr["bh_s"])
        pr["n2"] = _bdot_tn(jnp.concatenate([pr["v_s"], u2_s], axis=0),
                            jnp.concatenate([pr["kh_s"].astype(BF16), pr["bh_s"]], axis=0))

    for pr in pairs:
        rr, p = pr["rr"], pr["p"]
        st = s_scr[rr, p] if carried else pair_state(rr, p)
        y_ref[rr, pr["rows"], pr["sl"]] = _bdot_nt(pr["r2"], st) + pr["y2"]
        new = st * pr["eg"] + (_bdot(st, pr["w2"]) + pr["n2"])
        if carried:
            s_scr[rr, p] = new
        else:
            put_state(rr, p, new)

    if carried:
        @pl.when(step == n_steps - 1)
        def _():
            for rr in range(gb):
                for p in range(RWKV_PAIRS):
                    put_state(rr, p, s_scr[rr, p])


def _rwkv_rec(r, k, v, ld, kn, bvec, s0, chunk, gb, gt):
    b, t, _ = r.shape
    span = gt * chunk
    n_steps = t // span
    assert b % gb == 0 and t % span == 0 and (gb == 1 or n_steps == 1)
    kern = functools.partial(_rwkv_rec_kernel, chunk=chunk, gb=gb, gt=gt, n_steps=n_steps)
    tok = pl.BlockSpec((gb, span, D_MODEL), lambda i, c: (i, c, 0))
    st = pl.BlockSpec((gb, RWKV_HEADS, RWKV_HEAD, RWKV_HEAD), lambda i, c: (i, 0, 0, 0))
    return pl.pallas_call(
        kern,
        grid=(b // gb, n_steps),
        in_specs=[tok] * 6 + [st],
        out_specs=[tok, st],
        out_shape=[
            jax.ShapeDtypeStruct((b, t, D_MODEL), F32),
            jax.ShapeDtypeStruct((b, RWKV_HEADS, RWKV_HEAD, RWKV_HEAD), F32),
        ],
        scratch_shapes=[pltpu.VMEM((gb if n_steps * gt > 1 else 1, RWKV_PAIRS, LANES, LANES), F32)],
        compiler_params=pltpu.CompilerParams(
            dimension_semantics=("parallel", "arbitrary"), vmem_limit_bytes=VMEM_LIMIT_BYTES),
        name="rwkv_rec",
    )(r, k, v, ld, kn, bvec, s0)


def _rwkv_out_kernel(y_ref, z_ref, bonus_ref, x_ref, lw_ref, lb_ref, avg_ref, w_ref, fw_ref, out_ref):
    y = y_ref[...]
    avg = avg_ref[...]
    mean = _head_sum(y, avg)
    yc = y - mean
    var = _head_sum(yc * yc, avg)
    gn = yc * lax.rsqrt(var + GN_EPS) * lw_ref[...] + lb_ref[...]
    o = (gn + bonus_ref[...]) * _silu(z_ref[...])
    x2 = x_ref[...] + _bdot(o, w_ref[...])
    out_ref[...] = _rms(x2, fw_ref[...])


def _rwkv_out(y2d, z2d, bonus2d, x2d, lnx_w, lnx_b, avg_bd, w_o, fw, tm):
    m = x2d.shape[0]
    tok = pl.BlockSpec((tm, D_MODEL), lambda i: (i, 0))
    row = pl.BlockSpec((1, D_MODEL), lambda i: (0, 0))
    return pl.pallas_call(
        _rwkv_out_kernel,
        grid=(m // tm,),
        in_specs=[tok, tok, tok, tok, row, row,
                  pl.BlockSpec((LANES, LANES), lambda i: (0, 0)),
                  pl.BlockSpec((D_MODEL, D_MODEL), lambda i: (0, 0)), row],
        out_specs=tok,
        out_shape=jax.ShapeDtypeStruct((m, D_MODEL), F32),
        compiler_params=pltpu.CompilerParams(
            dimension_semantics=("parallel",), vmem_limit_bytes=VMEM_LIMIT_BYTES),
        name="rwkv_out",
    )(y2d, z2d, bonus2d, x2d, lnx_w, lnx_b, avg_bd, w_o, fw)


def _pick_tile(m, cap):
    t = min(m, cap)
    while m % t:
        t //= 2
    return t


def _step_groups(b, t, chunk, chunks_per_step):
    n_chunks = t // chunk
    if n_chunks > 1:
        return 1, _pick_tile(n_chunks, chunks_per_step)
    return _pick_tile(b, ROWS_PER_STEP), 1


def _trunk(x, gdn_s, gdn_conv, rwkv_s, rwkv_shift, p, gdn_chunk, rwkv_chunk):
    b, t, _ = x.shape
    m = b * t
    x2d = x.reshape(m, D_MODEL)
    prev8 = jnp.pad(gdn_conv, ((0, 0), (SUBLANES - (CONV_W - 1), 0), (0, 0)))
    act, ba, last8 = _gdn_in(x2d, p["nw0"], p["w_in"], p["w_ba"], prev8, p["conv_wt"], t)
    conv_new = last8.reshape(b, -1, SUBLANES, GDN_CONV_DIM)[:, -1, SUBLANES - (CONV_W - 1):, :]
    o, s_gdn = _gdn_chunk(act.reshape(b, t, GDN_QKVZ_DIM), ba.reshape(b, t, LANES), gdn_s,
                          p["alog_row"], p["dtb_row"], gdn_chunk, *_step_groups(b, t, gdn_chunk, GDN_CHUNKS_PER_STEP))
    tm2 = _pick_tile(m, OUT_TM)
    x1 = _gdn_out(o.reshape(m, GDN_VALUE_DIM), act, x2d, p["gn_w"], p["w_out"], tm2)
    if t >= RWKV_TM:
        tb, tt = 1, RWKV_TM
    elif t * b <= RWKV_TM_SHORT:
        tb, tt = b, t
    else:
        tb, tt = RWKV_TM_SHORT // t, t
    r, k, v, ld, kn, bvec, z, bonus, last = _rwkv_in(
        x1.reshape(b, t, D_MODEL), rwkv_shift.reshape(b, 1, D_MODEL), p["nw1"], p["mu"],
        p["wr"], p["wk"], p["wv"], p["wz"], p["w0"], p["w1"], p["w2"], p["a0"], p["a1"], p["a2"],
        p["k_k"], p["k_a"], p["r_k"], p["ones_bd"], tb, tt)
    sh3 = lambda a: a.reshape(b, t, D_MODEL)
    y, s_rwkv = _rwkv_rec(sh3(r), sh3(k), sh3(v), sh3(ld), sh3(kn), sh3(bvec), rwkv_s,
                          rwkv_chunk, *_step_groups(b, t, rwkv_chunk, RWKV_CHUNKS_PER_STEP))
    out = _rwkv_out(y.reshape(m, D_MODEL), z, bonus, x1, p["lnx_w"], p["lnx_b"], p["avg_bd"], p["w_o"], p["fw"], tm2)
    return (out.reshape(b, t, D_MODEL), s_gdn, conv_new, s_rwkv, last.reshape(b, D_MODEL))


def kernel(x_prompt, x_sample, state_gdn, state_gdn_conv, state_rwkv, state_rwkv_shift, meta_tokens, norm_w, final_norm_w, gdn_w_in, gdn_conv_w, gdn_a_log, gdn_dt_bias, gdn_norm_w, gdn_w_out, rwkv_mu, rwkv_w_rkvz, rwkv_w0, rwkv_w1, rwkv_w2, rwkv_a0, rwkv_a1, rwkv_a2, rwkv_k_k, rwkv_k_a, rwkv_r_k, rwkv_lnx_w, rwkv_lnx_b, rwkv_w_o):
    assert norm_w.shape[0] == 2 and gdn_w_in.shape[0] == 1 and rwkv_mu.shape[0] == 1
    row = lambda a: a.reshape(1, -1).astype(F32)
    w_in = gdn_w_in[0]
    w_ba = jnp.pad(w_in[:, GDN_QKVZ_DIM:], ((0, 0), (0, LANES - 2 * GDN_V_HEADS)))
    gate_row = lambda a: jnp.pad(a.astype(F32), (GDN_V_HEADS, LANES - 2 * GDN_V_HEADS)).reshape(1, LANES)
    rb, cb = jnp.arange(LANES)[:, None] // RWKV_HEAD, jnp.arange(LANES)[None, :] // RWKV_HEAD
    bd = (rb == cb)
    p = {
        "nw0": row(norm_w[0]), "nw1": row(norm_w[1]), "fw": row(final_norm_w),
        "w_in": w_in.astype(BF16), "w_ba": w_ba.astype(BF16),
        "conv_wt": gdn_conv_w[0].T.astype(F32),
        "alog_row": gate_row(gdn_a_log[0]), "dtb_row": gate_row(gdn_dt_bias[0]),
        "gn_w": row(gdn_norm_w[0]), "w_out": gdn_w_out[0].astype(BF16),
        "mu": rwkv_mu[0].astype(F32),
        "wr": rwkv_w_rkvz[0, 0].astype(BF16), "wk": rwkv_w_rkvz[0, 1].astype(BF16),
        "wv": rwkv_w_rkvz[0, 2].astype(BF16), "wz": rwkv_w_rkvz[0, 3].astype(BF16),
        "w0": row(rwkv_w0[0]), "w1": rwkv_w1[0].astype(BF16), "w2": rwkv_w2[0].astype(BF16),
        "a0": row(rwkv_a0[0]), "a1": rwkv_a1[0].astype(BF16), "a2": rwkv_a2[0].astype(BF16),
        "k_k": row(rwkv_k_k[0]), "k_a": row(rwkv_k_a[0]), "r_k": row(rwkv_r_k[0]),
        "lnx_w": row(rwkv_lnx_w[0]), "lnx_b": row(rwkv_lnx_b[0]), "w_o": rwkv_w_o[0].astype(BF16),
        "ones_bd": bd.astype(BF16), "avg_bd": (bd.astype(F32) / RWKV_HEAD).astype(BF16),
    }
    bp = x_prompt.shape[0]
    zeros = lambda *s: jnp.zeros(s, F32)
    _, m_gdn, m_conv, m_rwkv, m_shift = _trunk(
        meta_tokens.astype(F32)[None], zeros(1, GDN_V_HEADS, GDN_HEAD, GDN_HEAD), zeros(1, CONV_W - 1, GDN_CONV_DIM),
        zeros(1, RWKV_HEADS, RWKV_HEAD, RWKV_HEAD), zeros(1, D_MODEL), p, N_META, N_META)
    rep = lambda a: jnp.broadcast_to(a, (bp,) + a.shape[1:])
    y_p, p_gdn, p_conv, p_rwkv, p_shift = _trunk(
        x_prompt, rep(m_gdn), rep(m_conv), rep(m_rwkv), rep(m_shift), p, GDN_CHUNK, RWKV_CHUNK)
    ts = x_sample.shape[1]
    y_s, s_gdn, s_conv, s_rwkv, s_shift = _trunk(
        x_sample, state_gdn[0], state_gdn_conv[0], state_rwkv[0], state_rwkv_shift[0], p, ts, ts)
    return (y_p, y_s, p_gdn[None], p_conv[None], p_rwkv[None], p_shift[None],
            s_gdn[None], s_conv[None], s_rwkv[None], s_shift[None])
```

```python
import functools

import jax
import jax.numpy as jnp
from jax import lax
from jax.experimental import pallas as pl
from jax.experimental.pallas import tpu as pltpu

F32 = jnp.float32
BF16 = jnp.bfloat16

D_MODEL = 1024
N_META = 16
GDN_QK_HEADS = 8
GDN_V_HEADS = 16
GDN_HEAD = 128
GDN_KEY_DIM = GDN_QK_HEADS * GDN_HEAD
GDN_VALUE_DIM = GDN_V_HEADS * GDN_HEAD
GDN_CONV_DIM = 2 * GDN_KEY_DIM + GDN_VALUE_DIM
GDN_QKVZ_DIM = GDN_CONV_DIM + GDN_VALUE_DIM
CONV_W = 4
GDN_SLAB = 256
GDN_ROWS = 128
GDN_TM = 512
GDN_TM_ROWS8 = 256
RWKV_SLAB = 256
RWKV_TM = 512
RWKV_TM_SHORT = 256
OUT_TM = 512
OUT_SLAB = 256
GDN_CHUNK = 64
RWKV_HEAD = 64
RWKV_HEADS = D_MODEL // RWKV_HEAD
RWKV_PAIRS = RWKV_HEADS // 2
RWKV_CHUNK = 64
LORA = 64
RMS_EPS = 1e-6
L2_EPS = 1e-6
GN_EPS = 64e-5
DECAY_SCALE = 0.6065306597126334
LOG2_E = 1.4426950408889634
LANES = 128
SUBLANES = 8
BF16_ROWS = 16
VMEM_LIMIT_BYTES = 56 * 1024 * 1024
GDN_CHUNKS_PER_STEP = 8
RWKV_CHUNKS_PER_STEP = 8
ROWS_PER_STEP = 8


def _bdot(a, b):
    return jnp.dot(a.astype(BF16), b.astype(BF16), preferred_element_type=F32)


def _bdot_nt(a, b):
    return lax.dot_general(a.astype(BF16), b.astype(BF16), (((1,), (1,)), ((), ())), preferred_element_type=F32)


def _bdot_tn(a, b):
    return lax.dot_general(a.astype(BF16), b.astype(BF16), (((0,), (0,)), ((), ())), preferred_element_type=F32)


def _split3(a):
    hi = a.astype(BF16).astype(F32)
    r1 = a - hi
    mid = r1.astype(BF16).astype(F32)
    return jnp.concatenate([hi, mid, r1 - mid], axis=0).astype(BF16)


def _tril3(n_out, c):
    r, k = _iota2(n_out, 3 * c)
    k = jnp.where(k >= 2 * c, k - 2 * c, jnp.where(k >= c, k - c, k))
    return jnp.where((r & (c - 1)) >= k, 1.0, 0.0).astype(BF16)


def _cumsum_rows(m3, x, dims):
    x3 = _split3(x)
    if dims == "nn":
        return jnp.dot(m3, x3, preferred_element_type=F32)
    return lax.dot_general(x3, m3, (((0,), (1,)), ((), ())), preferred_element_type=F32)


def _rms(x, w):
    return x * lax.rsqrt(jnp.mean(x * x, axis=-1, keepdims=True) + RMS_EPS) * w


def _sigmoid(x):
    return 1.0 / (1.0 + jnp.exp2(x * -LOG2_E))


def _silu(x):
    return x * _sigmoid(x)


def _softplus(x):
    return jnp.maximum(x, 0.0) + jnp.log1p(jnp.exp(-jnp.abs(x)))


def _iota2(n, m):
    return lax.broadcasted_iota(jnp.int32, (n, m), 0), lax.broadcasted_iota(jnp.int32, (n, m), 1)


def _log2(n):
    l = n.bit_length() - 1
    assert (1 << l) == n, n
    return l


def _tri_inv_many(lows, n, blk, half_rows):
    row, col = _iota2(n, n)
    eye = jnp.where(row == col, 1.0, 0.0).astype(BF16)
    negs = [(-low).astype(BF16) for low in lows]
    d = lambda a, b: jnp.dot(a, b, preferred_element_type=F32).astype(BF16)
    s = 1
    ts = None
    while s < blk:
        sh = _log2(s)
        if half_rows and s >= BF16_ROWS:
            hr, hc = _iota2(n // 2, n)
            row_blk = ((hr >> sh) << 1) + 1
            m = (row_blk >> 1 == (hc >> sh) >> 1) & (((hc >> sh) & 1) == 0)
            lower = lambda t: jnp.concatenate([t[(2 * j + 1) * s:(2 * j + 2) * s] for j in range(n // (2 * s))], axis=0)
            tl = [lower(t) for t in ts]
            xs = [d(t, ng) for t, ng in zip(tl, negs)]
            zs = [d(x, t) for x, t in zip(xs, ts)]
            new = [jnp.where(m, z, t) for z, t in zip(zs, tl)]
            ts = [jnp.concatenate([piece for j in range(n // (2 * s))
                                   for piece in (t[2 * j * s:(2 * j + 1) * s], nw[j * s:(j + 1) * s])], axis=0)
                  for t, nw in zip(ts, new)]
        else:
            sub_r = row >> sh
            sub_c = col >> sh
            m = ((sub_r >> 1) == (sub_c >> 1)) & ((sub_r & 1) == 1) & ((sub_c & 1) == 0)
            if ts is None:
                ts = [jnp.where(m, ng, eye) for ng in negs]
            else:
                xs = [d(t, ng) for t, ng in zip(ts, negs)]
                zs = [d(x, t) for x, t in zip(xs, ts)]
                ts = [jnp.where(m, z, t) for z, t in zip(zs, ts)]
        s *= 2
    return [eye for _ in lows] if ts is None else ts


def _gdn_in_kernel(x_ref, nw_ref, w_ref, wba_ref, prev_ref, cw_ref, act_ref, ba_ref, last_ref, xn_scr, tail_scr, u_scr,
                   *, tm, groups, tiles_per_row):
    i = pl.program_id(0)
    xb = _rms(x_ref[...], nw_ref[...]).astype(BF16)
    xn_scr[...] = xb
    ba_ref[...] = jnp.dot(xb, wba_ref[...], preferred_element_type=F32)

    if groups == 1:
        @pl.when(i == 0)
        def _():
            tail_scr[...] = jnp.zeros(tail_scr.shape, F32)

    col = lambda c: slice(c * GDN_SLAB, (c + 1) * GDN_SLAB)
    n_conv = GDN_CONV_DIM // GDN_SLAB
    n_key = GDN_KEY_DIM // GDN_SLAB
    n_all = GDN_QKVZ_DIM // GDN_SLAB
    in_scratch = lambda c: groups == 1 and c < n_conv

    def l2n(a, scale):
        parts = []
        for h in range(GDN_SLAB // GDN_HEAD):
            ah = a[:, h * GDN_HEAD:(h + 1) * GDN_HEAD]
            parts.append(ah * (lax.rsqrt(jnp.sum(ah * ah, axis=-1, keepdims=True) + L2_EPS) * scale))
        return jnp.concatenate(parts, axis=1)

    def finish(a, c):
        if c < n_key:
            return l2n(a, GDN_HEAD ** -0.5)
        return l2n(a, 1.0) if c < 2 * n_key else a

    def issue(c):
        u = jnp.dot(xn_scr[...], w_ref[:, col(c)], preferred_element_type=F32)
        if in_scratch(c):
            u_scr[c % 2, SUBLANES:SUBLANES + tm, :] = u
            return None
        return u

    def complete(c, u):
        cs = col(c)
        if c >= n_conv:
            act_ref[:, cs] = _silu(u).astype(BF16)
        elif groups == 1:
            buf = c % 2
            u_scr[buf, 0:SUBLANES, :] = jnp.where((i % tiles_per_row) == 0, prev_ref[0, :, cs], tail_scr[:, cs])
            end = u_scr[buf, tm:tm + SUBLANES, :]
            tail_scr[:, cs] = end
            last_ref[0, :, cs] = end
            nr = min(GDN_ROWS, tm)
            for rb in range(tm // nr):
                r0 = SUBLANES + rb * nr
                y = u_scr[buf, r0:r0 + nr, :] * cw_ref[CONV_W - 1:CONV_W, cs]
                for s in range(1, CONV_W):
                    y = y + u_scr[buf, r0 - s:r0 - s + nr, :] * cw_ref[CONV_W - 1 - s:CONV_W - s, cs]
                act_ref[rb * nr:(rb + 1) * nr, cs] = finish(_silu(y), c).astype(BF16)
        else:
            rows = lax.broadcasted_iota(jnp.int32, (tm, 1), 0)
            t_in = rows & (SUBLANES - 1)
            prev = prev_ref[:, :, cs].reshape(tm, GDN_SLAB)
            y = u * cw_ref[CONV_W - 1:CONV_W, cs]
            for s in range(1, CONV_W):
                shifted = jnp.where(t_in < s, pltpu.roll(prev, tm - SUBLANES + s, axis=0), pltpu.roll(u, s, axis=0))
                y = y + shifted * cw_ref[CONV_W - 1 - s:CONV_W - s, cs]
            last_ref[:, :, cs] = u.reshape(groups, SUBLANES, GDN_SLAB)
            act_ref[:, cs] = finish(_silu(y), c).astype(BF16)

    conv_slabs, z_slabs = list(range(n_conv)), list(range(n_conv, n_all))
    order = []
    while conv_slabs or z_slabs:
        order += conv_slabs[:2]
        conv_slabs = conv_slabs[2:]
        order += z_slabs[:1]
        z_slabs = z_slabs[1:]
    pending = issue(order[0])
    for pos, c in enumerate(order):
        u = pending
        if pos + 1 < len(order):
            pending = issue(order[pos + 1])
        complete(c, u)


def _gdn_in(x2d, nw, w_in, w_ba, prev8, conv_wt, t):
    m = x2d.shape[0]
    if t == SUBLANES:
        tm = _pick_tile(m, GDN_TM_ROWS8)
        groups, tiles_per_row = tm // SUBLANES, 1
        before = pl.BlockSpec((groups, SUBLANES, GDN_CONV_DIM), lambda i: (i, 0, 0))
    else:
        tm = _pick_tile(t, GDN_TM)
        groups, tiles_per_row = 1, t // tm
        assert tm % SUBLANES == 0 and tm > SUBLANES and tm % min(GDN_ROWS, tm) == 0
        before = pl.BlockSpec((1, SUBLANES, GDN_CONV_DIM), lambda i: (i // tiles_per_row, 0, 0))
    kern = functools.partial(_gdn_in_kernel, tm=tm, groups=groups, tiles_per_row=tiles_per_row)
    return pl.pallas_call(
        kern,
        grid=(m // tm,),
        in_specs=[
            pl.BlockSpec((tm, D_MODEL), lambda i: (i, 0)),
            pl.BlockSpec((1, D_MODEL), lambda i: (0, 0)),
            pl.BlockSpec((D_MODEL, GDN_QKVZ_DIM), lambda i: (0, 0)),
            pl.BlockSpec((D_MODEL, LANES), lambda i: (0, 0)),
            before,
            pl.BlockSpec((CONV_W, GDN_CONV_DIM), lambda i: (0, 0)),
        ],
        out_specs=[
            pl.BlockSpec((tm, GDN_QKVZ_DIM), lambda i: (i, 0)),
            pl.BlockSpec((tm, LANES), lambda i: (i, 0)),
            pl.BlockSpec((groups, SUBLANES, GDN_CONV_DIM), lambda i: (i, 0, 0)),
        ],
        out_shape=[
            jax.ShapeDtypeStruct((m, GDN_QKVZ_DIM), BF16),
            jax.ShapeDtypeStruct((m, LANES), F32),
            jax.ShapeDtypeStruct((m // tm * groups, SUBLANES, GDN_CONV_DIM), F32),
        ],
        scratch_shapes=[pltpu.VMEM((tm, D_MODEL), BF16), pltpu.VMEM((SUBLANES, GDN_CONV_DIM), F32),
                        pltpu.VMEM((2, tm + SUBLANES, GDN_SLAB), F32)],
        compiler_params=pltpu.CompilerParams(dimension_semantics=("arbitrary",), vmem_limit_bytes=VMEM_LIMIT_BYTES),
        name="gdn_in",
    )(x2d, nw, w_in, w_ba, prev8, conv_wt)


def _gdn_chunk_kernel(act_ref, ba_ref, s0_ref, alog_ref, dtb_ref, o_ref, sout_ref, s_scr, *, chunk, gb, gt, n_steps):
    step = pl.program_id(1)
    C = chunk
    C2 = 2 * C
    carried = n_steps * gt > 1

    if carried:
        @pl.when(step == 0)
        def _():
            s_scr[...] = s0_ref[...]

    tril = _tril3(C, C)
    tril2 = _tril3(C2, C)
    row, col = _iota2(C2, C2)
    causal = ((row >= C) == (col >= C)) & (row >= col)
    top = lax.broadcasted_iota(jnp.int32, (C2, 1), 0) < C
    left = lax.broadcasted_iota(jnp.int32, (1, C2), 1) < C

    pairs = []
    for r in range(gb):
        ba = ba_ref[r]
        beta_l = _sigmoid(ba)
        g_l = -jnp.exp(alog_ref[...]) * _softplus(ba + dtb_ref[...])
        for j in range(gt):
            rows = slice(j * C, (j + 1) * C)
            beta_all = beta_l[rows]
            gc_all = _cumsum_rows(tril, g_l[rows], "nn")
            gct_all = _cumsum_rows(tril2, g_l[rows], "tn")
            for qh in range(GDN_QK_HEADS):
                q = act_ref[r, rows, qh * GDN_HEAD:(qh + 1) * GDN_HEAD].astype(F32)
                k = act_ref[r, rows, GDN_KEY_DIM + qh * GDN_HEAD:GDN_KEY_DIM + (qh + 1) * GDN_HEAD].astype(F32)
                h0 = 2 * qh
                h1 = h0 + 1
                a0, a1 = GDN_V_HEADS + h0, GDN_V_HEADS + h1
                beta_s = jnp.concatenate([beta_all[:, h0:h0 + 1], beta_all[:, h1:h1 + 1]], axis=0)
                gc_s = jnp.concatenate([gc_all[:, a0:a0 + 1], gc_all[:, a1:a1 + 1]], axis=0)
                gc_row = jnp.where(left, gct_all[a0:a0 + 1, :], gct_all[a1:a1 + 1, :])
                glast0 = gc_all[C - 1:C, a0:a0 + 1]
                glast1 = gc_all[C - 1:C, a1:a1 + 1]
                glast_s = jnp.where(top, glast0, glast1)
                decay = jnp.exp(jnp.where(causal, gc_s - gc_row, -jnp.inf))
                k_s = jnp.concatenate([k, k], axis=0)
                q_s = jnp.concatenate([q, q], axis=0)
                v0 = 2 * GDN_KEY_DIM + h0 * GDN_HEAD
                v_s = jnp.concatenate([act_ref[r, rows, v0:v0 + GDN_HEAD],
                                       act_ref[r, rows, v0 + GDN_HEAD:v0 + 2 * GDN_HEAD]], axis=0).astype(F32)
                egc = jnp.exp(gc_s)
                kb_s = k_s * beta_s
                pairs.append(dict(
                    r=r, j=j, h0=h0, h1=h1,
                    low=_bdot_nt(kb_s, k_s) * decay,
                    rhs=jnp.concatenate([v_s * beta_s, kb_s * egc], axis=1).astype(BF16),
                    attn=_bdot_nt(q_s, k_s) * decay, qd_s=q_s * egc, kd_s=k_s * jnp.exp(glast_s - gc_s),
                    eg0=jnp.exp(glast0), eg1=jnp.exp(glast1)))

    tinvs = _tri_inv_many([pr["low"] for pr in pairs], C2, C, half_rows=True)
    sols = [jnp.dot(t, pr["rhs"], preferred_element_type=F32) for t, pr in zip(tinvs, pairs)]

    def state_in(r, h):
        return s_scr[r, h] if carried else s0_ref[r, h]

    def state_out(r, h, val):
        if carried:
            s_scr[r, h] = val
        else:
            sout_ref[r, h] = val

    for r in range(gb):
        for j in range(gt):
            grp = [(pr, sol) for pr, sol in zip(pairs, sols) if pr["r"] == r and pr["j"] == j]
            rows = slice(j * C, (j + 1) * C)
            sts, wqs = [], []
            for pr, sol in grp:
                w_s = sol[:, GDN_HEAD:]
                st0 = state_in(r, pr["h0"])
                st1 = state_in(r, pr["h1"])
                sts.append((st0, st1))
                wqs.append((_bdot(jnp.concatenate([w_s[:C], pr["qd_s"][:C]], axis=0), st0),
                            _bdot(jnp.concatenate([w_s[C:], pr["qd_s"][C:]], axis=0), st1)))
            for (pr, sol), (st0, st1), (wq0, wq1) in zip(grp, sts, wqs):
                h0, h1 = pr["h0"], pr["h1"]
                u_s = sol[:, :GDN_HEAD]
                vn0 = u_s[:C] - wq0[:C]
                vn1 = u_s[C:] - wq1[:C]
                o_s = _bdot(pr["attn"], jnp.concatenate([vn0, vn1], axis=0))
                o_ref[r, rows, h0 * GDN_HEAD:(h0 + 1) * GDN_HEAD] = (o_s[:C] + wq0[C:]).astype(BF16)
                o_ref[r, rows, h1 * GDN_HEAD:(h1 + 1) * GDN_HEAD] = (o_s[C:] + wq1[C:]).astype(BF16)
                state_out(r, h0, st0 * pr["eg0"] + _bdot_tn(pr["kd_s"][:C], vn0))
                state_out(r, h1, st1 * pr["eg1"] + _bdot_tn(pr["kd_s"][C:], vn1))

    if carried:
        @pl.when(step == n_steps - 1)
        def _():
            sout_ref[...] = s_scr[...]


def _gdn_chunk(act, ba, s0, alog_row, dtb_row, chunk, gb, gt):
    b, t, _ = act.shape
    span = gt * chunk
    n_steps = t // span
    assert b % gb == 0 and t % span == 0 and (gb == 1 or n_steps == 1)
    kern = functools.partial(_gdn_chunk_kernel, chunk=chunk, gb=gb, gt=gt, n_steps=n_steps)
    state = pl.BlockSpec((gb, GDN_V_HEADS, GDN_HEAD, GDN_HEAD), lambda i, c: (i, 0, 0, 0))
    return pl.pallas_call(
        kern,
        grid=(b // gb, n_steps),
        in_specs=[
            pl.BlockSpec((gb, span, GDN_CONV_DIM), lambda i, c: (i, c, 0)),
            pl.BlockSpec((gb, span, LANES), lambda i, c: (i, c, 0)),
            state,
            pl.BlockSpec((1, LANES), lambda i, c: (0, 0)),
            pl.BlockSpec((1, LANES), lambda i, c: (0, 0)),
        ],
        out_specs=[pl.BlockSpec((gb, span, GDN_VALUE_DIM), lambda i, c: (i, c, 0)), state],
        out_shape=[
            jax.ShapeDtypeStruct((b, t, GDN_VALUE_DIM), BF16),
            jax.ShapeDtypeStruct((b, GDN_V_HEADS, GDN_HEAD, GDN_HEAD), F32),
        ],
        scratch_shapes=[pltpu.VMEM((gb if n_steps * gt > 1 else 1, GDN_V_HEADS, GDN_HEAD, GDN_HEAD), F32)],
        compiler_params=pltpu.CompilerParams(
            dimension_semantics=("parallel", "arbitrary"), vmem_limit_bytes=VMEM_LIMIT_BYTES),
        name="gdn_chunk",
    )(act, ba, s0, alog_row, dtb_row)


def _gdn_out_kernel(o_ref, gate_ref, x_ref, gw_ref, w_ref, y_ref):
    gw = gw_ref[...]
    acc = x_ref[...]
    for c in range(GDN_VALUE_DIM // OUT_SLAB):
        parts = []
        for h in range(c * OUT_SLAB // GDN_HEAD, (c + 1) * OUT_SLAB // GDN_HEAD):
            sl = slice(h * GDN_HEAD, (h + 1) * GDN_HEAD)
            parts.append((_rms(o_ref[:, sl].astype(F32), gw) * gate_ref[:, sl].astype(F32)).astype(BF16))
        acc = acc + jnp.dot(jnp.concatenate(parts, axis=1), w_ref[c * OUT_SLAB:(c + 1) * OUT_SLAB, :],
                            preferred_element_type=F32)
    y_ref[...] = acc


def _gdn_out(o2d, act2d, x2d, gw, w_out, tm):
    m = x2d.shape[0]
    z_block = GDN_CONV_DIM // GDN_VALUE_DIM
    return pl.pallas_call(
        _gdn_out_kernel,
        grid=(m // tm,),
        in_specs=[
            pl.BlockSpec((tm, GDN_VALUE_DIM), lambda i: (i, 0)),
            pl.BlockSpec((tm, GDN_VALUE_DIM), lambda i: (i, z_block)),
            pl.BlockSpec((tm, D_MODEL), lambda i: (i, 0)),
            pl.BlockSpec((1, GDN_HEAD), lambda i: (0, 0)),
            pl.BlockSpec((GDN_VALUE_DIM, D_MODEL), lambda i: (0, 0)),
        ],
        out_specs=pl.BlockSpec((tm, D_MODEL), lambda i: (i, 0)),
        out_shape=jax.ShapeDtypeStruct((m, D_MODEL), F32),
        compiler_params=pltpu.CompilerParams(
            dimension_semantics=("parallel",), vmem_limit_bytes=VMEM_LIMIT_BYTES),
        name="gdn_out",
    )(o2d, act2d, x2d, gw, w_out)


def _head_sum(x, ones_bd):
    return jnp.concatenate([_bdot(x[:, p * LANES:(p + 1) * LANES], ones_bd) for p in range(x.shape[1] // LANES)], axis=1)


def _rwkv_in_kernel(x_ref, xp_ref, sh_ref, nw_ref, mu_ref, wr_ref, wk_ref, wv_ref, wz_ref,
                    w0_ref, w1_ref, w2_ref, a0_ref, a1_ref, a2_ref, kk_ref, ka_ref, rk_ref, ones_ref,
                    r_out, k_out, v_out, ld_out, kn_out, b_out, z_out, bonus_out, last_out, *, tb, tt, n_t):
    t = pl.program_id(1)
    m = tb * tt
    nw = nw_ref[...]
    xn = _rms(x_ref[...].reshape(m, D_MODEL), nw)
    prev_last = _rms(xp_ref[:, SUBLANES - 1:SUBLANES, :], nw)
    first = jnp.where(t == 0, sh_ref[...], prev_last)
    first = jnp.broadcast_to(first, (tb, tt, D_MODEL)).reshape(m, D_MODEL)
    rolled = pltpu.roll(xn, 1, axis=0)
    rows = lax.broadcasted_iota(jnp.int32, (m, 1), 0)
    xprev = jnp.where((rows & (tt - 1)) == 0, first, rolled)
    xx = xprev - xn
    mu = mu_ref[...]

    mixed = [(xn + xx * mu[i:i + 1, :]).astype(BF16) for i in range(6)]
    w_mid = jnp.tanh(jnp.dot(mixed[4], w1_ref[...], preferred_element_type=F32)).astype(BF16)
    a_mid = jnp.dot(mixed[5], a1_ref[...], preferred_element_type=F32).astype(BF16)
    ones_bd = ones_ref[...]
    d = functools.partial(jnp.dot, preferred_element_type=F32)

    def project(cs):
        return (d(mixed[0], wr_ref[:, cs]), d(mixed[1], wk_ref[:, cs]), d(mixed[2], wv_ref[:, cs]),
                d(mixed[3], wz_ref[:, cs]), d(w_mid, w2_ref[:, cs]), d(a_mid, a2_ref[:, cs]))

    n_slabs = D_MODEL // RWKV_SLAB
    col = lambda c: slice(c * RWKV_SLAB, (c + 1) * RWKV_SLAB)
    nxt = project(col(0))
    for c in range(n_slabs):
        cs = col(c)
        r, k, v, z, w_lo, a_lo = nxt
        if c + 1 < n_slabs:
            nxt = project(col(c + 1))
        a = _sigmoid(a0_ref[:, cs] + a_lo)
        kk = k * kk_ref[:, cs]
        kn = kk * lax.rsqrt(_head_sum(kk * kk, ones_bd) + L2_EPS)
        k = k * (1.0 + (a - 1.0) * ka_ref[:, cs])
        r_out[:, cs] = r
        k_out[:, cs] = k
        v_out[:, cs] = v
        z_out[:, cs] = z
        ld_out[:, cs] = -DECAY_SCALE * _sigmoid(w0_ref[:, cs] + w_lo)
        kn_out[:, cs] = kn
        b_out[:, cs] = kn * a
        bonus_out[:, cs] = _head_sum(r * k * rk_ref[:, cs], ones_bd) * v

    @pl.when(t == n_t - 1)
    def _():
        last_out[...] = _rms(x_ref[:, tt - 1:tt, :], nw)


def _rwkv_in(x3d, shift, nw, mu, wr, wk, wv, wz, w0, w1, w2, a0, a1, a2, k_k, k_a, r_k, ones_bd, tb, tt):
    b, t, _ = x3d.shape
    n_t = t // tt
    tpb = tt // SUBLANES
    kern = functools.partial(_rwkv_in_kernel, tb=tb, tt=tt, n_t=n_t)
    row = lambda n: pl.BlockSpec((n, D_MODEL), lambda i, j: (0, 0))
    big = pl.BlockSpec((D_MODEL, D_MODEL), lambda i, j: (0, 0))
    tok = pl.BlockSpec((tb * tt, D_MODEL), lambda i, j: (i * n_t + j, 0))
    m = b * t
    outs = pl.pallas_call(
        kern,
        grid=(b // tb, n_t),
        in_specs=[
            pl.BlockSpec((tb, tt, D_MODEL), lambda i, j: (i, j, 0)),
            pl.BlockSpec((tb, SUBLANES, D_MODEL), lambda i, j: (i, jnp.maximum(j * tpb - 1, 0), 0)),
            pl.BlockSpec((tb, 1, D_MODEL), lambda i, j: (i, 0, 0)),
            row(1), row(6), big, big, big, big,
            row(1), pl.BlockSpec((D_MODEL, LORA), lambda i, j: (0, 0)), pl.BlockSpec((LORA, D_MODEL), lambda i, j: (0, 0)),
            row(1), pl.BlockSpec((D_MODEL, LORA), lambda i, j: (0, 0)), pl.BlockSpec((LORA, D_MODEL), lambda i, j: (0, 0)),
            row(1), row(1), row(1),
            pl.BlockSpec((LANES, LANES), lambda i, j: (0, 0)),
        ],
        out_specs=[tok] * 8 + [pl.BlockSpec((tb, 1, D_MODEL), lambda i, j: (i, 0, 0))],
        out_shape=[jax.ShapeDtypeStruct((m, D_MODEL), F32)] * 8 + [jax.ShapeDtypeStruct((b, 1, D_MODEL), F32)],
        compiler_params=pltpu.CompilerParams(
            dimension_semantics=("parallel", "arbitrary"), vmem_limit_bytes=VMEM_LIMIT_BYTES),
        name="rwkv_in",
    )(x3d, x3d, shift, nw, mu, wr, wk, wv, wz, w0, w1, w2, a0, a1, a2, k_k, k_a, r_k, ones_bd)
    return outs


def _rwkv_rec_kernel(r_ref, k_ref, v_ref, ld_ref, kn_ref, b_ref, s0_ref, y_ref, sout_ref, s_scr,
                     *, chunk, gb, gt, n_steps):
    step = pl.program_id(1)
    C = chunk
    C2 = 2 * C
    carried = n_steps * gt > 1

    def pair_state(rr, p):
        z = jnp.zeros((RWKV_HEAD, RWKV_HEAD), F32)
        return jnp.concatenate([jnp.concatenate([s0_ref[rr, 2 * p], z], axis=1),
                                jnp.concatenate([z, s0_ref[rr, 2 * p + 1]], axis=1)], axis=0)

    def put_state(rr, p, val):
        sout_ref[rr, 2 * p] = val[:RWKV_HEAD, :RWKV_HEAD]
        sout_ref[rr, 2 * p + 1] = val[RWKV_HEAD:, RWKV_HEAD:]

    if carried:
        @pl.when(step == 0)
        def _():
            for rr in range(gb):
                for p in range(RWKV_PAIRS):
                    s_scr[rr, p] = pair_state(rr, p)

    tril = _tril3(C, C)
    lane = lax.broadcasted_iota(jnp.int32, (1, LANES), 1)
    m_lo = lane < RWKV_HEAD
    rc, cc = _iota2(C, C2)
    tl = cc & (C - 1)
    strict_pair = rc > tl
    incl_pair = rc >= tl

    def stack(x):
        return jnp.concatenate([jnp.where(m_lo, x, 0.0), jnp.where(m_lo, 0.0, x)], axis=0)

    pairs = []
    for rr in range(gb):
        for j in range(gt):
            rows = slice(j * C, (j + 1) * C)
            for p in range(RWKV_PAIRS):
                sl = slice(p * LANES, (p + 1) * LANES)
                pairs.append(dict(rr=rr, j=j, p=p, rows=rows, sl=sl))
    for pr in pairs:
        pr["gc"] = _cumsum_rows(tril, ld_ref[pr["rr"], pr["rows"], pr["sl"]], "nn")
    for pr in pairs:
        at_ref = (pr["rr"], pr["rows"], pr["sl"])
        gc = pr["gc"]
        k = k_ref[at_ref]
        b = b_ref[at_ref]
        glast = gc[C - 1:C, :]
        e_neg = jnp.exp(-gc)
        e_end = jnp.exp(glast - gc)
        rt = r_ref[at_ref] * jnp.exp(gc)
        at = -kn_ref[at_ref] * jnp.exp(gc - ld_ref[at_ref])
        at_s = stack(at)
        pr.update(rt=rt, at_s=at_s, v_s=stack(v_ref[at_ref]).astype(BF16),
                  kh_s=stack(k * e_end), bh_s=stack(b * e_end).astype(BF16), eg=jnp.exp(glast))
        g = _bdot_nt(jnp.concatenate([at, rt], axis=0),
                     jnp.concatenate([stack(k * e_neg), stack(b * e_neg)], axis=0))
        ab = g[:C, C2:]
        pr["low"] = -jnp.concatenate([ab, ab], axis=0)
        pr["rb"] = jnp.where(incl_pair, g[C:, C2:], 0.0).astype(BF16)
        pr["rk"] = jnp.where(incl_pair, g[C:, :C2], 0.0).astype(BF16)
        pr["ak"] = jnp.where(strict_pair, g[:C, :C2], 0.0)
    for pr in pairs:
        kv = jnp.dot(jnp.concatenate([pr["ak"].astype(BF16), pr["rk"]], axis=0), pr["v_s"], preferred_element_type=F32)
        pr["rkv"] = kv[C:]
        pr["rhs"] = jnp.concatenate([pr["at_s"], stack(kv[:C])], axis=1).astype(BF16)

    tinvs = _tri_inv_many([pr["low"] for pr in pairs], C2, C, half_rows=False)
    sols = [jnp.dot(t, pr["rhs"], preferred_element_type=F32).astype(BF16) for t, pr in zip(tinvs, pairs)]
    for pr, sol in zip(pairs, sols):
        a2_s = sol[:, :LANES]
        u2_s = sol[:, LANES:]
        rb_sol = jnp.dot(pr["rb"], sol, preferred_element_type=F32)
        pr["r2"] = pr["rt"] + rb_sol[:, :LANES]
        pr["y2"] = pr["rkv"] + rb_sol[:, LANES:]
        pr["w2"] = _bdot_tn(a2_s, pr["bh_s"])
        pr["n2"] = _bdot_tn(jnp.concatenate([pr["v_s"], u2_s], axis=0),
                            jnp.concatenate([pr["kh_s"].astype(BF16), pr["bh_s"]], axis=0))

    for pr in pairs:
        rr, p = pr["rr"], pr["p"]
        st = s_scr[rr, p] if carried else pair_state(rr, p)
        y_ref[rr, pr["rows"], pr["sl"]] = _bdot_nt(pr["r2"], st) + pr["y2"]
        new = st * pr["eg"] + (_bdot(st, pr["w2"]) + pr["n2"])
        if carried:
            s_scr[rr, p] = new
        else:
            put_state(rr, p, new)

    if carried:
        @pl.when(step == n_steps - 1)
        def _():
            for rr in range(gb):
                for p in range(RWKV_PAIRS):
                    put_state(rr, p, s_scr[rr, p])


def _rwkv_rec(r, k, v, ld, kn, bvec, s0, chunk, gb, gt):
    b, t, _ = r.shape
    span = gt * chunk
    n_steps = t // span
    assert b % gb == 0 and t % span == 0 and (gb == 1 or n_steps == 1)
    kern = functools.partial(_rwkv_rec_kernel, chunk=chunk, gb=gb, gt=gt, n_steps=n_steps)
    tok = pl.BlockSpec((gb, span, D_MODEL), lambda i, c: (i, c, 0))
    st = pl.BlockSpec((gb, RWKV_HEADS, RWKV_HEAD, RWKV_HEAD), lambda i, c: (i, 0, 0, 0))
    return pl.pallas_call(
        kern,
        grid=(b // gb, n_steps),
        in_specs=[tok] * 6 + [st],
        out_specs=[tok, st],
        out_shape=[
            jax.ShapeDtypeStruct((b, t, D_MODEL), F32),
            jax.ShapeDtypeStruct((b, RWKV_HEADS, RWKV_HEAD, RWKV_HEAD), F32),
        ],
        scratch_shapes=[pltpu.VMEM((gb if n_steps * gt > 1 else 1, RWKV_PAIRS, LANES, LANES), F32)],
        compiler_params=pltpu.CompilerParams(
            dimension_semantics=("parallel", "arbitrary"), vmem_limit_bytes=VMEM_LIMIT_BYTES),
        name="rwkv_rec",
    )(r, k, v, ld, kn, bvec, s0)


def _rwkv_out_kernel(y_ref, z_ref, bonus_ref, x_ref, lw_ref, lb_ref, avg_ref, w_ref, fw_ref, out_ref):
    y = y_ref[...]
    avg = avg_ref[...]
    mean = _head_sum(y, avg)
    yc = y - mean
    var = _head_sum(yc * yc, avg)
    gn = yc * lax.rsqrt(var + GN_EPS) * lw_ref[...] + lb_ref[...]
    o = (gn + bonus_ref[...]) * _silu(z_ref[...])
    x2 = x_ref[...] + _bdot(o, w_ref[...])
    out_ref[...] = _rms(x2, fw_ref[...])


def _rwkv_out(y2d, z2d, bonus2d, x2d, lnx_w, lnx_b, avg_bd, w_o, fw, tm):
    m = x2d.shape[0]
    tok = pl.BlockSpec((tm, D_MODEL), lambda i: (i, 0))
    row = pl.BlockSpec((1, D_MODEL), lambda i: (0, 0))
    return pl.pallas_call(
        _rwkv_out_kernel,
        grid=(m // tm,),
        in_specs=[tok, tok, tok, tok, row, row,
                  pl.BlockSpec((LANES, LANES), lambda i: (0, 0)),
                  pl.BlockSpec((D_MODEL, D_MODEL), lambda i: (0, 0)), row],
        out_specs=tok,
        out_shape=jax.ShapeDtypeStruct((m, D_MODEL), F32),
        compiler_params=pltpu.CompilerParams(
            dimension_semantics=("parallel",), vmem_limit_bytes=VMEM_LIMIT_BYTES),
        name="rwkv_out",
    )(y2d, z2d, bonus2d, x2d, lnx_w, lnx_b, avg_bd, w_o, fw)


def _pick_tile(m, cap):
    t = min(m, cap)
    while m % t:
        t //= 2
    return t


def _step_groups(b, t, chunk, chunks_per_step):
    n_chunks = t // chunk
    if n_chunks > 1:
        return 1, _pick_tile(n_chunks, chunks_per_step)
    return _pick_tile(b, ROWS_PER_STEP), 1


def _trunk(x, gdn_s, gdn_conv, rwkv_s, rwkv_shift, p, gdn_chunk, rwkv_chunk):
    b, t, _ = x.shape
    m = b * t
    x2d = x.reshape(m, D_MODEL)
    prev8 = jnp.pad(gdn_conv, ((0, 0), (SUBLANES - (CONV_W - 1), 0), (0, 0)))
    act, ba, last8 = _gdn_in(x2d, p["nw0"], p["w_in"], p["w_ba"], prev8, p["conv_wt"], t)
    conv_new = last8.reshape(b, -1, SUBLANES, GDN_CONV_DIM)[:, -1, SUBLANES - (CONV_W - 1):, :]
    o, s_gdn = _gdn_chunk(act.reshape(b, t, GDN_QKVZ_DIM), ba.reshape(b, t, LANES), gdn_s,
                          p["alog_row"], p["dtb_row"], gdn_chunk, *_step_groups(b, t, gdn_chunk, GDN_CHUNKS_PER_STEP))
    tm2 = _pick_tile(m, OUT_TM)
    x1 = _gdn_out(o.reshape(m, GDN_VALUE_DIM), act, x2d, p["gn_w"], p["w_out"], tm2)
    if t >= RWKV_TM:
        tb, tt = 1, RWKV_TM
    elif t * b <= RWKV_TM_SHORT:
        tb, tt = b, t
    else:
        tb, tt = RWKV_TM_SHORT // t, t
    r, k, v, ld, kn, bvec, z, bonus, last = _rwkv_in(
        x1.reshape(b, t, D_MODEL), rwkv_shift.reshape(b, 1, D_MODEL), p["nw1"], p["mu"],
        p["wr"], p["wk"], p["wv"], p["wz"], p["w0"], p["w1"], p["w2"], p["a0"], p["a1"], p["a2"],
        p["k_k"], p["k_a"], p["r_k"], p["ones_bd"], tb, tt)
    sh3 = lambda a: a.reshape(b, t, D_MODEL)
    y, s_rwkv = _rwkv_rec(sh3(r), sh3(k), sh3(v), sh3(ld), sh3(kn), sh3(bvec), rwkv_s,
                          rwkv_chunk, *_step_groups(b, t, rwkv_chunk, RWKV_CHUNKS_PER_STEP))
    out = _rwkv_out(y.reshape(m, D_MODEL), z, bonus, x1, p["lnx_w"], p["lnx_b"], p["avg_bd"], p["w_o"], p["fw"], tm2)
    return (out.reshape(b, t, D_MODEL), s_gdn, conv_new, s_rwkv, last.reshape(b, D_MODEL))


def kernel(x_prompt, x_sample, state_gdn, state_gdn_conv, state_rwkv, state_rwkv_shift, meta_tokens, norm_w, final_norm_w, gdn_w_in, gdn_conv_w, gdn_a_log, gdn_dt_bias, gdn_norm_w, gdn_w_out, rwkv_mu, rwkv_w_rkvz, rwkv_w0, rwkv_w1, rwkv_w2, rwkv_a0, rwkv_a1, rwkv_a2, rwkv_k_k, rwkv_k_a, rwkv_r_k, rwkv_lnx_w, rwkv_lnx_b, rwkv_w_o):
    assert norm_w.shape[0] == 2 and gdn_w_in.shape[0] == 1 and rwkv_mu.shape[0] == 1
    row = lambda a: a.reshape(1, -1).astype(F32)
    w_in = gdn_w_in[0]
    w_ba = jnp.pad(w_in[:, GDN_QKVZ_DIM:], ((0, 0), (0, LANES - 2 * GDN_V_HEADS)))
    gate_row = lambda a: jnp.pad(a.astype(F32), (GDN_V_HEADS, LANES - 2 * GDN_V_HEADS)).reshape(1, LANES)
    rb, cb = jnp.arange(LANES)[:, None] // RWKV_HEAD, jnp.arange(LANES)[None, :] // RWKV_HEAD
    bd = (rb == cb)
    p = {
        "nw0": row(norm_w[0]), "nw1": row(norm_w[1]), "fw": row(final_norm_w),
        "w_in": w_in.astype(BF16), "w_ba": w_ba.astype(BF16),
        "conv_wt": gdn_conv_w[0].T.astype(F32),
        "alog_row": gate_row(gdn_a_log[0]), "dtb_row": gate_row(gdn_dt_bias[0]),
        "gn_w": row(gdn_norm_w[0]), "w_out": gdn_w_out[0].astype(BF16),
        "mu": rwkv_mu[0].astype(F32),
        "wr": rwkv_w_rkvz[0, 0].astype(BF16), "wk": rwkv_w_rkvz[0, 1].astype(BF16),
        "wv": rwkv_w_rkvz[0, 2].astype(BF16), "wz": rwkv_w_rkvz[0, 3].astype(BF16),
        "w0": row(rwkv_w0[0]), "w1": rwkv_w1[0].astype(BF16), "w2": rwkv_w2[0].astype(BF16),
        "a0": row(rwkv_a0[0]), "a1": rwkv_a1[0].astype(BF16), "a2": rwkv_a2[0].astype(BF16),
        "k_k": row(rwkv_k_k[0]), "k_a": row(rwkv_k_a[0]), "r_k": row(rwkv_r_k[0]),
        "lnx_w": row(rwkv_lnx_w[0]), "lnx_b": row(rwkv_lnx_b[0]), "w_o": rwkv_w_o[0].astype(BF16),
        "ones_bd": bd.astype(BF16), "avg_bd": (bd.astype(F32) / RWKV_HEAD).astype(BF16),
    }
    bp = x_prompt.shape[0]
    zeros = lambda *s: jnp.zeros(s, F32)
    _, m_gdn, m_conv, m_rwkv, m_shift = _trunk(
        meta_tokens.astype(F32)[None], zeros(1, GDN_V_HEADS, GDN_HEAD, GDN_HEAD), zeros(1, CONV_W - 1, GDN_CONV_DIM),
        zeros(1, RWKV_HEADS, RWKV_HEAD, RWKV_HEAD), zeros(1, D_MODEL), p, N_META, N_META)
    rep = lambda a: jnp.broadcast_to(a, (bp,) + a.shape[1:])
    y_p, p_gdn, p_conv, p_rwkv, p_shift = _trunk(
        x_prompt, rep(m_gdn), rep(m_conv), rep(m_rwkv), rep(m_shift), p, GDN_CHUNK, RWKV_CHUNK)
    ts = x_sample.shape[1]
    y_s, s_gdn, s_conv, s_rwkv, s_shift = _trunk(
        x_sample, state_gdn[0], state_gdn_conv[0], state_rwkv[0], state_rwkv_shift[0], p, ts, ts)
    return (y_p, y_s, p_gdn[None], p_conv[None], p_rwkv[None], p_shift[None],
            s_gdn[None], s_conv[None], s_rwkv[None], s_shift[None])
```
